```python
import math
import jax, jax.numpy as jnp
from jax import lax
import numpy as np

D_MODEL = 2048
BATCH = 2
SEQ = 4096
DEPTH = 1

D_MIX = D_MODEL
D_S5 = D_MIX // 2
D_GLA = D_MIX - D_S5
S5_GROUP = 16
S5_GROUPS = D_S5 // S5_GROUP
S5_STATE = 64
GLA_HEADS = 4
GLA_DK = D_GLA // 2
GLA_DV = D_GLA
GLA_HK = GLA_DK // GLA_HEADS
GLA_HV = GLA_DV // GLA_HEADS
GLA_RANK = 16
GLA_TAU = 16.0
GLA_CHUNK = 64
EPS = 1e-6
D_IN = 2 * D_S5 + 2 * GLA_DK + 2 * GLA_DV + GLA_RANK

kernel_name = "hybrid_s5_gla_parallel_heads"


def rmsnorm(x, w):
    xf = x.astype(jnp.float32)
    return xf * lax.rsqrt(jnp.mean(xf * xf, axis=-1, keepdims=True) + EPS) * w.astype(jnp.float32)


def s5_branch(u, A_re, A_im, B_re, B_im, C_re, C_im, D, log_dt, glu_w, glu_b):
    bsz, L, _ = u.shape
    ug = u.reshape(bsz, L, S5_GROUPS, S5_GROUP)
    a_re = jnp.minimum(A_re.astype(jnp.float32), -1e-4)
    A = lax.complex(a_re, A_im.astype(jnp.float32))
    dt = jnp.exp(log_dt.astype(jnp.float32))[:, None]
    A_bar = jnp.exp(A * dt)
    Bc = lax.complex(B_re.astype(jnp.float32), B_im.astype(jnp.float32))
    B_bar = ((A_bar - 1.0) / A)[..., None] * Bc
    Bu = jnp.einsum('blgh,gph->blgp', ug.astype(jnp.complex64), B_bar)
    a_seq = jnp.broadcast_to(A_bar, Bu.shape)

    def combine(e1, e2):
        a1, b1 = e1
        a2, b2 = e2
        return a1 * a2, a2 * b1 + b2

    _, states = lax.associative_scan(combine, (a_seq, Bu), axis=1)
    Cc = lax.complex(C_re.astype(jnp.float32), C_im.astype(jnp.float32))
    y = jnp.real(jnp.einsum('blgp,ghp->blgh', states, Cc))
    y = y + D.astype(jnp.float32).reshape(S5_GROUPS, S5_GROUP) * ug
    y = jax.nn.gelu(y.reshape(bsz, L, D_S5))
    return y * jax.nn.sigmoid(y @ glu_w.astype(jnp.float32) + glu_b.astype(jnp.float32))


def gla_branch(q, k, v, g_low, gate_up, gate_bias, norm_w):
    bsz, L, _ = q.shape
    n = L // GLA_CHUNK

    def heads(t, d):
        return t.reshape(bsz, n, GLA_CHUNK, GLA_HEADS, d).transpose(0, 3, 1, 2, 4)

    log_g = jax.nn.log_sigmoid(g_low @ gate_up.astype(jnp.float32)
                               + gate_bias.astype(jnp.float32)) / GLA_TAU
    q = heads(q, GLA_HK) * (GLA_HK ** -0.5)
    k = heads(k, GLA_HK)
    v = heads(v, GLA_HV)
    log_g = heads(log_g, GLA_HK)
    b = jnp.cumsum(log_g, axis=3)
    b_last = b[:, :, :, -1:, :]
    q_e = q * jnp.exp(b)
    k_e = k * jnp.exp(-b)
    k_tail = k * jnp.exp(b_last - b)
    mask = jnp.tril(jnp.ones((GLA_CHUNK, GLA_CHUNK), dtype=bool))
    attn = jnp.where(mask, jnp.einsum('bhncd,bhnsd->bhncs', q_e, k_e), 0.0)
    o_intra = jnp.einsum('bhncs,bhnse->bhnce', attn, v)

    kv = jnp.einsum('bhncd,bhnce->bhnde', k_tail, v)
    decay = jnp.exp(b_last[:, :, :, 0, :])

    def step(S, inp):
        kv_n, dec_n = inp
        return dec_n[..., None] * S + kv_n, S

    S0 = jnp.zeros((bsz, GLA_HEADS, GLA_HK, GLA_HV), jnp.float32)
    _, S_prev = lax.scan(step, S0, (kv.transpose(2, 0, 1, 3, 4), decay.transpose(2, 0, 1, 3)))
    S_prev = S_prev.transpose(1, 2, 0, 3, 4)
    o = o_intra + jnp.einsum('bhncd,bhnde->bhnce', q_e, S_prev)
    o = rmsnorm(o, norm_w)
    return o.transpose(0, 2, 3, 1, 4).reshape(bsz, L, GLA_DV)


def setup_inputs(seed: int = 0) -> dict:
    key = jax.random.key(seed)
    ks = jax.random.split(key, 20)
    f32 = jnp.float32
    nrm = lambda k, shape, s: jax.random.normal(k, shape, f32) * s
    x = jax.random.normal(ks[0], (BATCH, SEQ, D_MODEL), f32)
    pre_norm_w = 1.0 + nrm(ks[1], (DEPTH, D_MODEL), 0.02)
    w_in = nrm(ks[2], (DEPTH, D_MODEL, D_IN), D_MODEL ** -0.5)
    s5_A_re = -0.5 + nrm(ks[3], (DEPTH, S5_GROUPS, S5_STATE), 0.01)
    s5_A_im = (math.pi * jnp.arange(S5_STATE, dtype=f32))[None, None, :] \
        + nrm(ks[4], (DEPTH, S5_GROUPS, S5_STATE), 0.01)
    bs = (2.0 * S5_GROUP) ** -0.5
    cs = (2.0 * S5_STATE) ** -0.5
    s5_B_re = nrm(ks[5], (DEPTH, S5_GROUPS, S5_STATE, S5_GROUP), bs)
    s5_B_im = nrm(ks[6], (DEPTH, S5_GROUPS, S5_STATE, S5_GROUP), bs)
    s5_C_re = nrm(ks[7], (DEPTH, S5_GROUPS, S5_GROUP, S5_STATE), cs)
    s5_C_im = nrm(ks[8], (DEPTH, S5_GROUPS, S5_GROUP, S5_STATE), cs)
    s5_D = nrm(ks[9], (DEPTH, D_S5), 1.0)
    s5_log_dt = jax.random.uniform(ks[10], (DEPTH, S5_GROUPS), f32,
                                   math.log(1e-3), math.log(1e-1))
    s5_glu_w = nrm(ks[11], (DEPTH, D_S5, D_S5), D_S5 ** -0.5)
    s5_glu_b = nrm(ks[12], (DEPTH, D_S5), 0.01)
    gla_gate_up = nrm(ks[13], (DEPTH, GLA_RANK, GLA_DK), GLA_RANK ** -0.5)
    gla_gate_bias = nrm(ks[14], (DEPTH, GLA_DK), 0.1)
    gla_norm_w = 1.0 + nrm(ks[15], (DEPTH, GLA_HV), 0.02)
    w_out = nrm(ks[16], (DEPTH, D_MIX, D_MODEL), D_MIX ** -0.5)
    post_norm_w = 1.0 + nrm(ks[17], (DEPTH, D_MODEL), 0.02)
    return {"x": x, "pre_norm_w": pre_norm_w, "w_in": w_in,
            "s5_A_re": s5_A_re, "s5_A_im": s5_A_im, "s5_B_re": s5_B_re, "s5_B_im": s5_B_im,
            "s5_C_re": s5_C_re, "s5_C_im": s5_C_im, "s5_D": s5_D, "s5_log_dt": s5_log_dt,
            "s5_glu_w": s5_glu_w, "s5_glu_b": s5_glu_b,
            "gla_gate_up": gla_gate_up, "gla_gate_bias": gla_gate_bias, "gla_norm_w": gla_norm_w,
            "w_out": w_out, "post_norm_w": post_norm_w}


def reference(x, pre_norm_w, w_in, s5_A_re, s5_A_im, s5_B_re, s5_B_im, s5_C_re, s5_C_im,
              s5_D, s5_log_dt, s5_glu_w, s5_glu_b, gla_gate_up, gla_gate_bias, gla_norm_w,
              w_out, post_norm_w):
    out_dtype = x.dtype
    resid = x.astype(jnp.float32)
    splits = [D_S5, 2 * D_S5, 2 * D_S5 + GLA_DK, 2 * D_S5 + 2 * GLA_DK,
              2 * D_S5 + 2 * GLA_DK + GLA_DV, 2 * D_S5 + 2 * GLA_DK + 2 * GLA_DV]
    for l in range(DEPTH):
        h = rmsnorm(resid, pre_norm_w[l])
        proj = h @ w_in[l].astype(jnp.float32)
        s5_u, s5_z, q, k, v, gla_z, g_low = jnp.split(proj, splits, axis=-1)
        y_s5 = s5_branch(s5_u, s5_A_re[l], s5_A_im[l], s5_B_re[l], s5_B_im[l],
                         s5_C_re[l], s5_C_im[l], s5_D[l], s5_log_dt[l],
                         s5_glu_w[l], s5_glu_b[l]) * jax.nn.silu(s5_z)
        y_gla = gla_branch(q, k, v, g_low, gla_gate_up[l], gla_gate_bias[l],
                           gla_norm_w[l]) * jax.nn.silu(gla_z)
        mixed = jnp.concatenate([y_s5, y_gla], axis=-1) @ w_out[l].astype(jnp.float32)
        resid = resid + rmsnorm(mixed, post_norm_w[l])
    return resid.astype(out_dtype)
```

```python
import functools

import jax
import jax.numpy as jnp
from jax import lax
from jax.experimental import pallas as pl
from jax.experimental.pallas import tpu as pltpu

F32 = jnp.float32
BF16 = jnp.bfloat16

S5_GROUP = 16
GLA_HEADS = 4
GLA_TAU = 16.0
GLA_CHUNK = 64
EPS = 1e-6

LANES = 128
S5_T = 16
VMEM_LIMIT = 56 * 1024 * 1024


def _in_proj_kernel(x_ref, nw_ref, w_ref, o_ref, h_ref, *, n_slabs):
    @pl.when(pl.program_id(1) == 0)
    def _():
        xf = x_ref[...]
        ms = jnp.mean(xf * xf, axis=-1, keepdims=True)
        h_ref[...] = (xf * lax.rsqrt(ms + EPS) * nw_ref[...]).astype(BF16)

    res = jnp.dot(h_ref[...], w_ref[...], preferred_element_type=F32)
    for s in range(n_slabs):
        o_ref[s] = res[:, s * LANES:(s + 1) * LANES].astype(BF16)


def _in_proj(x2d, norm_w, w_pad, *, tm, tn):
    m, d = x2d.shape
    n_pad = w_pad.shape[1]
    n_slabs = tn // LANES
    return pl.pallas_call(
        functools.partial(_in_proj_kernel, n_slabs=n_slabs),
        grid=(m // tm, n_pad // tn),
        in_specs=[
            pl.BlockSpec((tm, d), lambda i, j: (i, 0)),
            pl.BlockSpec((1, d), lambda i, j: (0, 0)),
            pl.BlockSpec((d, tn), lambda i, j: (0, j)),
        ],
        out_specs=pl.BlockSpec((n_slabs, tm, LANES), lambda i, j: (j, i, 0)),
        out_shape=jax.ShapeDtypeStruct((n_pad // LANES, m, LANES), BF16),
        scratch_shapes=[pltpu.VMEM((tm, d), BF16)],
        compiler_params=pltpu.CompilerParams(
            dimension_semantics=("parallel", "arbitrary"),
            vmem_limit_bytes=VMEM_LIMIT),
        name="in_proj",
    )(x2d, norm_w, w_pad)


def _s5_kernel(x_ref, kb_ref, wz_ref, wy_ref, a_ref, d_ref, o_ref, m_ref, z_ref, sp_ref,
               *, rows_per_seq, n_seq):
    nt = S5_T
    half = a_ref.shape[-1] // 2

    @pl.when(pl.program_id(0) == 0)
    def _():
        for n in range(nt // 2):
            m_ref[(2 * n + 1) * LANES:(2 * n + 2) * LANES, 2 * n * LANES:(2 * n + 1) * LANES] = (
                jnp.zeros((LANES, LANES), BF16))

    for tau in range(nt):
        blk = kb_ref[tau]
        for s in range(nt - tau):
            t = s + tau
            m_ref[s * LANES:(s + 1) * LANES, t * LANES:(t + 1) * LANES] = blk

    z_ref[...] = jnp.dot(x_ref[...], wz_ref[...], preferred_element_type=F32)

    a_re = a_ref[:, :half]
    a_im = a_ref[:, half:]

    def body(c, carry):
        new = []
        for b in range(n_seq):
            s_re, s_im = carry[2 * b], carry[2 * b + 1]
            r = b * rows_per_seq + c
            sp_ref[pl.ds(r, 1), :half] = s_re
            sp_ref[pl.ds(r, 1), half:] = s_im
            z_re = z_ref[pl.ds(r, 1), :half]
            z_im = z_ref[pl.ds(r, 1), half:]
            new.append(a_re * s_re - a_im * s_im + z_re)
            new.append(a_re * s_im + a_im * s_re + z_im)
        return tuple(new)

    zero = jnp.zeros((1, half), F32)
    lax.fori_loop(0, rows_per_seq, body, (zero,) * (2 * n_seq), unroll=8)

    spb = sp_ref[...].astype(BF16)
    wide = 2 * LANES
    for n in range(nt // 2):
        cols = slice(n * wide, (n + 1) * wide)
        kk = (n + 1) * wide
        acc = jnp.dot(x_ref[:, :kk], m_ref[:kk, cols], preferred_element_type=F32)
        acc = acc + jnp.dot(spb, wy_ref[:, cols], preferred_element_type=F32)
        acc = acc + d_ref[:, cols] * x_ref[:, cols].astype(F32)
        o_ref[:, cols] = jax.nn.gelu(acc).astype(BF16)


def _s5(p3, kb, wz, wy, abar, dtile, *, n_seq):
    n_j, nt, _, _ = kb.shape
    rows = p3.shape[1]
    width = nt * LANES
    n_state = abar.shape[-1]
    return pl.pallas_call(
        functools.partial(_s5_kernel, rows_per_seq=rows // n_seq, n_seq=n_seq),
        grid=(n_j,),
        in_specs=[
            pl.BlockSpec((None, rows, width), lambda j: (j, 0, 0)),
            pl.BlockSpec((None, nt, LANES, LANES), lambda j: (j, 0, 0, 0)),
            pl.BlockSpec((None, width, n_state), lambda j: (j, 0, 0)),
            pl.BlockSpec((None, n_state, width), lambda j: (j, 0, 0)),
            pl.BlockSpec((None, 1, n_state), lambda j: (j, 0, 0)),
            pl.BlockSpec((None, 1, width), lambda j: (j, 0, 0)),
        ],
        out_specs=pl.BlockSpec((None, rows, width), lambda j: (j, 0, 0)),
        out_shape=jax.ShapeDtypeStruct((n_j, rows, width), BF16),
        scratch_shapes=[
            pltpu.VMEM((width, width), BF16),
            pltpu.VMEM((rows, n_state), F32),
            pltpu.VMEM((rows, n_state), F32),
        ],
        compiler_params=pltpu.CompilerParams(
            dimension_semantics=("arbitrary",),
            vmem_limit_bytes=VMEM_LIMIT),
        name="s5",
    )(p3, kb, wz, wy, abar, dtile)


def _s5_operators(a_re, a_im, b_re, b_im, c_re, c_im, d_skip, log_dt):
    g, p = a_re.shape
    h = S5_GROUP
    gps = LANES // h
    n_j = g // gps
    nt = S5_T
    a = lax.complex(jnp.minimum(a_re.astype(F32), -1e-4), a_im.astype(F32))
    dt = jnp.exp(log_dt.astype(F32))[:, None]
    a_bar = jnp.exp(a * dt)
    b_bar = ((a_bar - 1.0) / a)[..., None] * lax.complex(b_re.astype(F32), b_im.astype(F32))
    cc = lax.complex(c_re.astype(F32), c_im.astype(F32))
    k = jnp.arange(nt + 1, dtype=F32)
    pw = jnp.exp((a * dt)[None] * k[:, None, None])
    eye = jnp.eye(gps, dtype=F32)

    ker = jnp.real(jnp.einsum('ghp,tgp,gpk->tgkh', cc, pw[:nt], b_bar))
    ker = ker.reshape(nt, n_j, gps, h, h)
    kb = jnp.einsum('tJjkh,jm->Jtjkmh', ker, eye).reshape(n_j, nt, LANES, LANES)

    wz_c = pw[nt - 1::-1][:nt][..., None] * b_bar[None]
    wz_c = jnp.stack([jnp.real(wz_c), jnp.imag(wz_c)], axis=0)
    wz_c = wz_c.reshape(2, nt, n_j, gps, p, h)
    wz = jnp.einsum('rsJjpk,jm->Jsjkrmp', wz_c, eye).reshape(n_j, nt * LANES, 2 * gps * p)

    wy_c = cc[None] * pw[1:nt + 1][:, :, None, :]
    wy_c = jnp.stack([jnp.real(wy_c), -jnp.imag(wy_c)], axis=0)
    wy_c = wy_c.reshape(2, nt, n_j, gps, h, p)
    wy = jnp.einsum('rtJjhp,jm->Jrmptjh', wy_c, eye).reshape(n_j, 2 * gps * p, nt * LANES)

    a_t = pw[nt].reshape(n_j, gps * p)
    abar = jnp.concatenate([jnp.real(a_t), jnp.imag(a_t)], axis=-1)[:, None, :]
    dtile = jnp.tile(d_skip.astype(F32).reshape(n_j, 1, LANES), (1, 1, nt))
    return kb.astype(BF16), wz.astype(BF16), wy.astype(BF16), abar, dtile


def _log_sigmoid(x):
    return jnp.minimum(x, 0.0) - jnp.log1p(jnp.exp(-jnp.abs(x)))


def _gla_kernel(q_ref, k_ref, v_ref, gz_ref, gl_ref, gu_ref, gb_ref, nw_ref, o_ref, st_ref,
                *, chunk):
    rows = q_ref.shape[0]
    n_chunks = rows // chunk
    dk = q_ref.shape[1]

    @pl.when(pl.program_id(2) == 0)
    def _():
        st_ref[...] = jnp.zeros_like(st_ref)

    zg = jnp.dot(gl_ref[...], gu_ref[...], preferred_element_type=F32) + gb_ref[...]
    lg = _log_sigmoid(zg) * (1.0 / GLA_TAU)

    row = lax.broadcasted_iota(jnp.int32, (rows, rows), 0)
    col = lax.broadcasted_iota(jnp.int32, (rows, rows), 1)
    same = (row // chunk) == (col // chunk)
    causal = same & (col <= row)
    tri = jnp.where(causal, 1.0, 0.0).astype(BF16)
    blk = jnp.where(same, 1.0, 0.0).astype(BF16)

    hi = lg.astype(BF16)
    lo = (lg - hi.astype(F32)).astype(BF16)
    b = (jnp.dot(tri, hi, preferred_element_type=F32)
         + jnp.dot(tri, lo, preferred_element_type=F32))
    b_last = (jnp.dot(blk, hi, preferred_element_type=F32)
              + jnp.dot(blk, lo, preferred_element_type=F32))

    q = q_ref[...].astype(F32)
    k = k_ref[...].astype(F32)
    q_e = (q * (dk ** -0.5) * jnp.exp(b)).astype(BF16)
    k_e = (k * jnp.exp(-b)).astype(BF16)
    k_t = (k * jnp.exp(b_last - b)).astype(BF16)
    decay = jnp.exp(b_last)
    v = jnp.concatenate([v_ref[0], v_ref[1]], axis=1)

    nt_dims = (((1,), (1,)), ((), ()))
    tn_dims = (((0,), (0,)), ((), ()))
    attn = lax.dot_general(q_e, k_e, nt_dims, preferred_element_type=F32)
    attn = jnp.where(causal, attn, 0.0).astype(BF16)
    o_intra = jnp.dot(attn, v, preferred_element_type=F32)

    st = st_ref[...]
    outs = []
    for n in range(n_chunks):
        rs = slice(n * chunk, (n + 1) * chunk)
        outs.append(lax.dot_general(q_e[rs], st.astype(BF16), nt_dims, preferred_element_type=F32))
        kv_t = lax.dot_general(v[rs], k_t[rs], tn_dims, preferred_element_type=F32)
        st = decay[n * chunk:n * chunk + 1, :] * st + kv_t
    st_ref[...] = st

    o = o_intra + jnp.concatenate(outs, axis=0)
    ms = jnp.mean(o * o, axis=-1, keepdims=True)
    o = o * lax.rsqrt(ms + EPS) * nw_ref[...]
    gz = jnp.concatenate([gz_ref[0], gz_ref[1]], axis=1).astype(F32)
    y = (o * (gz * jax.nn.sigmoid(gz))).astype(BF16)
    o_ref[0] = y[:, :LANES]
    o_ref[1] = y[:, LANES:]


def _gla(proj, gate_up_pad, gate_bias, norm_w, *, n_seq, seq_len, rows, slab_q, slab_k, slab_v,
         slab_gz, slab_gl):
    m = proj.shape[1]
    nb = seq_len // rows
    dk = LANES
    dv = 2 * LANES
    tok = lambda b, h, n: b * nb + n
    return pl.pallas_call(
        functools.partial(_gla_kernel, chunk=GLA_CHUNK),
        grid=(n_seq, GLA_HEADS, nb),
        in_specs=[
            pl.BlockSpec((None, rows, LANES), lambda b, h, n: (slab_q + h, tok(b, h, n), 0)),
            pl.BlockSpec((None, rows, LANES), lambda b, h, n: (slab_k + h, tok(b, h, n), 0)),
            pl.BlockSpec((2, rows, LANES), lambda b, h, n: (slab_v // 2 + h, tok(b, h, n), 0)),
            pl.BlockSpec((2, rows, LANES), lambda b, h, n: (slab_gz // 2 + h, tok(b, h, n), 0)),
            pl.BlockSpec((None, rows, LANES), lambda b, h, n: (slab_gl, tok(b, h, n), 0)),
            pl.BlockSpec((LANES, dk), lambda b, h, n: (0, h)),
            pl.BlockSpec((1, dk), lambda b, h, n: (0, h)),
            pl.BlockSpec((1, dv), lambda b, h, n: (0, 0)),
        ],
        out_specs=pl.BlockSpec((2, rows, LANES), lambda b, h, n: (h, tok(b, h, n), 0)),
        out_shape=jax.ShapeDtypeStruct((GLA_HEADS * 2, m, LANES), BF16),
        scratch_shapes=[pltpu.VMEM((dv, dk), F32)],
        compiler_params=pltpu.CompilerParams(
            dimension_semantics=("parallel", "parallel", "arbitrary"),
            vmem_limit_bytes=VMEM_LIMIT),
        name="gla",
    )(proj, proj, proj, proj, proj, gate_up_pad, gate_bias, norm_w)


def _out_proj_kernel(g_ref, z_ref, yg_ref, x_ref, gw_ref, gb_ref, wo_ref, pw_ref, o_ref):
    n_s5 = g_ref.shape[0]
    d_s5 = n_s5 * LANES
    g = jnp.concatenate([g_ref[s] for s in range(n_s5)], axis=1)
    gate = jax.nn.sigmoid(jnp.dot(g, gw_ref[...], preferred_element_type=F32) + gb_ref[...])
    z = jnp.concatenate([z_ref[s] for s in range(n_s5)], axis=1).astype(F32)
    y_s5 = (g.astype(F32) * gate * (z * jax.nn.sigmoid(z))).astype(BF16)
    y_gla = jnp.concatenate([yg_ref[s] for s in range(yg_ref.shape[0])], axis=1)
    mixed = (jnp.dot(y_s5, wo_ref[:d_s5, :], preferred_element_type=F32)
             + jnp.dot(y_gla, wo_ref[d_s5:, :], preferred_element_type=F32))
    ms = jnp.mean(mixed * mixed, axis=-1, keepdims=True)
    o_ref[...] = x_ref[...] + mixed * lax.rsqrt(ms + EPS) * pw_ref[...]


def _out_proj(g_slabs, proj, y_gla, x2d, glu_w, glu_b, w_out, post_w, *, tm, slab_z):
    m, d = x2d.shape
    n_s5 = g_slabs.shape[0]
    n_gla = y_gla.shape[0]
    d_s5 = n_s5 * LANES
    d_mix = w_out.shape[0]
    return pl.pallas_call(
        _out_proj_kernel,
        grid=(m // tm,),
        in_specs=[
            pl.BlockSpec((n_s5, tm, LANES), lambda i: (0, i, 0)),
            pl.BlockSpec((n_s5, tm, LANES), lambda i: (slab_z // n_s5, i, 0)),
            pl.BlockSpec((n_gla, tm, LANES), lambda i: (0, i, 0)),
            pl.BlockSpec((tm, d), lambda i: (i, 0)),
            pl.BlockSpec((d_s5, d_s5), lambda i: (0, 0)),
            pl.BlockSpec((1, d_s5), lambda i: (0, 0)),
            pl.BlockSpec((d_mix, d), lambda i: (0, 0)),
            pl.BlockSpec((1, d), lambda i: (0, 0)),
        ],
        out_specs=pl.BlockSpec((tm, d), lambda i: (i, 0)),
        out_shape=jax.ShapeDtypeStruct((m, d), F32),
        compiler_params=pltpu.CompilerParams(
            dimension_semantics=("parallel",),
            vmem_limit_bytes=VMEM_LIMIT),
        name="out_proj",
    )(g_slabs, proj, y_gla, x2d, glu_w, glu_b, w_out, post_w)


def kernel(x, pre_norm_w, w_in, s5_A_re, s5_A_im, s5_B_re, s5_B_im, s5_C_re, s5_C_im, s5_D,
           s5_log_dt, s5_glu_w, s5_glu_b, gla_gate_up, gla_gate_bias, gla_norm_w, w_out,
           post_norm_w):
    bsz, seq_len, d_model = x.shape
    depth = w_in.shape[0]
    d_in = w_in.shape[2]
    d_s5 = s5_glu_w.shape[1]
    rank, d_gk = gla_gate_up.shape[1:]
    d_gv = GLA_HEADS * gla_norm_w.shape[1]
    m = bsz * seq_len
    assert d_s5 % LANES == 0 and d_gk == GLA_HEADS * LANES and d_gv == GLA_HEADS * 2 * LANES
    assert d_in == 2 * d_s5 + 2 * d_gk + 2 * d_gv + rank and rank <= LANES
    assert seq_len % (S5_T * 8) == 0 and w_out.shape[1] == 2 * d_s5

    n_s5 = d_s5 // LANES
    slab_z = n_s5
    slab_q = 2 * n_s5
    slab_k = slab_q + d_gk // LANES
    slab_v = slab_k + d_gk // LANES
    slab_gz = slab_v + d_gv // LANES
    slab_gl = slab_gz + d_gv // LANES
    tn = 6 * LANES
    n_pad = -(-(d_in) // tn) * tn

    resid = x.astype(F32).reshape(m, d_model)
    for l in range(depth):
        w_pad = jnp.pad(w_in[l].astype(BF16), ((0, 0), (0, n_pad - d_in)))
        proj = _in_proj(resid, pre_norm_w[l].astype(F32)[None], w_pad, tm=1024, tn=tn)

        kb, wz, wy, abar, dtile = _s5_operators(
            s5_A_re[l], s5_A_im[l], s5_B_re[l], s5_B_im[l], s5_C_re[l], s5_C_im[l],
            s5_D[l], s5_log_dt[l])
        p3 = proj.reshape(proj.shape[0], m // S5_T, S5_T * LANES)
        g3 = _s5(p3, kb, wz, wy, abar, dtile, n_seq=bsz)
        g_slabs = g3.reshape(n_s5, m, LANES)

        gate_up_pad = jnp.pad(gla_gate_up[l].astype(BF16), ((0, LANES - rank), (0, 0)))
        y_gla = _gla(proj, gate_up_pad, gla_gate_bias[l].astype(F32)[None],
                     gla_norm_w[l].astype(F32)[None], n_seq=bsz, seq_len=seq_len, rows=512,
                     slab_q=slab_q, slab_k=slab_k, slab_v=slab_v, slab_gz=slab_gz, slab_gl=slab_gl)

        resid = _out_proj(g_slabs, proj, y_gla, resid, s5_glu_w[l].astype(BF16),
                          s5_glu_b[l].astype(F32)[None], w_out[l].astype(BF16),
                          post_norm_w[l].astype(F32)[None], tm=512, slab_z=slab_z)
    return resid.reshape(bsz, seq_len, d_model).astype(x.dtype)
```

```python
import functools

import jax
import jax.numpy as jnp
from jax import lax
from jax.experimental import pallas as pl
from jax.experimental.pallas import tpu as pltpu

F32 = jnp.float32
BF16 = jnp.bfloat16

S5_GROUP = 16
GLA_HEADS = 4
GLA_TAU = 16.0
GLA_CHUNK = 64
EPS = 1e-6

LANES = 128
S5_T = 16
VMEM_LIMIT = 56 * 1024 * 1024


def _cmul(ar, ai, br, bi):
    return ar * br - ai * bi, ar * bi + ai * br


def _s5_ops_kernel(tp_ref, np_ref, kb_ref, u_ref, v_ref, ab_ref, *, n_state):
    nt = S5_T
    half = LANES // 2

    def a_bar(are_raw, aim, ldt):
        are = jnp.minimum(are_raw, -1e-4)
        dt = jnp.exp(ldt)
        mag = jnp.exp(are * dt)
        return are, mag * jnp.cos(aim * dt), mag * jnp.sin(aim * dt)

    are, abr, abi = a_bar(tp_ref[0], tp_ref[1], tp_ref[2])
    aim = tp_ref[1]
    den = are * are + aim * aim
    nr = abr - 1.0
    fr = (nr * are + abi * aim) / den
    fi = (abi * are - nr * aim) / den
    xr, xi = _cmul(fr, fi, tp_ref[3], tp_ref[4])

    row = lax.broadcasted_iota(jnp.int32, (LANES, LANES), 0)
    lane = lax.broadcasted_iota(jnp.int32, (LANES, LANES), 1)
    own_half = ((row // S5_GROUP) % 2) == (lane // half)
    same_group = (row // S5_GROUP) == (lane // S5_GROUP)
    first_half = lane < half

    c_stack = jnp.concatenate([np_ref[3], np_ref[4]], axis=0)
    for tau in range(nt):
        lhs = jnp.where(first_half, xr, -xi)
        ker = jnp.dot(lhs, c_stack, preferred_element_type=F32, precision=lax.Precision.HIGHEST)
        kb_ref[tau] = jnp.where(same_group, ker, 0.0).astype(BF16)
        u_ref[nt - 1 - tau, 0] = jnp.where(own_half, xr, 0.0).astype(BF16)
        u_ref[nt - 1 - tau, 1] = jnp.where(own_half, xi, 0.0).astype(BF16)
        if tau + 1 < nt:
            xr, xi = _cmul(xr, xi, abr, abi)

    pr, pi = abr, abi
    for _ in range(4):
        pr, pi = _cmul(pr, pi, pr, pi)
    assert nt == 16
    pairs = LANES // (2 * S5_GROUP)
    for part, val in enumerate((pr, pi)):
        for a in range(pairs):
            r0 = 2 * a * S5_GROUP
            piece = jnp.where(first_half[:1], val[r0:r0 + 1], val[r0 + S5_GROUP:r0 + S5_GROUP + 1])
            c0 = part * (n_state // 2) + a * LANES
            ab_ref[:, c0:c0 + LANES] = piece

    _, nbr, nbi = a_bar(np_ref[0], np_ref[1], np_ref[2])
    cr, ci = np_ref[3], np_ref[4]
    lane_n = lax.broadcasted_iota(jnp.int32, (half, LANES), 1)
    parity = (lane_n // S5_GROUP) % 2
    pr, pi = nbr, nbi
    for t in range(nt):
        wr, wi = _cmul(cr, ci, pr, pi)
        for q in range(2):
            rows = slice(q * half, (q + 1) * half)
            v_ref[t, 0, rows, :] = jnp.where(parity == q, wr, 0.0).astype(BF16)
            v_ref[t, 1, rows, :] = jnp.where(parity == q, -wi, 0.0).astype(BF16)
        if t + 1 < nt:
            pr, pi = _cmul(pr, pi, nbr, nbi)


def _s5_ops(a_re, a_im, b_re, b_im, c_re, c_im, log_dt):
    g, p = a_re.shape
    h = S5_GROUP
    gps = LANES // h
    n_j = g // gps
    assert 2 * p == LANES and g % gps == 0
    n_state = 2 * gps * p
    f = lambda t: t.astype(F32)

    def t_arr(t):
        t = f(t).reshape(n_j, gps, -1, 1, p)
        return jnp.broadcast_to(t, (n_j, gps, h, 2, p)).reshape(n_j, LANES, LANES)

    def n_arr(t):
        t = jnp.swapaxes(f(t).reshape(n_j, gps, -1, p), 1, 3)
        t = jnp.swapaxes(jnp.broadcast_to(t, (n_j, p, h, gps)), 2, 3)
        return t.reshape(n_j, p, LANES)

    ldt = jnp.broadcast_to(f(log_dt)[:, None], (g, p))
    tpar = jnp.stack([t_arr(a_re), t_arr(a_im), t_arr(ldt),
                      t_arr(jnp.swapaxes(b_re, 1, 2)), t_arr(jnp.swapaxes(b_im, 1, 2))], axis=1)
    npar = jnp.stack([n_arr(a_re), n_arr(a_im), n_arr(ldt), n_arr(c_re), n_arr(c_im)], axis=1)
    nt = S5_T
    return pl.pallas_call(
        functools.partial(_s5_ops_kernel, n_state=n_state),
        grid=(n_j,),
        in_specs=[
            pl.BlockSpec((None, 5, LANES, LANES), lambda j: (j, 0, 0, 0)),
            pl.BlockSpec((None, 5, p, LANES), lambda j: (j, 0, 0, 0)),
        ],
        out_specs=[
            pl.BlockSpec((None, nt, LANES, LANES), lambda j: (j, 0, 0, 0)),
            pl.BlockSpec((None, nt, 2, LANES, LANES), lambda j: (j, 0, 0, 0, 0)),
            pl.BlockSpec((None, nt, 2, LANES, LANES), lambda j: (j, 0, 0, 0, 0)),
            pl.BlockSpec((None, 1, n_state), lambda j: (j, 0, 0)),
        ],
        out_shape=[
            jax.ShapeDtypeStruct((n_j, nt, LANES, LANES), BF16),
            jax.ShapeDtypeStruct((n_j, nt, 2, LANES, LANES), BF16),
            jax.ShapeDtypeStruct((n_j, nt, 2, LANES, LANES), BF16),
            jax.ShapeDtypeStruct((n_j, 1, n_state), F32),
        ],
        compiler_params=pltpu.CompilerParams(dimension_semantics=("parallel",)),
        name="s5_ops",
    )(tpar, npar)


def _in_proj_kernel(x_ref, nw_ref, w_ref, wl_ref, u_ref, p_ref, l_ref, h_ref, s_ref):
    j = pl.program_id(1)
    n_slabs = p_ref.shape[0]
    rows = u_ref.shape[1]

    @pl.when(j == 0)
    def _():
        xf = x_ref[...]
        ms = jnp.mean(xf * xf, axis=-1, keepdims=True)
        h_ref[...] = (xf * lax.rsqrt(ms + EPS) * nw_ref[...]).astype(BF16)

    res = jnp.dot(h_ref[...], w_ref[...], preferred_element_type=F32)

    @pl.when(j == 0)
    def _():
        for s in range(n_slabs):
            s_ref[s] = res[:, s * LANES:(s + 1) * LANES]
        for s in range(n_slabs):
            for t in range(S5_T):
                u_ref[s, :, t * LANES:(t + 1) * LANES] = (
                    s_ref[s, pl.ds(t, rows, stride=S5_T), :].astype(BF16))

    @pl.when(j > 0)
    def _():
        for s in range(n_slabs):
            p_ref[s] = res[:, s * LANES:(s + 1) * LANES].astype(BF16)

    @pl.when(j == pl.num_programs(1) - 1)
    def _():
        l_ref[...] = jnp.dot(h_ref[...], wl_ref[...], preferred_element_type=F32).astype(BF16)


def _in_proj(x2d, norm_w, w_main, w_low, *, tm, tn):
    m, d = x2d.shape
    n_steps = w_main.shape[1] // tn
    n_slabs = tn // LANES
    return pl.pallas_call(
        _in_proj_kernel,
        grid=(m // tm, n_steps),
        in_specs=[
            pl.BlockSpec((tm, d), lambda i, j: (i, 0)),
            pl.BlockSpec((1, d), lambda i, j: (0, 0)),
            pl.BlockSpec((d, tn), lambda i, j: (0, j)),
            pl.BlockSpec((d, LANES), lambda i, j: (0, 0)),
        ],
        out_specs=[
            pl.BlockSpec((n_slabs, tm // S5_T, S5_T * LANES), lambda i, j: (0, i, 0)),
            pl.BlockSpec((n_slabs, tm, LANES), lambda i, j: (jnp.maximum(j - 1, 0), i, 0)),
            pl.BlockSpec((tm, LANES), lambda i, j: (i, 0)),
        ],
        out_shape=[
            jax.ShapeDtypeStruct((n_slabs, m // S5_T, S5_T * LANES), BF16),
            jax.ShapeDtypeStruct(((n_steps - 1) * n_slabs, m, LANES), BF16),
            jax.ShapeDtypeStruct((m, LANES), BF16),
        ],
        scratch_shapes=[pltpu.VMEM((tm, d), BF16), pltpu.VMEM((n_slabs, tm, LANES), F32)],
        compiler_params=pltpu.CompilerParams(
            dimension_semantics=("parallel", "arbitrary"),
            vmem_limit_bytes=VMEM_LIMIT),
        name="in_proj",
    )(x2d, norm_w, w_main, w_low)


def _s5_kernel(x_ref, kb_ref, u_ref, v_ref, a_ref, d_ref, o_ref,
               m_ref, wz_ref, wy_ref, z_ref, sp_ref, y_ref, *, rows_per_seq, n_seq):
    nt = S5_T
    rows = x_ref.shape[0]
    half = a_ref.shape[-1] // 2
    pairs = half // LANES
    sub = LANES // pairs

    @pl.when(pl.program_id(0) == 0)
    def _():
        for n in range(nt // 2):
            m_ref[(2 * n + 1) * LANES:(2 * n + 2) * LANES, 2 * n * LANES:(2 * n + 1) * LANES] = (
                jnp.zeros((LANES, LANES), BF16))
        wz_ref[...] = jnp.zeros_like(wz_ref)

    for tau in range(nt):
        blk = kb_ref[tau]
        for s in range(nt - tau):
            t = s + tau
            m_ref[s * LANES:(s + 1) * LANES, t * LANES:(t + 1) * LANES] = blk

    lane = lax.broadcasted_iota(jnp.int32, (LANES, LANES), 1)
    for s in range(nt):
        for part in range(2):
            for a in range(pairs):
                c0 = part * half + a * LANES
                wz_ref[s * LANES + a * sub:s * LANES + (a + 1) * sub, c0:c0 + LANES] = (
                    u_ref[s, part, a * sub:(a + 1) * sub, :])
    for t in range(nt):
        for part in range(2):
            blk = v_ref[t, part]
            for a in range(pairs):
                r0 = part * half + a * LANES
                wy_ref[r0:r0 + LANES, t * LANES:(t + 1) * LANES] = (
                    jnp.where((lane // sub) == a, blk, jnp.zeros_like(blk)))

    z_ref[...] = jnp.dot(x_ref[...], wz_ref[...], preferred_element_type=F32)

    a_re = a_ref[:, :half]
    a_im = a_ref[:, half:]

    def body(c, carry):
        new = []
        for b in range(n_seq):
            s_re, s_im = carry[2 * b], carry[2 * b + 1]
            r = b * rows_per_seq + c
            sp_ref[pl.ds(r, 1), :half] = s_re
            sp_ref[pl.ds(r, 1), half:] = s_im
            z_re = z_ref[pl.ds(r, 1), :half]
            z_im = z_ref[pl.ds(r, 1), half:]
            new.append(a_re * s_re - a_im * s_im + z_re)
            new.append(a_re * s_im + a_im * s_re + z_im)
        return tuple(new)

    zero = jnp.zeros((1, half), F32)
    lax.fori_loop(0, rows_per_seq, body, (zero,) * (2 * n_seq), unroll=8)

    spb = sp_ref[...].astype(BF16)
    wide = 2 * LANES
    for n in range(nt // 2):
        cols = slice(n * wide, (n + 1) * wide)
        kk = (n + 1) * wide
        acc = jnp.dot(x_ref[:, :kk], m_ref[:kk, cols], preferred_element_type=F32)
        acc = acc + jnp.dot(spb, wy_ref[:, cols], preferred_element_type=F32)
        acc = acc + d_ref[:, cols] * x_ref[:, cols].astype(F32)
        y = jax.nn.gelu(acc)
        for k in range(2):
            y_ref[pl.ds(2 * n + k, rows, stride=nt), :] = y[:, k * LANES:(k + 1) * LANES]
    o_ref[...] = y_ref[...].astype(BF16)


def _s5(u2, kb, u_op, v_op, abar, dtile, *, n_seq):
    n_j, nt, _, _ = kb.shape
    rows = u2.shape[1]
    width = nt * LANES
    n_state = abar.shape[-1]
    return pl.pallas_call(
        functools.partial(_s5_kernel, rows_per_seq=rows // n_seq, n_seq=n_seq),
        grid=(n_j,),
        in_specs=[
            pl.BlockSpec((None, rows, width), lambda j: (j, 0, 0)),
            pl.BlockSpec((None, nt, LANES, LANES), lambda j: (j, 0, 0, 0)),
            pl.BlockSpec((None, nt, 2, LANES, LANES), lambda j: (j, 0, 0, 0, 0)),
            pl.BlockSpec((None, nt, 2, LANES, LANES), lambda j: (j, 0, 0, 0, 0)),
            pl.BlockSpec((None, 1, n_state), lambda j: (j, 0, 0)),
            pl.BlockSpec((None, 1, width), lambda j: (j, 0, 0)),
        ],
        out_specs=pl.BlockSpec((None, rows * nt, LANES), lambda j: (j, 0, 0)),
        out_shape=jax.ShapeDtypeStruct((n_j, rows * nt, LANES), BF16),
        scratch_shapes=[
            pltpu.VMEM((width, width), BF16),
            pltpu.VMEM((width, n_state), BF16),
            pltpu.VMEM((n_state, width), BF16),
            pltpu.VMEM((rows, n_state), F32),
            pltpu.VMEM((rows, n_state), F32),
            pltpu.VMEM((rows * nt, LANES), F32),
        ],
        compiler_params=pltpu.CompilerParams(
            dimension_semantics=("arbitrary",),
            vmem_limit_bytes=VMEM_LIMIT),
        name="s5",
    )(u2, kb, u_op, v_op, abar, dtile)


def _log_sigmoid(x):
    return jnp.minimum(x, 0.0) - jnp.log1p(jnp.exp(-jnp.abs(x)))


def _gla_kernel(q_ref, k_ref, v_ref, gz_ref, gl_ref, gu_ref, gb_ref, nw_ref, o_ref, st_ref,
                *, chunk):
    rows = q_ref.shape[0]
    n_chunks = rows // chunk
    dk = q_ref.shape[1]

    @pl.when(pl.program_id(2) == 0)
    def _():
        st_ref[...] = jnp.zeros_like(st_ref)

    zg = jnp.dot(gl_ref[...], gu_ref[...], preferred_element_type=F32) + gb_ref[...]
    lg = _log_sigmoid(zg) * (1.0 / GLA_TAU)

    row = lax.broadcasted_iota(jnp.int32, (rows, rows), 0)
    col = lax.broadcasted_iota(jnp.int32, (rows, rows), 1)
    same = (row // chunk) == (col // chunk)
    causal = same & (col <= row)
    tri = jnp.where(causal, 1.0, 0.0).astype(BF16)
    blk = jnp.where(same, 1.0, 0.0).astype(BF16)

    hi = lg.astype(BF16)
    lo = (lg - hi.astype(F32)).astype(BF16)
    b = (jnp.dot(tri, hi, preferred_element_type=F32)
         + jnp.dot(tri, lo, preferred_element_type=F32))
    b_last = (jnp.dot(blk, hi, preferred_element_type=F32)
              + jnp.dot(blk, lo, preferred_element_type=F32))

    q = q_ref[...].astype(F32)
    k = k_ref[...].astype(F32)
    q_e = (q * (dk ** -0.5) * jnp.exp(b)).astype(BF16)
    k_e = (k * jnp.exp(-b)).astype(BF16)
    k_t = (k * jnp.exp(b_last - b)).astype(BF16)
    decay = jnp.exp(b_last)
    v = jnp.concatenate([v_ref[0], v_ref[1]], axis=1)

    nt_dims = (((1,), (1,)), ((), ()))
    tn_dims = (((0,), (0,)), ((), ()))
    attn = lax.dot_general(q_e, k_e, nt_dims, preferred_element_type=F32)
    attn = jnp.where(causal, attn, 0.0).astype(BF16)
    o_intra = jnp.dot(attn, v, preferred_element_type=F32)

    st = st_ref[...]
    outs = []
    for n in range(n_chunks):
        rs = slice(n * chunk, (n + 1) * chunk)
        outs.append(lax.dot_general(q_e[rs], st.astype(BF16), nt_dims, preferred_element_type=F32))
        kv_t = lax.dot_general(v[rs], k_t[rs], tn_dims, preferred_element_type=F32)
        st = decay[n * chunk:n * chunk + 1, :] * st + kv_t
    st_ref[...] = st

    o = o_intra + jnp.concatenate(outs, axis=0)
    ms = jnp.mean(o * o, axis=-1, keepdims=True)
    o = o * lax.rsqrt(ms + EPS) * nw_ref[...]
    gz = jnp.concatenate([gz_ref[0], gz_ref[1]], axis=1).astype(F32)
    y = (o * (gz * jax.nn.sigmoid(gz))).astype(BF16)
    o_ref[0] = y[:, :LANES]
    o_ref[1] = y[:, LANES:]


def _gla(proj, g_low, gate_up_pad, gate_bias, norm_w, *, n_seq, seq_len, rows, slab_q, slab_k,
         slab_v, slab_gz):
    m = proj.shape[1]
    nb = seq_len // rows
    dk = LANES
    dv = 2 * LANES
    tok = lambda b, h, n: b * nb + n
    return pl.pallas_call(
        functools.partial(_gla_kernel, chunk=GLA_CHUNK),
        grid=(n_seq, GLA_HEADS, nb),
        in_specs=[
            pl.BlockSpec((None, rows, LANES), lambda b, h, n: (slab_q + h, tok(b, h, n), 0)),
            pl.BlockSpec((None, rows, LANES), lambda b, h, n: (slab_k + h, tok(b, h, n), 0)),
            pl.BlockSpec((2, rows, LANES), lambda b, h, n: (slab_v // 2 + h, tok(b, h, n), 0)),
            pl.BlockSpec((2, rows, LANES), lambda b, h, n: (slab_gz // 2 + h, tok(b, h, n), 0)),
            pl.BlockSpec((rows, LANES), lambda b, h, n: (tok(b, h, n), 0)),
            pl.BlockSpec((LANES, dk), lambda b, h, n: (0, h)),
            pl.BlockSpec((1, dk), lambda b, h, n: (0, h)),
            pl.BlockSpec((1, dv), lambda b, h, n: (0, 0)),
        ],
        out_specs=pl.BlockSpec((2, rows, LANES), lambda b, h, n: (h, tok(b, h, n), 0)),
        out_shape=jax.ShapeDtypeStruct((GLA_HEADS * 2, m, LANES), BF16),
        scratch_shapes=[pltpu.VMEM((dv, dk), F32)],
        compiler_params=pltpu.CompilerParams(
            dimension_semantics=("parallel", "parallel", "arbitrary"),
            vmem_limit_bytes=VMEM_LIMIT),
        name="gla",
    )(proj, proj, proj, proj, g_low, gate_up_pad, gate_bias, norm_w)


def _out_proj_kernel(g_ref, z_ref, yg_ref, x_ref, gw_ref, gb_ref, wo_ref, pw_ref, o_ref):
    n_s5 = g_ref.shape[0]
    d_s5 = n_s5 * LANES
    g = jnp.concatenate([g_ref[s] for s in range(n_s5)], axis=1)
    gate = jax.nn.sigmoid(jnp.dot(g, gw_ref[...], preferred_element_type=F32) + gb_ref[...])
    z = jnp.concatenate([z_ref[s] for s in range(n_s5)], axis=1).astype(F32)
    y_s5 = (g.astype(F32) * gate * (z * jax.nn.sigmoid(z))).astype(BF16)
    y_gla = jnp.concatenate([yg_ref[s] for s in range(yg_ref.shape[0])], axis=1)
    mixed = (jnp.dot(y_s5, wo_ref[:d_s5, :], preferred_element_type=F32)
             + jnp.dot(y_gla, wo_ref[d_s5:, :], preferred_element_type=F32))
    ms = jnp.mean(mixed * mixed, axis=-1, keepdims=True)
    o_ref[...] = x_ref[...] + mixed * lax.rsqrt(ms + EPS) * pw_ref[...]


def _out_proj(g_slabs, proj, y_gla, x2d, glu_w, glu_b, w_out, post_w, *, tm, slab_z):
    m, d = x2d.shape
    n_s5 = g_slabs.shape[0]
    n_gla = y_gla.shape[0]
    d_s5 = n_s5 * LANES
    d_mix = w_out.shape[0]
    return pl.pallas_call(
        _out_proj_kernel,
        grid=(m // tm,),
        in_specs=[
            pl.BlockSpec((n_s5, tm, LANES), lambda i: (0, i, 0)),
            pl.BlockSpec((n_s5, tm, LANES), lambda i: (slab_z // n_s5, i, 0)),
            pl.BlockSpec((n_gla, tm, LANES), lambda i: (0, i, 0)),
            pl.BlockSpec((tm, d), lambda i: (i, 0)),
            pl.BlockSpec((d_s5, d_s5), lambda i: (0, 0)),
            pl.BlockSpec((1, d_s5), lambda i: (0, 0)),
            pl.BlockSpec((d_mix, d), lambda i: (0, 0)),
            pl.BlockSpec((1, d), lambda i: (0, 0)),
        ],
        out_specs=pl.BlockSpec((tm, d), lambda i: (i, 0)),
        out_shape=jax.ShapeDtypeStruct((m, d), F32),
        compiler_params=pltpu.CompilerParams(
            dimension_semantics=("parallel",),
            vmem_limit_bytes=VMEM_LIMIT),
        name="out_proj",
    )(g_slabs, proj, y_gla, x2d, glu_w, glu_b, w_out, post_w)


def kernel(x, pre_norm_w, w_in, s5_A_re, s5_A_im, s5_B_re, s5_B_im, s5_C_re, s5_C_im, s5_D,
           s5_log_dt, s5_glu_w, s5_glu_b, gla_gate_up, gla_gate_bias, gla_norm_w, w_out,
           post_norm_w):
    bsz, seq_len, d_model = x.shape
    depth = w_in.shape[0]
    d_in = w_in.shape[2]
    d_s5 = s5_glu_w.shape[1]
    rank, d_gk = gla_gate_up.shape[1:]
    d_gv = GLA_HEADS * gla_norm_w.shape[1]
    m = bsz * seq_len
    d_main = 2 * d_s5 + 2 * d_gk + 2 * d_gv
    tn = d_s5
    assert d_s5 % LANES == 0 and d_gk == GLA_HEADS * LANES and d_gv == GLA_HEADS * 2 * LANES
    assert d_in == d_main + rank and rank <= LANES and d_main % tn == 0 and 2 * d_gk == tn
    assert seq_len % (S5_T * 8) == 0 and w_out.shape[1] == 2 * d_s5

    n_s5 = d_s5 // LANES
    slab_z = 0
    slab_q = n_s5
    slab_k = slab_q + d_gk // LANES
    slab_v = slab_k + d_gk // LANES
    slab_gz = slab_v + d_gv // LANES

    resid = x.astype(F32).reshape(m, d_model)
    for l in range(depth):
        kb, u_op, v_op, abar = _s5_ops(s5_A_re[l], s5_A_im[l], s5_B_re[l], s5_B_im[l],
                                       s5_C_re[l], s5_C_im[l], s5_log_dt[l])
        dtile = jnp.tile(s5_D[l].astype(F32).reshape(n_s5, 1, LANES), (1, 1, S5_T))

        w_main = w_in[l, :, :d_main].astype(BF16)
        w_low = jnp.pad(w_in[l, :, d_main:].astype(BF16), ((0, 0), (0, LANES - rank)))
        u2, proj, g_low = _in_proj(resid, pre_norm_w[l].astype(F32)[None], w_main, w_low,
                                   tm=1024, tn=tn)

        g_slabs = _s5(u2, kb, u_op, v_op, abar, dtile, n_seq=bsz)

        gate_up_pad = jnp.pad(gla_gate_up[l].astype(BF16), ((0, LANES - rank), (0, 0)))
        y_gla = _gla(proj, g_low, gate_up_pad, gla_gate_bias[l].astype(F32)[None],
                     gla_norm_w[l].astype(F32)[None], n_seq=bsz, seq_len=seq_len, rows=512,
                     slab_q=slab_q, slab_k=slab_k, slab_v=slab_v, slab_gz=slab_gz)

        resid = _out_proj(g_slabs, proj, y_gla, resid, s5_glu_w[l].astype(BF16),
                          s5_glu_b[l].astype(F32)[None], w_out[l].astype(BF16),
                          post_norm_w[l].astype(F32)[None], tm=512, slab_z=slab_z)
    return resid.reshape(bsz, seq_len, d_model).astype(x.dtype)
```

```python
import functools

import jax
import jax.numpy as jnp
from jax import lax
from jax.experimental import pallas as pl
from jax.experimental.pallas import tpu as pltpu

F32 = jnp.float32
BF16 = jnp.bfloat16

S5_GROUP = 16
GLA_HEADS = 4
GLA_TAU = 16.0
GLA_CHUNK = 64
EPS = 1e-6

LANES = 128
S5_T = 16
VMEM_LIMIT = 56 * 1024 * 1024


def _cmul(ar, ai, br, bi):
    return ar * br - ai * bi, ar * bi + ai * br


def _s5_ops_kernel(tp_ref, np_ref, kb_ref, u_ref, v_ref, ab_ref, *, n_state):
    nt = S5_T
    half = LANES // 2

    def a_bar(are_raw, aim, ldt):
        are = jnp.minimum(are_raw, -1e-4)
        dt = jnp.exp(ldt)
        mag = jnp.exp(are * dt)
        return are, mag * jnp.cos(aim * dt), mag * jnp.sin(aim * dt)

    are, abr, abi = a_bar(tp_ref[0], tp_ref[1], tp_ref[2])
    aim = tp_ref[1]
    den = are * are + aim * aim
    nr = abr - 1.0
    fr = (nr * are + abi * aim) / den
    fi = (abi * are - nr * aim) / den
    xr, xi = _cmul(fr, fi, tp_ref[3], tp_ref[4])

    row = lax.broadcasted_iota(jnp.int32, (LANES, LANES), 0)
    lane = lax.broadcasted_iota(jnp.int32, (LANES, LANES), 1)
    own_half = ((row // S5_GROUP) % 2) == (lane // half)
    same_group = (row // S5_GROUP) == (lane // S5_GROUP)
    first_half = lane < half

    c_stack = jnp.concatenate([np_ref[3], np_ref[4]], axis=0)
    for tau in range(nt):
        lhs = jnp.where(first_half, xr, -xi)
        ker = jnp.dot(lhs, c_stack, preferred_element_type=F32, precision=lax.Precision.HIGHEST)
        kb_ref[tau] = jnp.where(same_group, ker, 0.0).astype(BF16)
        u_ref[nt - 1 - tau, 0] = jnp.where(own_half, xr, 0.0).astype(BF16)
        u_ref[nt - 1 - tau, 1] = jnp.where(own_half, xi, 0.0).astype(BF16)
        if tau + 1 < nt:
            xr, xi = _cmul(xr, xi, abr, abi)

    pr, pi = abr, abi
    for _ in range(4):
        pr, pi = _cmul(pr, pi, pr, pi)
    assert nt == 16
    pairs = LANES // (2 * S5_GROUP)
    for part, val in enumerate((pr, pi)):
        for a in range(pairs):
            r0 = 2 * a * S5_GROUP
            piece = jnp.where(first_half[:1], val[r0:r0 + 1], val[r0 + S5_GROUP:r0 + S5_GROUP + 1])
            c0 = part * (n_state // 2) + a * LANES
            ab_ref[:, c0:c0 + LANES] = piece

    _, nbr, nbi = a_bar(np_ref[0], np_ref[1], np_ref[2])
    cr, ci = np_ref[3], np_ref[4]
    lane_n = lax.broadcasted_iota(jnp.int32, (half, LANES), 1)
    parity = (lane_n // S5_GROUP) % 2
    pr, pi = nbr, nbi
    for t in range(nt):
        wr, wi = _cmul(cr, ci, pr, pi)
        for q in range(2):
            rows = slice(q * half, (q + 1) * half)
            v_ref[t, 0, rows, :] = jnp.where(parity == q, wr, 0.0).astype(BF16)
            v_ref[t, 1, rows, :] = jnp.where(parity == q, -wi, 0.0).astype(BF16)
        if t + 1 < nt:
            pr, pi = _cmul(pr, pi, nbr, nbi)


def _s5_ops(a_re, a_im, b_re, b_im, c_re, c_im, log_dt):
    g, p = a_re.shape
    h = S5_GROUP
    gps = LANES // h
    n_j = g // gps
    assert 2 * p == LANES and g % gps == 0
    n_state = 2 * gps * p
    f = lambda t: t.astype(F32)

    def t_arr(t):
        t = f(t).reshape(n_j, gps, -1, 1, p)
        return jnp.broadcast_to(t, (n_j, gps, h, 2, p)).reshape(n_j, LANES, LANES)

    def n_arr(t):
        t = jnp.swapaxes(f(t).reshape(n_j, gps, -1, p), 1, 3)
        t = jnp.swapaxes(jnp.broadcast_to(t, (n_j, p, h, gps)), 2, 3)
        return t.reshape(n_j, p, LANES)

    ldt = jnp.broadcast_to(f(log_dt)[:, None], (g, p))
    tpar = jnp.stack([t_arr(a_re), t_arr(a_im), t_arr(ldt),
                      t_arr(jnp.swapaxes(b_re, 1, 2)), t_arr(jnp.swapaxes(b_im, 1, 2))], axis=1)
    npar = jnp.stack([n_arr(a_re), n_arr(a_im), n_arr(ldt), n_arr(c_re), n_arr(c_im)], axis=1)
    nt = S5_T
    return pl.pallas_call(
        functools.partial(_s5_ops_kernel, n_state=n_state),
        grid=(n_j,),
        in_specs=[
            pl.BlockSpec((None, 5, LANES, LANES), lambda j: (j, 0, 0, 0)),
            pl.BlockSpec((None, 5, p, LANES), lambda j: (j, 0, 0, 0)),
        ],
        out_specs=[
            pl.BlockSpec((None, nt, LANES, LANES), lambda j: (j, 0, 0, 0)),
            pl.BlockSpec((None, nt, 2, LANES, LANES), lambda j: (j, 0, 0, 0, 0)),
            pl.BlockSpec((None, nt, 2, LANES, LANES), lambda j: (j, 0, 0, 0, 0)),
            pl.BlockSpec((None, 1, n_state), lambda j: (j, 0, 0)),
        ],
        out_shape=[
            jax.ShapeDtypeStruct((n_j, nt, LANES, LANES), BF16),
            jax.ShapeDtypeStruct((n_j, nt, 2, LANES, LANES), BF16),
            jax.ShapeDtypeStruct((n_j, nt, 2, LANES, LANES), BF16),
            jax.ShapeDtypeStruct((n_j, 1, n_state), F32),
        ],
        compiler_params=pltpu.CompilerParams(dimension_semantics=("parallel",)),
        name="s5_ops",
    )(tpar, npar)


def _in_proj_kernel(x_ref, nw_ref, w_ref, wl_ref, u_ref, p_ref, l_ref, h_ref, s_ref):
    j = pl.program_id(1)
    n_slabs = p_ref.shape[0]
    rows = u_ref.shape[1]

    @pl.when(j == 0)
    def _():
        xf = x_ref[...]
        ms = jnp.mean(xf * xf, axis=-1, keepdims=True)
        h_ref[...] = (xf * lax.rsqrt(ms + EPS) * nw_ref[...]).astype(BF16)

    res = jnp.dot(h_ref[...], w_ref[...], preferred_element_type=F32)

    @pl.when(j == 0)
    def _():
        for s in range(n_slabs):
            s_ref[s] = res[:, s * LANES:(s + 1) * LANES]
        for s in range(n_slabs):
            for t in range(S5_T):
                u_ref[s, :, t * LANES:(t + 1) * LANES] = (
                    s_ref[s, pl.ds(t, rows, stride=S5_T), :].astype(BF16))

    @pl.when(j > 0)
    def _():
        for s in range(n_slabs):
            p_ref[s] = res[:, s * LANES:(s + 1) * LANES].astype(BF16)

    @pl.when(j == pl.num_programs(1) - 1)
    def _():
        l_ref[...] = jnp.dot(h_ref[...], wl_ref[...], preferred_element_type=F32).astype(BF16)


def _in_proj(x2d, norm_w, w_main, w_low, *, tm, tn):
    m, d = x2d.shape
    n_steps = w_main.shape[1] // tn
    n_slabs = tn // LANES
    return pl.pallas_call(
        _in_proj_kernel,
        grid=(m // tm, n_steps),
        in_specs=[
            pl.BlockSpec((tm, d), lambda i, j: (i, 0)),
            pl.BlockSpec((1, d), lambda i, j: (0, 0)),
            pl.BlockSpec((d, tn), lambda i, j: (0, j)),
            pl.BlockSpec((d, LANES), lambda i, j: (0, 0)),
        ],
        out_specs=[
            pl.BlockSpec((n_slabs, tm // S5_T, S5_T * LANES), lambda i, j: (0, i, 0)),
            pl.BlockSpec((n_slabs, tm, LANES), lambda i, j: (jnp.maximum(j - 1, 0), i, 0)),
            pl.BlockSpec((tm, LANES), lambda i, j: (i, 0)),
        ],
        out_shape=[
            jax.ShapeDtypeStruct((n_slabs, m // S5_T, S5_T * LANES), BF16),
            jax.ShapeDtypeStruct(((n_steps - 1) * n_slabs, m, LANES), BF16),
            jax.ShapeDtypeStruct((m, LANES), BF16),
        ],
        scratch_shapes=[pltpu.VMEM((tm, d), BF16), pltpu.VMEM((n_slabs, tm, LANES), F32)],
        compiler_params=pltpu.CompilerParams(
            dimension_semantics=("parallel", "arbitrary"),
            vmem_limit_bytes=VMEM_LIMIT),
        name="in_proj",
    )(x2d, norm_w, w_main, w_low)


def _s5_kernel(x_ref, kb_ref, u_ref, v_ref, a_ref, d_ref, o_ref,
               m_ref, wz_ref, wy_ref, z_ref, sp_ref, y_ref, *, rows_per_seq, n_seq):
    nt = S5_T
    rows = x_ref.shape[0]
    half = a_ref.shape[-1] // 2
    pairs = half // LANES
    sub = LANES // pairs

    @pl.when(pl.program_id(0) == 0)
    def _():
        for n in range(nt // 2):
            m_ref[(2 * n + 1) * LANES:(2 * n + 2) * LANES, 2 * n * LANES:(2 * n + 1) * LANES] = (
                jnp.zeros((LANES, LANES), BF16))
        wz_ref[...] = jnp.zeros_like(wz_ref)

    for tau in range(nt):
        blk = kb_ref[tau]
        for s in range(nt - tau):
            t = s + tau
            m_ref[s * LANES:(s + 1) * LANES, t * LANES:(t + 1) * LANES] = blk

    lane = lax.broadcasted_iota(jnp.int32, (LANES, LANES), 1)
    for s in range(nt):
        for part in range(2):
            for a in range(pairs):
                c0 = part * half + a * LANES
                wz_ref[s * LANES + a * sub:s * LANES + (a + 1) * sub, c0:c0 + LANES] = (
                    u_ref[s, part, a * sub:(a + 1) * sub, :])
    for t in range(nt):
        for part in range(2):
            blk = v_ref[t, part]
            for a in range(pairs):
                r0 = part * half + a * LANES
                wy_ref[r0:r0 + LANES, t * LANES:(t + 1) * LANES] = (
                    jnp.where((lane // sub) == a, blk, jnp.zeros_like(blk)))

    z_ref[...] = jnp.dot(x_ref[...], wz_ref[...], preferred_element_type=F32)

    a_re = a_ref[:, :half]
    a_im = a_ref[:, half:]

    def body(c, carry):
        new = []
        for b in range(n_seq):
            s_re, s_im = carry[2 * b], carry[2 * b + 1]
            r = b * rows_per_seq + c
            sp_ref[pl.ds(r, 1), :half] = s_re
            sp_ref[pl.ds(r, 1), half:] = s_im
            z_re = z_ref[pl.ds(r, 1), :half]
            z_im = z_ref[pl.ds(r, 1), half:]
            new.append(a_re * s_re - a_im * s_im + z_re)
            new.append(a_re * s_im + a_im * s_re + z_im)
        return tuple(new)

    zero = jnp.zeros((1, half), F32)
    lax.fori_loop(0, rows_per_seq, body, (zero,) * (2 * n_seq), unroll=8)

    spb = sp_ref[...].astype(BF16)
    wide = 2 * LANES
    for n in range(nt // 2):
        cols = slice(n * wide, (n + 1) * wide)
        kk = (n + 1) * wide
        acc = jnp.dot(x_ref[:, :kk], m_ref[:kk, cols], preferred_element_type=F32)
        acc = acc + jnp.dot(spb, wy_ref[:, cols], preferred_element_type=F32)
        acc = acc + d_ref[:, cols] * x_ref[:, cols].astype(F32)
        y = jax.nn.gelu(acc)
        for k in range(2):
            y_ref[pl.ds(2 * n + k, rows, stride=nt), :] = y[:, k * LANES:(k + 1) * LANES]
    o_ref[...] = y_ref[...].astype(BF16)


def _s5(u2, kb, u_op, v_op, abar, dtile, *, n_seq):
    n_j, nt, _, _ = kb.shape
    rows = u2.shape[1]
    width = nt * LANES
    n_state = abar.shape[-1]
    return pl.pallas_call(
        functools.partial(_s5_kernel, rows_per_seq=rows // n_seq, n_seq=n_seq),
        grid=(n_j,),
        in_specs=[
            pl.BlockSpec((None, rows, width), lambda j: (j, 0, 0)),
            pl.BlockSpec((None, nt, LANES, LANES), lambda j: (j, 0, 0, 0)),
            pl.BlockSpec((None, nt, 2, LANES, LANES), lambda j: (j, 0, 0, 0, 0)),
            pl.BlockSpec((None, nt, 2, LANES, LANES), lambda j: (j, 0, 0, 0, 0)),
            pl.BlockSpec((None, 1, n_state), lambda j: (j, 0, 0)),
            pl.BlockSpec((None, 1, width), lambda j: (j, 0, 0)),
        ],
        out_specs=pl.BlockSpec((None, rows * nt, LANES), lambda j: (j, 0, 0)),
        out_shape=jax.ShapeDtypeStruct((n_j, rows * nt, LANES), BF16),
        scratch_shapes=[
            pltpu.VMEM((width, width), BF16),
            pltpu.VMEM((width, n_state), BF16),
            pltpu.VMEM((n_state, width), BF16),
            pltpu.VMEM((rows, n_state), F32),
            pltpu.VMEM((rows, n_state), F32),
            pltpu.VMEM((rows * nt, LANES), F32),
        ],
        compiler_params=pltpu.CompilerParams(
            dimension_semantics=("arbitrary",),
            vmem_limit_bytes=VMEM_LIMIT),
        name="s5",
    )(u2, kb, u_op, v_op, abar, dtile)


def _log_sigmoid(x):
    return jnp.minimum(x, 0.0) - jnp.log(1.0 + jnp.exp(-jnp.abs(x)))


def _gla_kernel(q_ref, k_ref, v_ref, gz_ref, gl_ref, gu_ref, gb_ref, nw_ref, o_ref, st_ref,
                *, chunk):
    n_heads, rows, dk = q_ref.shape
    n_chunks = rows // chunk
    d_gk = n_heads * dk

    @pl.when(pl.program_id(1) == 0)
    def _():
        st_ref[...] = jnp.zeros_like(st_ref)

    zg = jnp.dot(gl_ref[...], gu_ref[...], preferred_element_type=F32) + gb_ref[...]
    lg = _log_sigmoid(zg) * (1.0 / GLA_TAU)

    hi = lg.astype(BF16)
    lo = (lg - hi.astype(F32)).astype(BF16)
    r_id = lax.broadcasted_iota(jnp.int32, (chunk, chunk), 0)
    c_id = lax.broadcasted_iota(jnp.int32, (chunk, chunk), 1)
    causal = c_id <= r_id
    tri = jnp.where(causal, 1.0, 0.0).astype(BF16)
    tri2 = jnp.concatenate([tri, tri], axis=1)
    b_parts, last_parts = [], []
    for n in range(n_chunks):
        rs = slice(n * chunk, (n + 1) * chunk)
        b_n = jnp.dot(tri2, jnp.concatenate([hi[rs], lo[rs]], axis=0), preferred_element_type=F32)
        b_parts.append(b_n)
        last_parts.append(jnp.broadcast_to(b_n[chunk - 1:chunk], (chunk, d_gk)))
    b = jnp.concatenate(b_parts, axis=0)
    e_pos = jnp.exp(b)
    e_neg = jnp.exp(-b)
    decay = jnp.exp(jnp.concatenate(last_parts, axis=0))

    nt_dims = (((1,), (1,)), ((), ()))
    tn_dims = (((0,), (0,)), ((), ()))
    for h in range(n_heads):
        hs = slice(h * dk, (h + 1) * dk)
        q_e = (q_ref[h].astype(F32) * ((dk ** -0.5) * e_pos[:, hs])).astype(BF16)
        k_f = k_ref[h].astype(F32) * e_neg[:, hs]
        k_e = k_f.astype(BF16)
        k_t = (k_f * decay[:, hs]).astype(BF16)
        v = jnp.concatenate([v_ref[2 * h], v_ref[2 * h + 1]], axis=1)

        st = st_ref[h]
        outs = []
        for n in range(n_chunks):
            rs = slice(n * chunk, (n + 1) * chunk)
            attn = lax.dot_general(q_e[rs], k_e[rs], nt_dims, preferred_element_type=F32)
            attn = jnp.where(causal, attn, 0.0).astype(BF16)
            outs.append(jnp.dot(attn, v[rs], preferred_element_type=F32)
                        + lax.dot_general(q_e[rs], st.astype(BF16), nt_dims,
                                          preferred_element_type=F32))
            kv_t = lax.dot_general(v[rs], k_t[rs], tn_dims, preferred_element_type=F32)
            st = decay[n * chunk:n * chunk + 1, hs] * st + kv_t
        st_ref[h] = st

        o = jnp.concatenate(outs, axis=0)
        ms = jnp.mean(o * o, axis=-1, keepdims=True)
        o = o * lax.rsqrt(ms + EPS) * nw_ref[...]
        gz = jnp.concatenate([gz_ref[2 * h], gz_ref[2 * h + 1]], axis=1).astype(F32)
        y = (o * (gz * jax.nn.sigmoid(gz))).astype(BF16)
        o_ref[2 * h] = y[:, :LANES]
        o_ref[2 * h + 1] = y[:, LANES:]


def _gla(proj, g_low, gate_up_pad, gate_bias, norm_w, *, n_seq, seq_len, rows, slab_q, slab_k,
         slab_v, slab_gz):
    m = proj.shape[1]
    nb = seq_len // rows
    nh = GLA_HEADS
    dk = LANES
    dv = 2 * LANES
    tok = lambda b, n: b * nb + n
    return pl.pallas_call(
        functools.partial(_gla_kernel, chunk=GLA_CHUNK),
        grid=(n_seq, nb),
        in_specs=[
            pl.BlockSpec((nh, rows, LANES), lambda b, n: (slab_q // nh, tok(b, n), 0)),
            pl.BlockSpec((nh, rows, LANES), lambda b, n: (slab_k // nh, tok(b, n), 0)),
            pl.BlockSpec((2 * nh, rows, LANES), lambda b, n: (slab_v // (2 * nh), tok(b, n), 0)),
            pl.BlockSpec((2 * nh, rows, LANES), lambda b, n: (slab_gz // (2 * nh), tok(b, n), 0)),
            pl.BlockSpec((rows, LANES), lambda b, n: (tok(b, n), 0)),
            pl.BlockSpec((LANES, nh * dk), lambda b, n: (0, 0)),
            pl.BlockSpec((1, nh * dk), lambda b, n: (0, 0)),
            pl.BlockSpec((1, dv), lambda b, n: (0, 0)),
        ],
        out_specs=pl.BlockSpec((2 * nh, rows, LANES), lambda b, n: (0, tok(b, n), 0)),
        out_shape=jax.ShapeDtypeStruct((2 * nh, m, LANES), BF16),
        scratch_shapes=[pltpu.VMEM((nh, dv, dk), F32)],
        compiler_params=pltpu.CompilerParams(
            dimension_semantics=("parallel", "arbitrary"),
            vmem_limit_bytes=VMEM_LIMIT),
        name="gla",
    )(proj, proj, proj, proj, g_low, gate_up_pad, gate_bias, norm_w)


def _out_proj_kernel(g_ref, z_ref, yg_ref, x_ref, gw_ref, gb_ref, wo_ref, pw_ref, o_ref):
    n_s5 = g_ref.shape[0]
    d_s5 = n_s5 * LANES
    g = jnp.concatenate([g_ref[s] for s in range(n_s5)], axis=1)
    gate = jax.nn.sigmoid(jnp.dot(g, gw_ref[...], preferred_element_type=F32) + gb_ref[...])
    z = jnp.concatenate([z_ref[s] for s in range(n_s5)], axis=1).astype(F32)
    y_s5 = (g.astype(F32) * gate * (z * jax.nn.sigmoid(z))).astype(BF16)
    y_gla = jnp.concatenate([yg_ref[s] for s in range(yg_ref.shape[0])], axis=1)
    mixed = (jnp.dot(y_s5, wo_ref[:d_s5, :], preferred_element_type=F32)
             + jnp.dot(y_gla, wo_ref[d_s5:, :], preferred_element_type=F32))
    ms = jnp.mean(mixed * mixed, axis=-1, keepdims=True)
    o_ref[...] = x_ref[...] + mixed * lax.rsqrt(ms + EPS) * pw_ref[...]


def _out_proj(g_slabs, proj, y_gla, x2d, glu_w, glu_b, w_out, post_w, *, tm, slab_z):
    m, d = x2d.shape
    n_s5 = g_slabs.shape[0]
    n_gla = y_gla.shape[0]
    d_s5 = n_s5 * LANES
    d_mix = w_out.shape[0]
    return pl.pallas_call(
        _out_proj_kernel,
        grid=(m // tm,),
        in_specs=[
            pl.BlockSpec((n_s5, tm, LANES), lambda i: (0, i, 0)),
            pl.BlockSpec((n_s5, tm, LANES), lambda i: (slab_z // n_s5, i, 0)),
            pl.BlockSpec((n_gla, tm, LANES), lambda i: (0, i, 0)),
            pl.BlockSpec((tm, d), lambda i: (i, 0)),
            pl.BlockSpec((d_s5, d_s5), lambda i: (0, 0)),
            pl.BlockSpec((1, d_s5), lambda i: (0, 0)),
            pl.BlockSpec((d_mix, d), lambda i: (0, 0)),
            pl.BlockSpec((1, d), lambda i: (0, 0)),
        ],
        out_specs=pl.BlockSpec((tm, d), lambda i: (i, 0)),
        out_shape=jax.ShapeDtypeStruct((m, d), F32),
        compiler_params=pltpu.CompilerParams(
            dimension_semantics=("parallel",),
            vmem_limit_bytes=VMEM_LIMIT),
        name="out_proj",
    )(g_slabs, proj, y_gla, x2d, glu_w, glu_b, w_out, post_w)


def kernel(x, pre_norm_w, w_in, s5_A_re, s5_A_im, s5_B_re, s5_B_im, s5_C_re, s5_C_im, s5_D,
           s5_log_dt, s5_glu_w, s5_glu_b, gla_gate_up, gla_gate_bias, gla_norm_w, w_out,
           post_norm_w):
    bsz, seq_len, d_model = x.shape
    depth = w_in.shape[0]
    d_in = w_in.shape[2]
    d_s5 = s5_glu_w.shape[1]
    rank, d_gk = gla_gate_up.shape[1:]
    d_gv = GLA_HEADS * gla_norm_w.shape[1]
    m = bsz * seq_len
    d_main = 2 * d_s5 + 2 * d_gk + 2 * d_gv
    tn = d_s5
    assert d_s5 % LANES == 0 and d_gk == GLA_HEADS * LANES and d_gv == GLA_HEADS * 2 * LANES
    assert d_in == d_main + rank and rank <= LANES and d_main % tn == 0 and 2 * d_gk == tn
    assert seq_len % (S5_T * 8) == 0 and w_out.shape[1] == 2 * d_s5

    n_s5 = d_s5 // LANES
    slab_z = 0
    slab_q = n_s5
    slab_k = slab_q + d_gk // LANES
    slab_v = slab_k + d_gk // LANES
    slab_gz = slab_v + d_gv // LANES

    resid = x.astype(F32).reshape(m, d_model)
    for l in range(depth):
        kb, u_op, v_op, abar = _s5_ops(s5_A_re[l], s5_A_im[l], s5_B_re[l], s5_B_im[l],
                                       s5_C_re[l], s5_C_im[l], s5_log_dt[l])
        dtile = jnp.tile(s5_D[l].astype(F32).reshape(n_s5, 1, LANES), (1, 1, S5_T))

        w_main = w_in[l, :, :d_main].astype(BF16)
        w_low = jnp.pad(w_in[l, :, d_main:].astype(BF16), ((0, 0), (0, LANES - rank)))
        u2, proj, g_low = _in_proj(resid, pre_norm_w[l].astype(F32)[None], w_main, w_low,
                                   tm=1024, tn=tn)

        g_slabs = _s5(u2, kb, u_op, v_op, abar, dtile, n_seq=bsz)

        gate_up_pad = jnp.pad(gla_gate_up[l].astype(BF16), ((0, LANES - rank), (0, 0)))
        y_gla = _gla(proj, g_low, gate_up_pad, gla_gate_bias[l].astype(F32)[None],
                     gla_norm_w[l].astype(F32)[None], n_seq=bsz, seq_len=seq_len, rows=512,
                     slab_q=slab_q, slab_k=slab_k, slab_v=slab_v, slab_gz=slab_gz)

        resid = _out_proj(g_slabs, proj, y_gla, resid, s5_glu_w[l].astype(BF16),
                          s5_glu_b[l].astype(F32)[None], w_out[l].astype(BF16),
                          post_norm_w[l].astype(F32)[None], tm=512, slab_z=slab_z)
    return resid.reshape(bsz, seq_len, d_model).astype(x.dtype)
```

```python
import functools

import jax
import jax.numpy as jnp
from jax import lax
from jax.experimental import pallas as pl
from jax.experimental.pallas import tpu as pltpu

F32 = jnp.float32
BF16 = jnp.bfloat16

S5_GROUP = 16
GLA_HEADS = 4
GLA_TAU = 16.0
GLA_CHUNK = 64
EPS = 1e-6

LANES = 128
S5_T = 16
IN_PROJ_ROW_CHUNK = 256
VMEM_LIMIT = 56 * 1024 * 1024


def _cmul(ar, ai, br, bi):
    return ar * br - ai * bi, ar * bi + ai * br


def _s5_ops_kernel(tp_ref, np_ref, kb_ref, u_ref, v_ref, ab_ref, *, n_state):
    nt = S5_T
    half = LANES // 2

    def a_bar(are_raw, aim, ldt):
        are = jnp.minimum(are_raw, -1e-4)
        dt = jnp.exp(ldt)
        mag = jnp.exp(are * dt)
        return are, mag * jnp.cos(aim * dt), mag * jnp.sin(aim * dt)

    are, abr, abi = a_bar(tp_ref[0], tp_ref[1], tp_ref[2])
    aim = tp_ref[1]
    den = are * are + aim * aim
    nr = abr - 1.0
    fr = (nr * are + abi * aim) / den
    fi = (abi * are - nr * aim) / den
    xr, xi = _cmul(fr, fi, tp_ref[3], tp_ref[4])

    row = lax.broadcasted_iota(jnp.int32, (LANES, LANES), 0)
    lane = lax.broadcasted_iota(jnp.int32, (LANES, LANES), 1)
    own_half = ((row // S5_GROUP) % 2) == (lane // half)
    same_group = (row // S5_GROUP) == (lane // S5_GROUP)
    first_half = lane < half

    c_stack = jnp.concatenate([np_ref[3], np_ref[4]], axis=0)
    for tau in range(nt):
        lhs = jnp.where(first_half, xr, -xi)
        ker = jnp.dot(lhs, c_stack, preferred_element_type=F32, precision=lax.Precision.HIGHEST)
        kb_ref[tau] = jnp.where(same_group, ker, 0.0).astype(BF16)
        u_ref[nt - 1 - tau, 0] = jnp.where(own_half, xr, 0.0).astype(BF16)
        u_ref[nt - 1 - tau, 1] = jnp.where(own_half, xi, 0.0).astype(BF16)
        if tau + 1 < nt:
            xr, xi = _cmul(xr, xi, abr, abi)

    pr, pi = abr, abi
    for _ in range(4):
        pr, pi = _cmul(pr, pi, pr, pi)
    assert nt == 16
    pairs = LANES // (2 * S5_GROUP)
    for part, val in enumerate((pr, pi)):
        for a in range(pairs):
            r0 = 2 * a * S5_GROUP
            piece = jnp.where(first_half[:1], val[r0:r0 + 1], val[r0 + S5_GROUP:r0 + S5_GROUP + 1])
            c0 = part * (n_state // 2) + a * LANES
            ab_ref[:, c0:c0 + LANES] = piece

    _, nbr, nbi = a_bar(np_ref[0], np_ref[1], np_ref[2])
    cr, ci = np_ref[3], np_ref[4]
    lane_n = lax.broadcasted_iota(jnp.int32, (half, LANES), 1)
    parity = (lane_n // S5_GROUP) % 2
    pr, pi = nbr, nbi
    for t in range(nt):
        wr, wi = _cmul(cr, ci, pr, pi)
        for q in range(2):
            rows = slice(q * half, (q + 1) * half)
            v_ref[t, 0, rows, :] = jnp.where(parity == q, wr, 0.0).astype(BF16)
            v_ref[t, 1, rows, :] = jnp.where(parity == q, -wi, 0.0).astype(BF16)
        if t + 1 < nt:
            pr, pi = _cmul(pr, pi, nbr, nbi)


def _s5_ops(a_re, a_im, b_re, b_im, c_re, c_im, log_dt):
    g, p = a_re.shape
    h = S5_GROUP
    gps = LANES // h
    n_j = g // gps
    assert 2 * p == LANES and g % gps == 0
    n_state = 2 * gps * p
    f = lambda t: t.astype(F32)

    def t_arr(t):
        t = f(t).reshape(n_j, gps, -1, 1, p)
        return jnp.broadcast_to(t, (n_j, gps, h, 2, p)).reshape(n_j, LANES, LANES)

    def n_arr(t):
        t = jnp.swapaxes(f(t).reshape(n_j, gps, -1, p), 1, 3)
        t = jnp.swapaxes(jnp.broadcast_to(t, (n_j, p, h, gps)), 2, 3)
        return t.reshape(n_j, p, LANES)

    ldt = jnp.broadcast_to(f(log_dt)[:, None], (g, p))
    tpar = jnp.stack([t_arr(a_re), t_arr(a_im), t_arr(ldt),
                      t_arr(jnp.swapaxes(b_re, 1, 2)), t_arr(jnp.swapaxes(b_im, 1, 2))], axis=1)
    npar = jnp.stack([n_arr(a_re), n_arr(a_im), n_arr(ldt), n_arr(c_re), n_arr(c_im)], axis=1)
    nt = S5_T
    return pl.pallas_call(
        functools.partial(_s5_ops_kernel, n_state=n_state),
        grid=(n_j,),
        in_specs=[
            pl.BlockSpec((None, 5, LANES, LANES), lambda j: (j, 0, 0, 0)),
            pl.BlockSpec((None, 5, p, LANES), lambda j: (j, 0, 0, 0)),
        ],
        out_specs=[
            pl.BlockSpec((None, nt, LANES, LANES), lambda j: (j, 0, 0, 0)),
            pl.BlockSpec((None, nt, 2, LANES, LANES), lambda j: (j, 0, 0, 0, 0)),
            pl.BlockSpec((None, nt, 2, LANES, LANES), lambda j: (j, 0, 0, 0, 0)),
            pl.BlockSpec((None, 1, n_state), lambda j: (j, 0, 0)),
        ],
        out_shape=[
            jax.ShapeDtypeStruct((n_j, nt, LANES, LANES), BF16),
            jax.ShapeDtypeStruct((n_j, nt, 2, LANES, LANES), BF16),
            jax.ShapeDtypeStruct((n_j, nt, 2, LANES, LANES), BF16),
            jax.ShapeDtypeStruct((n_j, 1, n_state), F32),
        ],
        compiler_params=pltpu.CompilerParams(dimension_semantics=("parallel",)),
        name="s5_ops",
    )(tpar, npar)


def _in_proj_kernel(x_ref, nw_ref, w_ref, wl_ref, u_ref, p_ref, l_ref, h_ref, s_ref):
    j = pl.program_id(1)
    n_slabs = p_ref.shape[0]
    tm = x_ref.shape[0]
    rc = IN_PROJ_ROW_CHUNK
    fold = rc // S5_T

    @pl.when(j == 0)
    def _():
        for c in range(tm // rc):
            rows = slice(c * rc, (c + 1) * rc)
            xf = x_ref[rows, :]
            ms = jnp.mean(xf * xf, axis=-1, keepdims=True)
            hc = (xf * lax.rsqrt(ms + EPS) * nw_ref[...]).astype(BF16)
            h_ref[rows, :] = hc
            res = jnp.dot(hc, w_ref[...], preferred_element_type=F32)
            for s in range(n_slabs):
                s_ref[s, rows, :] = res[:, s * LANES:(s + 1) * LANES]
            for s in range(n_slabs):
                for t in range(S5_T):
                    u_ref[s, c * fold:(c + 1) * fold, t * LANES:(t + 1) * LANES] = (
                        s_ref[s, pl.ds(c * rc + t, fold, stride=S5_T), :].astype(BF16))

    @pl.when(j > 0)
    def _():
        res = jnp.dot(h_ref[...], w_ref[...], preferred_element_type=F32)
        for s in range(n_slabs):
            p_ref[s] = res[:, s * LANES:(s + 1) * LANES].astype(BF16)

    @pl.when(j == pl.num_programs(1) - 1)
    def _():
        l_ref[...] = jnp.dot(h_ref[...], wl_ref[...], preferred_element_type=F32).astype(BF16)


def _cast_cols_kernel(w_ref, o_ref, *, n_valid):
    tn = o_ref.shape[1]
    col = pl.program_id(0) * tn + lax.broadcasted_iota(jnp.int32, (1, tn), 1)
    o_ref[...] = jnp.where(col < n_valid, w_ref[...], 0.0).astype(BF16)


def _cast_cols(w3d, layer, *, tn):
    _, d, n = w3d.shape
    n_blocks = pl.cdiv(n, tn)
    return pl.pallas_call(
        functools.partial(_cast_cols_kernel, n_valid=n),
        grid=(n_blocks,),
        in_specs=[pl.BlockSpec((None, d, tn), lambda c: (layer, 0, c))],
        out_specs=pl.BlockSpec((d, tn), lambda c: (0, c)),
        out_shape=jax.ShapeDtypeStruct((d, n_blocks * tn), BF16),
        compiler_params=pltpu.CompilerParams(dimension_semantics=("parallel",)),
        name="cast_cols",
    )(w3d)


def _in_proj(x2d, norm_w, w_bf, *, d_main, tm, tn):
    m, d = x2d.shape
    n_steps = d_main // tn
    n_slabs = tn // LANES
    return pl.pallas_call(
        _in_proj_kernel,
        grid=(m // tm, n_steps),
        in_specs=[
            pl.BlockSpec((tm, d), lambda i, j: (i, 0)),
            pl.BlockSpec((1, d), lambda i, j: (0, 0)),
            pl.BlockSpec((d, tn), lambda i, j: (0, j)),
            pl.BlockSpec((d, LANES), lambda i, j: (0, d_main // LANES)),
        ],
        out_specs=[
            pl.BlockSpec((n_slabs, tm // S5_T, S5_T * LANES), lambda i, j: (0, i, 0)),
            pl.BlockSpec((n_slabs, tm, LANES), lambda i, j: (jnp.maximum(j - 1, 0), i, 0)),
            pl.BlockSpec((tm, LANES), lambda i, j: (i, 0)),
        ],
        out_shape=[
            jax.ShapeDtypeStruct((n_slabs, m // S5_T, S5_T * LANES), BF16),
            jax.ShapeDtypeStruct(((n_steps - 1) * n_slabs, m, LANES), BF16),
            jax.ShapeDtypeStruct((m, LANES), BF16),
        ],
        scratch_shapes=[pltpu.VMEM((tm, d), BF16), pltpu.VMEM((n_slabs, tm, LANES), F32)],
        compiler_params=pltpu.CompilerParams(
            dimension_semantics=("parallel", "arbitrary"),
            vmem_limit_bytes=VMEM_LIMIT),
        name="in_proj",
    )(x2d, norm_w, w_bf, w_bf)


def _s5_kernel(x_ref, kb_ref, u_ref, v_ref, a_ref, d_ref, o_ref,
               m_ref, wz_ref, wy_ref, z_ref, sp_ref, y_ref, *, rows_per_seq, n_seq):
    nt = S5_T
    rows = x_ref.shape[0]
    half = a_ref.shape[-1] // 2
    pairs = half // LANES
    sub = LANES // pairs

    @pl.when(pl.program_id(0) == 0)
    def _():
        for n in range(nt // 2):
            m_ref[(2 * n + 1) * LANES:(2 * n + 2) * LANES, 2 * n * LANES:(2 * n + 1) * LANES] = (
                jnp.zeros((LANES, LANES), BF16))
        wz_ref[...] = jnp.zeros_like(wz_ref)

    for tau in range(nt):
        blk = kb_ref[tau]
        for s in range(nt - tau):
            t = s + tau
            m_ref[s * LANES:(s + 1) * LANES, t * LANES:(t + 1) * LANES] = blk

    lane = lax.broadcasted_iota(jnp.int32, (LANES, LANES), 1)
    for s in range(nt):
        for part in range(2):
            for a in range(pairs):
                c0 = part * half + a * LANES
                wz_ref[s * LANES + a * sub:s * LANES + (a + 1) * sub, c0:c0 + LANES] = (
                    u_ref[s, part, a * sub:(a + 1) * sub, :])
    for t in range(nt):
        for part in range(2):
            blk = v_ref[t, part]
            for a in range(pairs):
                r0 = part * half + a * LANES
                wy_ref[r0:r0 + LANES, t * LANES:(t + 1) * LANES] = (
                    jnp.where((lane // sub) == a, blk, jnp.zeros_like(blk)))

    z_ref[...] = jnp.dot(x_ref[...], wz_ref[...], preferred_element_type=F32)

    a_re = a_ref[:, :half]
    a_im = a_ref[:, half:]

    def body(c, carry):
        new = []
        for b in range(n_seq):
            s_re, s_im = carry[2 * b], carry[2 * b + 1]
            r = b * rows_per_seq + c
            sp_ref[pl.ds(r, 1), :half] = s_re
            sp_ref[pl.ds(r, 1), half:] = s_im
            z_re = z_ref[pl.ds(r, 1), :half]
            z_im = z_ref[pl.ds(r, 1), half:]
            new.append(a_re * s_re - a_im * s_im + z_re)
            new.append(a_re * s_im + a_im * s_re + z_im)
        return tuple(new)

    zero = jnp.zeros((1, half), F32)
    lax.fori_loop(0, rows_per_seq, body, (zero,) * (2 * n_seq), unroll=8)

    spb = sp_ref[...].astype(BF16)
    wide = 2 * LANES
    for n in range(nt // 2):
        cols = slice(n * wide, (n + 1) * wide)
        kk = (n + 1) * wide
        acc = jnp.dot(x_ref[:, :kk], m_ref[:kk, cols], preferred_element_type=F32)
        acc = acc + jnp.dot(spb, wy_ref[:, cols], preferred_element_type=F32)
        acc = acc + d_ref[:, cols] * x_ref[:, cols].astype(F32)
        y = jax.nn.gelu(acc)
        for k in range(2):
            y_ref[pl.ds(2 * n + k, rows, stride=nt), :] = y[:, k * LANES:(k + 1) * LANES]
    o_ref[...] = y_ref[...].astype(BF16)


def _s5(u2, kb, u_op, v_op, abar, dtile, *, n_seq):
    n_j, nt, _, _ = kb.shape
    rows = u2.shape[1]
    width = nt * LANES
    n_state = abar.shape[-1]
    return pl.pallas_call(
        functools.partial(_s5_kernel, rows_per_seq=rows // n_seq, n_seq=n_seq),
        grid=(n_j,),
        in_specs=[
            pl.BlockSpec((None, rows, width), lambda j: (j, 0, 0)),
            pl.BlockSpec((None, nt, LANES, LANES), lambda j: (j, 0, 0, 0)),
            pl.BlockSpec((None, nt, 2, LANES, LANES), lambda j: (j, 0, 0, 0, 0)),
            pl.BlockSpec((None, nt, 2, LANES, LANES), lambda j: (j, 0, 0, 0, 0)),
            pl.BlockSpec((None, 1, n_state), lambda j: (j, 0, 0)),
            pl.BlockSpec((None, 1, width), lambda j: (j, 0, 0)),
        ],
        out_specs=pl.BlockSpec((None, rows * nt, LANES), lambda j: (j, 0, 0)),
        out_shape=jax.ShapeDtypeStruct((n_j, rows * nt, LANES), BF16),
        scratch_shapes=[
            pltpu.VMEM((width, width), BF16),
            pltpu.VMEM((width, n_state), BF16),
            pltpu.VMEM((n_state, width), BF16),
            pltpu.VMEM((rows, n_state), F32),
            pltpu.VMEM((rows, n_state), F32),
            pltpu.VMEM((rows * nt, LANES), F32),
        ],
        compiler_params=pltpu.CompilerParams(
            dimension_semantics=("arbitrary",),
            vmem_limit_bytes=VMEM_LIMIT),
        name="s5",
    )(u2, kb, u_op, v_op, abar, dtile)


def _log_sigmoid(x):
    return jnp.minimum(x, 0.0) - jnp.log(1.0 + jnp.exp(-jnp.abs(x)))


def _gla_kernel(q_ref, k_ref, v_ref, gz_ref, gl_ref, gu_ref, gb_ref, nw_ref, o_ref, st_ref,
                *, chunk):
    n_heads, rows, dk = q_ref.shape
    n_chunks = rows // chunk
    d_gk = n_heads * dk

    @pl.when(pl.program_id(1) == 0)
    def _():
        st_ref[...] = jnp.zeros_like(st_ref)

    zg = jnp.dot(gl_ref[...], gu_ref[...], preferred_element_type=F32) + gb_ref[...]
    lg = _log_sigmoid(zg) * (1.0 / GLA_TAU)

    hi = lg.astype(BF16)
    lo = (lg - hi.astype(F32)).astype(BF16)
    r_id = lax.broadcasted_iota(jnp.int32, (chunk, chunk), 0)
    c_id = lax.broadcasted_iota(jnp.int32, (chunk, chunk), 1)
    causal = c_id <= r_id
    tri = jnp.where(causal, 1.0, 0.0).astype(BF16)
    tri2 = jnp.concatenate([tri, tri], axis=1)
    b_parts, last_parts = [], []
    for n in range(n_chunks):
        rs = slice(n * chunk, (n + 1) * chunk)
        b_n = jnp.dot(tri2, jnp.concatenate([hi[rs], lo[rs]], axis=0), preferred_element_type=F32)
        b_parts.append(b_n)
        last_parts.append(jnp.broadcast_to(b_n[chunk - 1:chunk], (chunk, d_gk)))
    b = jnp.concatenate(b_parts, axis=0)
    e_pos = jnp.exp(b)
    e_neg = jnp.exp(-b)
    decay = jnp.exp(jnp.concatenate(last_parts, axis=0))

    nt_dims = (((1,), (1,)), ((), ()))
    tn_dims = (((0,), (0,)), ((), ()))
    for h in range(n_heads):
        hs = slice(h * dk, (h + 1) * dk)
        q_e = (q_ref[h].astype(F32) * ((dk ** -0.5) * e_pos[:, hs])).astype(BF16)
        k_f = k_ref[h].astype(F32) * e_neg[:, hs]
        k_e = k_f.astype(BF16)
        k_t = (k_f * decay[:, hs]).astype(BF16)
        v = jnp.concatenate([v_ref[2 * h], v_ref[2 * h + 1]], axis=1)

        st = st_ref[h]
        outs = []
        for n in range(n_chunks):
            rs = slice(n * chunk, (n + 1) * chunk)
            attn = lax.dot_general(q_e[rs], k_e[rs], nt_dims, preferred_element_type=F32)
            attn = jnp.where(causal, attn, 0.0).astype(BF16)
            outs.append(jnp.dot(attn, v[rs], preferred_element_type=F32)
                        + lax.dot_general(q_e[rs], st.astype(BF16), nt_dims,
                                          preferred_element_type=F32))
            kv_t = lax.dot_general(v[rs], k_t[rs], tn_dims, preferred_element_type=F32)
            st = decay[n * chunk:n * chunk + 1, hs] * st + kv_t
        st_ref[h] = st

        o = jnp.concatenate(outs, axis=0)
        ms = jnp.mean(o * o, axis=-1, keepdims=True)
        o = o * lax.rsqrt(ms + EPS) * nw_ref[...]
        gz = jnp.concatenate([gz_ref[2 * h], gz_ref[2 * h + 1]], axis=1).astype(F32)
        y = (o * (gz * jax.nn.sigmoid(gz))).astype(BF16)
        o_ref[2 * h] = y[:, :LANES]
        o_ref[2 * h + 1] = y[:, LANES:]


def _gla(proj, g_low, gate_up_pad, gate_bias, norm_w, *, n_seq, seq_len, rows, slab_q, slab_k,
         slab_v, slab_gz):
    m = proj.shape[1]
    nb = seq_len // rows
    nh = GLA_HEADS
    dk = LANES
    dv = 2 * LANES
    tok = lambda b, n: b * nb + n
    return pl.pallas_call(
        functools.partial(_gla_kernel, chunk=GLA_CHUNK),
        grid=(n_seq, nb),
        in_specs=[
            pl.BlockSpec((nh, rows, LANES), lambda b, n: (slab_q // nh, tok(b, n), 0)),
            pl.BlockSpec((nh, rows, LANES), lambda b, n: (slab_k // nh, tok(b, n), 0)),
            pl.BlockSpec((2 * nh, rows, LANES), lambda b, n: (slab_v // (2 * nh), tok(b, n), 0)),
            pl.BlockSpec((2 * nh, rows, LANES), lambda b, n: (slab_gz // (2 * nh), tok(b, n), 0)),
            pl.BlockSpec((rows, LANES), lambda b, n: (tok(b, n), 0)),
            pl.BlockSpec((LANES, nh * dk), lambda b, n: (0, 0)),
            pl.BlockSpec((1, nh * dk), lambda b, n: (0, 0)),
            pl.BlockSpec((1, dv), lambda b, n: (0, 0)),
        ],
        out_specs=pl.BlockSpec((2 * nh, rows, LANES), lambda b, n: (0, tok(b, n), 0)),
        out_shape=jax.ShapeDtypeStruct((2 * nh, m, LANES), BF16),
        scratch_shapes=[pltpu.VMEM((nh, dv, dk), F32)],
        compiler_params=pltpu.CompilerParams(
            dimension_semantics=("parallel", "arbitrary"),
            vmem_limit_bytes=VMEM_LIMIT),
        name="gla",
    )(proj, proj, proj, proj, g_low, gate_up_pad, gate_bias, norm_w)


def _out_proj_kernel(g_ref, z_ref, yg_ref, x_ref, gw_ref, gb_ref, wo_ref, pw_ref, o_ref):
    n_s5 = g_ref.shape[0]
    d_s5 = n_s5 * LANES
    g = jnp.concatenate([g_ref[s] for s in range(n_s5)], axis=1)
    gate = jax.nn.sigmoid(jnp.dot(g, gw_ref[...], preferred_element_type=F32) + gb_ref[...])
    z = jnp.concatenate([z_ref[s] for s in range(n_s5)], axis=1).astype(F32)
    y_s5 = (g.astype(F32) * gate * (z * jax.nn.sigmoid(z))).astype(BF16)
    y_gla = jnp.concatenate([yg_ref[s] for s in range(yg_ref.shape[0])], axis=1)
    mixed = (jnp.dot(y_s5, wo_ref[:d_s5, :], preferred_element_type=F32)
             + jnp.dot(y_gla, wo_ref[d_s5:, :], preferred_element_type=F32))
    ms = jnp.mean(mixed * mixed, axis=-1, keepdims=True)
    o_ref[...] = x_ref[...] + mixed * lax.rsqrt(ms + EPS) * pw_ref[...]


def _out_proj(g_slabs, proj, y_gla, x2d, glu_w, glu_b, w_out, post_w, *, tm, slab_z):
    m, d = x2d.shape
    n_s5 = g_slabs.shape[0]
    n_gla = y_gla.shape[0]
    d_s5 = n_s5 * LANES
    d_mix = w_out.shape[0]
    return pl.pallas_call(
        _out_proj_kernel,
        grid=(m // tm,),
        in_specs=[
            pl.BlockSpec((n_s5, tm, LANES), lambda i: (0, i, 0)),
            pl.BlockSpec((n_s5, tm, LANES), lambda i: (slab_z // n_s5, i, 0)),
            pl.BlockSpec((n_gla, tm, LANES), lambda i: (0, i, 0)),
            pl.BlockSpec((tm, d), lambda i: (i, 0)),
            pl.BlockSpec((d_s5, d_s5), lambda i: (0, 0)),
            pl.BlockSpec((1, d_s5), lambda i: (0, 0)),
            pl.BlockSpec((d_mix, d), lambda i: (0, 0)),
            pl.BlockSpec((1, d), lambda i: (0, 0)),
        ],
        out_specs=pl.BlockSpec((tm, d), lambda i: (i, 0)),
        out_shape=jax.ShapeDtypeStruct((m, d), F32),
        compiler_params=pltpu.CompilerParams(
            dimension_semantics=("parallel",),
            vmem_limit_bytes=VMEM_LIMIT),
        name="out_proj",
    )(g_slabs, proj, y_gla, x2d, glu_w, glu_b, w_out, post_w)


def kernel(x, pre_norm_w, w_in, s5_A_re, s5_A_im, s5_B_re, s5_B_im, s5_C_re, s5_C_im, s5_D,
           s5_log_dt, s5_glu_w, s5_glu_b, gla_gate_up, gla_gate_bias, gla_norm_w, w_out,
           post_norm_w):
    bsz, seq_len, d_model = x.shape
    depth = w_in.shape[0]
    d_in = w_in.shape[2]
    d_s5 = s5_glu_w.shape[1]
    rank, d_gk = gla_gate_up.shape[1:]
    d_gv = GLA_HEADS * gla_norm_w.shape[1]
    m = bsz * seq_len
    d_main = 2 * d_s5 + 2 * d_gk + 2 * d_gv
    tn = d_s5
    assert d_s5 % LANES == 0 and d_gk == GLA_HEADS * LANES and d_gv == GLA_HEADS * 2 * LANES
    assert d_in == d_main + rank and rank <= LANES and d_main % tn == 0 and 2 * d_gk == tn
    assert seq_len % (S5_T * 8) == 0 and w_out.shape[1] == 2 * d_s5

    n_s5 = d_s5 // LANES
    slab_z = 0
    slab_q = n_s5
    slab_k = slab_q + d_gk // LANES
    slab_v = slab_k + d_gk // LANES
    slab_gz = slab_v + d_gv // LANES

    resid = x.astype(F32).reshape(m, d_model)
    for l in range(depth):
        kb, u_op, v_op, abar = _s5_ops(s5_A_re[l], s5_A_im[l], s5_B_re[l], s5_B_im[l],
                                       s5_C_re[l], s5_C_im[l], s5_log_dt[l])
        dtile = jnp.tile(s5_D[l].astype(F32).reshape(n_s5, 1, LANES), (1, 1, S5_T))

        w_bf = _cast_cols(w_in, l, tn=4 * LANES)
        u2, proj, g_low = _in_proj(resid, pre_norm_w[l].astype(F32)[None], w_bf,
                                   d_main=d_main, tm=1024, tn=tn)

        g_slabs = _s5(u2, kb, u_op, v_op, abar, dtile, n_seq=bsz)

        gate_up_pad = jnp.pad(gla_gate_up[l].astype(BF16), ((0, LANES - rank), (0, 0)))
        y_gla = _gla(proj, g_low, gate_up_pad, gla_gate_bias[l].astype(F32)[None],
                     gla_norm_w[l].astype(F32)[None], n_seq=bsz, seq_len=seq_len, rows=512,
                     slab_q=slab_q, slab_k=slab_k, slab_v=slab_v, slab_gz=slab_gz)

        resid = _out_proj(g_slabs, proj, y_gla, resid, s5_glu_w[l].astype(BF16),
                          s5_glu_b[l].astype(F32)[None], w_out[l].astype(BF16),
                          post_norm_w[l].astype(F32)[None], tm=512, slab_z=slab_z)
    return resid.reshape(bsz, seq_len, d_model).astype(x.dtype)
```

```python
import functools

import jax
import jax.numpy as jnp
from jax import lax
from jax.experimental import pallas as pl
from jax.experimental.pallas import tpu as pltpu

F32 = jnp.float32
BF16 = jnp.bfloat16

S5_GROUP = 16
GLA_HEADS = 4
GLA_TAU = 16.0
GLA_CHUNK = 64
EPS = 1e-6

LANES = 128
S5_T = 16
IN_PROJ_ROW_CHUNK = 256
VMEM_LIMIT = 56 * 1024 * 1024


NT_DIMS = (((1,), (1,)), ((), ()))
TN_DIMS = (((0,), (0,)), ((), ()))


def _cmul(ar, ai, br, bi):
    return ar * br - ai * bi, ar * bi + ai * br


def _s5_ops_kernel(tp_ref, np_ref, kb_ref, u_ref, v_ref, ab_ref, *, n_state):
    nt = S5_T
    half = LANES // 2

    def a_bar(are_raw, aim, ldt):
        are = jnp.minimum(are_raw, -1e-4)
        dt = jnp.exp(ldt)
        mag = jnp.exp(are * dt)
        return are, mag * jnp.cos(aim * dt), mag * jnp.sin(aim * dt)

    are, abr, abi = a_bar(tp_ref[0], tp_ref[1], tp_ref[2])
    aim = tp_ref[1]
    den = are * are + aim * aim
    nr = abr - 1.0
    fr = (nr * are + abi * aim) / den
    fi = (abi * are - nr * aim) / den
    xr, xi = _cmul(fr, fi, tp_ref[3], tp_ref[4])

    row = lax.broadcasted_iota(jnp.int32, (LANES, LANES), 0)
    lane = lax.broadcasted_iota(jnp.int32, (LANES, LANES), 1)
    own_half = ((row // S5_GROUP) % 2) == (lane // half)
    same_group = (row // S5_GROUP) == (lane // S5_GROUP)
    first_half = lane < half

    c_stack = jnp.concatenate([np_ref[3], np_ref[4]], axis=0)
    for tau in range(nt):
        lhs = jnp.where(first_half, xr, -xi)
        ker = jnp.dot(lhs, c_stack, preferred_element_type=F32, precision=lax.Precision.HIGHEST)
        kb_ref[tau] = jnp.where(same_group, ker, 0.0).astype(BF16)
        u_ref[nt - 1 - tau, 0] = jnp.where(own_half, xr, 0.0).astype(BF16)
        u_ref[nt - 1 - tau, 1] = jnp.where(own_half, xi, 0.0).astype(BF16)
        if tau + 1 < nt:
            xr, xi = _cmul(xr, xi, abr, abi)

    pr, pi = abr, abi
    for _ in range(4):
        pr, pi = _cmul(pr, pi, pr, pi)
    assert nt == 16
    pairs = LANES // (2 * S5_GROUP)
    for part, val in enumerate((pr, pi)):
        for a in range(pairs):
            r0 = 2 * a * S5_GROUP
            piece = jnp.where(first_half[:1], val[r0:r0 + 1], val[r0 + S5_GROUP:r0 + S5_GROUP + 1])
            c0 = part * (n_state // 2) + a * LANES
            ab_ref[:, c0:c0 + LANES] = piece

    _, nbr, nbi = a_bar(np_ref[0], np_ref[1], np_ref[2])
    cr, ci = np_ref[3], np_ref[4]
    lane_n = lax.broadcasted_iota(jnp.int32, (half, LANES), 1)
    parity = (lane_n // S5_GROUP) % 2
    pr, pi = nbr, nbi
    for t in range(nt):
        wr, wi = _cmul(cr, ci, pr, pi)
        for q in range(2):
            rows = slice(q * half, (q + 1) * half)
            v_ref[t, 0, rows, :] = jnp.where(parity == q, wr, 0.0).astype(BF16)
            v_ref[t, 1, rows, :] = jnp.where(parity == q, -wi, 0.0).astype(BF16)
        if t + 1 < nt:
            pr, pi = _cmul(pr, pi, nbr, nbi)


def _s5_ops(a_re, a_im, b_re, b_im, c_re, c_im, log_dt):
    g, p = a_re.shape
    h = S5_GROUP
    gps = LANES // h
    n_j = g // gps
    assert 2 * p == LANES and g % gps == 0
    n_state = 2 * gps * p
    f = lambda t: t.astype(F32)

    def t_arr(t):
        t = f(t).reshape(n_j, gps, -1, 1, p)
        return jnp.broadcast_to(t, (n_j, gps, h, 2, p)).reshape(n_j, LANES, LANES)

    def n_arr(t):
        t = jnp.swapaxes(f(t).reshape(n_j, gps, -1, p), 1, 3)
        t = jnp.swapaxes(jnp.broadcast_to(t, (n_j, p, h, gps)), 2, 3)
        return t.reshape(n_j, p, LANES)

    ldt = jnp.broadcast_to(f(log_dt)[:, None], (g, p))
    tpar = jnp.stack([t_arr(a_re), t_arr(a_im), t_arr(ldt),
                      t_arr(jnp.swapaxes(b_re, 1, 2)), t_arr(jnp.swapaxes(b_im, 1, 2))], axis=1)
    npar = jnp.stack([n_arr(a_re), n_arr(a_im), n_arr(ldt), n_arr(c_re), n_arr(c_im)], axis=1)
    nt = S5_T
    return pl.pallas_call(
        functools.partial(_s5_ops_kernel, n_state=n_state),
        grid=(n_j,),
        in_specs=[
            pl.BlockSpec((None, 5, LANES, LANES), lambda j: (j, 0, 0, 0)),
            pl.BlockSpec((None, 5, p, LANES), lambda j: (j, 0, 0, 0)),
        ],
        out_specs=[
            pl.BlockSpec((None, nt, LANES, LANES), lambda j: (j, 0, 0, 0)),
            pl.BlockSpec((None, nt, 2, LANES, LANES), lambda j: (j, 0, 0, 0, 0)),
            pl.BlockSpec((None, nt, 2, LANES, LANES), lambda j: (j, 0, 0, 0, 0)),
            pl.BlockSpec((None, 1, n_state), lambda j: (j, 0, 0)),
        ],
        out_shape=[
            jax.ShapeDtypeStruct((n_j, nt, LANES, LANES), BF16),
            jax.ShapeDtypeStruct((n_j, nt, 2, LANES, LANES), BF16),
            jax.ShapeDtypeStruct((n_j, nt, 2, LANES, LANES), BF16),
            jax.ShapeDtypeStruct((n_j, 1, n_state), F32),
        ],
        compiler_params=pltpu.CompilerParams(dimension_semantics=("parallel",)),
        name="s5_ops",
    )(tpar, npar)


def _in_proj_kernel(x_ref, nw_ref, w_ref, wl_ref, u_ref, p_ref, l_ref, h_ref, s_ref):
    j = pl.program_id(1)
    n_slabs = p_ref.shape[0]
    tm = x_ref.shape[0]
    rc = IN_PROJ_ROW_CHUNK
    fold = rc // S5_T

    @pl.when(j == 0)
    def _():
        for c in range(tm // rc):
            rows = slice(c * rc, (c + 1) * rc)
            xf = x_ref[rows, :]
            ms = jnp.mean(xf * xf, axis=-1, keepdims=True)
            hc = (xf * lax.rsqrt(ms + EPS) * nw_ref[...]).astype(BF16)
            h_ref[rows, :] = hc
            res = lax.dot_general(hc, w_ref[...], NT_DIMS, preferred_element_type=F32)
            for s in range(n_slabs):
                s_ref[s, rows, :] = res[:, s * LANES:(s + 1) * LANES]
            for s in range(n_slabs):
                for t in range(S5_T):
                    u_ref[s, c * fold:(c + 1) * fold, t * LANES:(t + 1) * LANES] = (
                        s_ref[s, pl.ds(c * rc + t, fold, stride=S5_T), :].astype(BF16))

    @pl.when(j > 0)
    def _():
        res = lax.dot_general(h_ref[...], w_ref[...], NT_DIMS, preferred_element_type=F32)
        for s in range(n_slabs):
            p_ref[s]= res[:, s * LANES:(s + 1) * LANES].astype(BF16)

    @pl.when(j == pl.num_programs(1) - 1)
    def _():
        l_ref[...] = lax.dot_general(h_ref[...], wl_ref[...], NT_DIMS,
                                     preferred_element_type=F32).astype(BF16)


def _cast_rows_kernel(w_ref, o_ref, *, n_valid):
    tn = o_ref.shape[0]
    row = pl.program_id(0) * tn + lax.broadcasted_iota(jnp.int32, (tn, 1), 0)
    o_ref[...] = jnp.where(row < n_valid, w_ref[...], 0.0).astype(BF16)


def _cast_rows(w3d, layer, *, tn):
    _, n, d = w3d.shape
    n_blocks = pl.cdiv(n, tn)
    return pl.pallas_call(
        functools.partial(_cast_rows_kernel, n_valid=n),
        grid=(n_blocks,),
        in_specs=[pl.BlockSpec((None, tn, d), lambda c: (layer, c, 0))],
        out_specs=pl.BlockSpec((tn, d), lambda c: (c, 0)),
        out_shape=jax.ShapeDtypeStruct((n_blocks * tn, d), BF16),
        compiler_params=pltpu.CompilerParams(dimension_semantics=("parallel",)),
        name="cast_rows",
    )(w3d)


def _in_proj(x2d, norm_w, w_bf, *, d_main, tm, tn):
    m, d = x2d.shape
    n_steps = d_main // tn
    n_slabs = tn // LANES
    return pl.pallas_call(
        _in_proj_kernel,
        grid=(m // tm, n_steps),
        in_specs=[
            pl.BlockSpec((tm, d), lambda i, j: (i, 0)),
            pl.BlockSpec((1, d), lambda i, j: (0, 0)),
            pl.BlockSpec((tn, d), lambda i, j: (j, 0)),
            pl.BlockSpec((LANES, d), lambda i, j: (d_main // LANES, 0)),
        ],
        out_specs=[
            pl.BlockSpec((n_slabs, tm // S5_T, S5_T * LANES), lambda i, j: (0, i, 0)),
            pl.BlockSpec((n_slabs, tm, LANES), lambda i, j: (jnp.maximum(j - 1, 0), i, 0)),
            pl.BlockSpec((tm, LANES), lambda i, j: (i, 0)),
        ],
        out_shape=[
            jax.ShapeDtypeStruct((n_slabs, m // S5_T, S5_T * LANES), BF16),
            jax.ShapeDtypeStruct(((n_steps - 1) * n_slabs, m, LANES), BF16),
            jax.ShapeDtypeStruct((m, LANES), BF16),
        ],
        scratch_shapes=[pltpu.VMEM((tm, d), BF16), pltpu.VMEM((n_slabs, tm, LANES), F32)],
        compiler_params=pltpu.CompilerParams(
            dimension_semantics=("parallel", "arbitrary"),
            vmem_limit_bytes=VMEM_LIMIT),
        name="in_proj",
    )(x2d, norm_w, w_bf, w_bf)


def _s5_kernel(x_ref, kb_ref, u_ref, v_ref, a_ref, d_ref, o_ref,
               m_ref, wz_ref, wy_ref, z_ref, sp_ref, y_ref, *, rows_per_seq, n_seq):
    nt = S5_T
    rows = x_ref.shape[0]
    half = a_ref.shape[-1] // 2
    pairs = half // LANES
    sub = LANES // pairs

    @pl.when(pl.program_id(0) == 0)
    def _():
        for n in range(nt // 2):
            m_ref[(2 * n + 1) * LANES:(2 * n + 2) * LANES, 2 * n * LANES:(2 * n + 1) * LANES] = (
                jnp.zeros((LANES, LANES), BF16))
        wz_ref[...] = jnp.zeros_like(wz_ref)

    for tau in range(nt):
        blk = kb_ref[tau]
        for s in range(nt - tau):
            t = s + tau
            m_ref[s * LANES:(s + 1) * LANES, t * LANES:(t + 1) * LANES] = blk

    lane = lax.broadcasted_iota(jnp.int32, (LANES, LANES), 1)
    for s in range(nt):
        for part in range(2):
            for a in range(pairs):
                c0 = part * half + a * LANES
                wz_ref[s * LANES + a * sub:s * LANES + (a + 1) * sub, c0:c0 + LANES] = (
                    u_ref[s, part, a * sub:(a + 1) * sub, :])
    for t in range(nt):
        for part in range(2):
            blk = v_ref[t, part]
            for a in range(pairs):
                r0 = part * half + a * LANES
                wy_ref[r0:r0 + LANES, t * LANES:(t + 1) * LANES] = (
                    jnp.where((lane // sub) == a, blk, jnp.zeros_like(blk)))

    z_ref[...] = jnp.dot(x_ref[...], wz_ref[...], preferred_element_type=F32)

    a_re = a_ref[:, :half]
    a_im = a_ref[:, half:]

    def body(c, carry):
        new = []
        for b in range(n_seq):
            s_re, s_im = carry[2 * b], carry[2 * b + 1]
            r = b * rows_per_seq + c
            sp_ref[pl.ds(r, 1), :half] = s_re
            sp_ref[pl.ds(r, 1), half:] = s_im
            z_re = z_ref[pl.ds(r, 1), :half]
            z_im = z_ref[pl.ds(r, 1), half:]
            new.append(a_re * s_re - a_im * s_im + z_re)
            new.append(a_re * s_im + a_im * s_re + z_im)
        return tuple(new)

    zero = jnp.zeros((1, half), F32)
    lax.fori_loop(0, rows_per_seq, body, (zero,) * (2 * n_seq), unroll=8)

    spb = sp_ref[...].astype(BF16)
    wide = 2 * LANES
    for n in range(nt // 2):
        cols = slice(n * wide, (n + 1) * wide)
        kk = (n + 1) * wide
        acc = jnp.dot(x_ref[:, :kk], m_ref[:kk, cols], preferred_element_type=F32)
        acc = acc + jnp.dot(spb, wy_ref[:, cols], preferred_element_type=F32)
        acc = acc + d_ref[:, cols] * x_ref[:, cols].astype(F32)
        y = jax.nn.gelu(acc)
        for k in range(2):
            y_ref[pl.ds(2 * n + k, rows, stride=nt), :] = y[:, k * LANES:(k + 1) * LANES]
    o_ref[...] = y_ref[...].astype(BF16)


def _s5(u2, kb, u_op, v_op, abar, dtile, *, n_seq):
    n_j, nt, _, _ = kb.shape
    rows = u2.shape[1]
    width = nt * LANES
    n_state = abar.shape[-1]
    return pl.pallas_call(
        functools.partial(_s5_kernel, rows_per_seq=rows // n_seq, n_seq=n_seq),
        grid=(n_j,),
        in_specs=[
            pl.BlockSpec((None, rows, width), lambda j: (j, 0, 0)),
            pl.BlockSpec((None, nt, LANES, LANES), lambda j: (j, 0, 0, 0)),
            pl.BlockSpec((None, nt, 2, LANES, LANES), lambda j: (j, 0, 0, 0, 0)),
            pl.BlockSpec((None, nt, 2, LANES, LANES), lambda j: (j, 0, 0, 0, 0)),
            pl.BlockSpec((None, 1, n_state), lambda j: (j, 0, 0)),
            pl.BlockSpec((None, 1, width), lambda j: (j, 0, 0)),
        ],
        out_specs=pl.BlockSpec((None, rows * nt, LANES), lambda j: (j, 0, 0)),
        out_shape=jax.ShapeDtypeStruct((n_j, rows * nt, LANES), BF16),
        scratch_shapes=[
            pltpu.VMEM((width, width), BF16),
            pltpu.VMEM((width, n_state), BF16),
            pltpu.VMEM((n_state, width), BF16),
            pltpu.VMEM((rows, n_state), F32),
            pltpu.VMEM((rows, n_state), F32),
            pltpu.VMEM((rows * nt, LANES), F32),
        ],
        compiler_params=pltpu.CompilerParams(
            dimension_semantics=("arbitrary",),
            vmem_limit_bytes=VMEM_LIMIT),
        name="s5",
    )(u2, kb, u_op, v_op, abar, dtile)


def _log_sigmoid(x):
    return jnp.minimum(x, 0.0) - jnp.log(1.0 + jnp.exp(-jnp.abs(x)))


def _gla_kernel(q_ref, k_ref, v_ref, gz_ref, gl_ref, gu_ref, gb_ref, nw_ref, o_ref, st_ref,
                *, chunk):
    n_heads, rows, dk = q_ref.shape
    n_chunks = rows // chunk
    d_gk = n_heads * dk

    @pl.when(pl.program_id(1) == 0)
    def _():
        st_ref[...] = jnp.zeros_like(st_ref)

    zg = jnp.dot(gl_ref[...], gu_ref[...], preferred_element_type=F32) + gb_ref[...]
    lg = _log_sigmoid(zg) * (1.0 / GLA_TAU)

    hi = lg.astype(BF16)
    lo = (lg - hi.astype(F32)).astype(BF16)
    r_id = lax.broadcasted_iota(jnp.int32, (chunk, chunk), 0)
    c_id = lax.broadcasted_iota(jnp.int32, (chunk, chunk), 1)
    causal = c_id <= r_id
    tri = jnp.where(causal, 1.0, 0.0).astype(BF16)
    tri2 = jnp.concatenate([tri, tri], axis=1)
    b_parts, last_parts = [], []
    for n in range(n_chunks):
        rs = slice(n * chunk, (n + 1) * chunk)
        b_n = jnp.dot(tri2, jnp.concatenate([hi[rs], lo[rs]], axis=0), preferred_element_type=F32)
        b_parts.append(b_n)
        last_parts.append(jnp.broadcast_to(b_n[chunk - 1:chunk], (chunk, d_gk)))
    b = jnp.concatenate(b_parts, axis=0)
    e_pos = jnp.exp(b)
    e_neg = jnp.exp(-b)
    decay = jnp.exp(jnp.concatenate(last_parts, axis=0))

    nt_dims = (((1,), (1,)), ((), ()))
    tn_dims = (((0,), (0,)), ((), ()))
    for h in range(n_heads):
        hs = slice(h * dk, (h + 1) * dk)
        q_e = (q_ref[h].astype(F32) * ((dk ** -0.5) * e_pos[:, hs])).astype(BF16)
        k_f = k_ref[h].astype(F32) * e_neg[:, hs]
        k_e = k_f.astype(BF16)
        k_t = (k_f * decay[:, hs]).astype(BF16)
        v = jnp.concatenate([v_ref[2 * h], v_ref[2 * h + 1]], axis=1)

        st = st_ref[h]
        outs = []
        for n in range(n_chunks):
            rs = slice(n * chunk, (n + 1) * chunk)
            attn = lax.dot_general(q_e[rs], k_e[rs], nt_dims, preferred_element_type=F32)
            attn = jnp.where(causal, attn, 0.0).astype(BF16)
            outs.append(jnp.dot(attn, v[rs], preferred_element_type=F32)
                        + lax.dot_general(q_e[rs], st.astype(BF16), nt_dims,
                                          preferred_element_type=F32))
            kv_t = lax.dot_general(v[rs], k_t[rs], tn_dims, preferred_element_type=F32)
            st = decay[n * chunk:n * chunk + 1, hs] * st + kv_t
        st_ref[h] = st

        o = jnp.concatenate(outs, axis=0)
        ms = jnp.mean(o * o, axis=-1, keepdims=True)
        o = o * lax.rsqrt(ms + EPS) * nw_ref[...]
        gz = jnp.concatenate([gz_ref[2 * h], gz_ref[2 * h + 1]], axis=1).astype(F32)
        y = (o * (gz * jax.nn.sigmoid(gz))).astype(BF16)
        o_ref[2 * h] = y[:, :LANES]
        o_ref[2 * h + 1] = y[:, LANES:]


def _gla(proj, g_low, gate_up_pad, gate_bias, norm_w, *, n_seq, seq_len, rows, slab_q, slab_k,
         slab_v, slab_gz):
    m = proj.shape[1]
    nb = seq_len // rows
    nh = GLA_HEADS
    dk = LANES
    dv = 2 * LANES
    tok = lambda b, n: b * nb + n
    return pl.pallas_call(
        functools.partial(_gla_kernel, chunk=GLA_CHUNK),
        grid=(n_seq, nb),
        in_specs=[
            pl.BlockSpec((nh, rows, LANES), lambda b, n: (slab_q // nh, tok(b, n), 0)),
            pl.BlockSpec((nh, rows, LANES), lambda b, n: (slab_k // nh, tok(b, n), 0)),
            pl.BlockSpec((2 * nh, rows, LANES), lambda b, n: (slab_v // (2 * nh), tok(b, n), 0)),
            pl.BlockSpec((2 * nh, rows, LANES), lambda b, n: (slab_gz // (2 * nh), tok(b, n), 0)),
            pl.BlockSpec((rows, LANES), lambda b, n: (tok(b, n), 0)),
            pl.BlockSpec((LANES, nh * dk), lambda b, n: (0, 0)),
            pl.BlockSpec((1, nh * dk), lambda b, n: (0, 0)),
            pl.BlockSpec((1, dv), lambda b, n: (0, 0)),
        ],
        out_specs=pl.BlockSpec((2 * nh, rows, LANES), lambda b, n: (0, tok(b, n), 0)),
        out_shape=jax.ShapeDtypeStruct((2 * nh, m, LANES), BF16),
        scratch_shapes=[pltpu.VMEM((nh, dv, dk), F32)],
        compiler_params=pltpu.CompilerParams(
            dimension_semantics=("parallel", "arbitrary"),
            vmem_limit_bytes=VMEM_LIMIT),
        name="gla",
    )(proj, proj, proj, proj, g_low, gate_up_pad, gate_bias, norm_w)


def _out_proj_kernel(g_ref, z_ref, yg_ref, x_ref, gw_ref, gb_ref, wo_ref, pw_ref, o_ref):
    n_s5 = g_ref.shape[0]
    d_s5 = n_s5 * LANES
    g = jnp.concatenate([g_ref[s] for s in range(n_s5)], axis=1)
    gate = jax.nn.sigmoid(jnp.dot(g, gw_ref[...], preferred_element_type=F32) + gb_ref[...])
    z = jnp.concatenate([z_ref[s] for s in range(n_s5)], axis=1).astype(F32)
    y_s5 = (g.astype(F32) * gate * (z * jax.nn.sigmoid(z))).astype(BF16)
    y_gla = jnp.concatenate([yg_ref[s] for s in range(yg_ref.shape[0])], axis=1)
    mixed = (jnp.dot(y_s5, wo_ref[:d_s5, :], preferred_element_type=F32)
             + jnp.dot(y_gla, wo_ref[d_s5:, :], preferred_element_type=F32))
    ms = jnp.mean(mixed * mixed, axis=-1, keepdims=True)
    o_ref[...] = x_ref[...] + mixed * lax.rsqrt(ms + EPS) * pw_ref[...]


def _out_proj(g_slabs, proj, y_gla, x2d, glu_w, glu_b, w_out, post_w, *, tm, slab_z):
    m, d = x2d.shape
    n_s5 = g_slabs.shape[0]
    n_gla = y_gla.shape[0]
    d_s5 = n_s5 * LANES
    d_mix = w_out.shape[0]
    return pl.pallas_call(
        _out_proj_kernel,
        grid=(m // tm,),
        in_specs=[
            pl.BlockSpec((n_s5, tm, LANES), lambda i: (0, i, 0)),
            pl.BlockSpec((n_s5, tm, LANES), lambda i: (slab_z // n_s5, i, 0)),
            pl.BlockSpec((n_gla, tm, LANES), lambda i: (0, i, 0)),
            pl.BlockSpec((tm, d), lambda i: (i, 0)),
            pl.BlockSpec((d_s5, d_s5), lambda i: (0, 0)),
            pl.BlockSpec((1, d_s5), lambda i: (0, 0)),
            pl.BlockSpec((d_mix, d), lambda i: (0, 0)),
            pl.BlockSpec((1, d), lambda i: (0, 0)),
        ],
        out_specs=pl.BlockSpec((tm, d), lambda i: (i, 0)),
        out_shape=jax.ShapeDtypeStruct((m, d), F32),
        compiler_params=pltpu.CompilerParams(
            dimension_semantics=("parallel",),
            vmem_limit_bytes=VMEM_LIMIT),
        name="out_proj",
    )(g_slabs, proj, y_gla, x2d, glu_w, glu_b, w_out, post_w)


def kernel(x, pre_norm_w, w_in, s5_A_re, s5_A_im, s5_B_re, s5_B_im, s5_C_re, s5_C_im, s5_D,
           s5_log_dt, s5_glu_w, s5_glu_b, gla_gate_up, gla_gate_bias, gla_norm_w, w_out,
           post_norm_w):
    bsz, seq_len, d_model = x.shape
    depth = w_in.shape[0]
    d_in = w_in.shape[2]
    d_s5 = s5_glu_w.shape[1]
    rank, d_gk = gla_gate_up.shape[1:]
    d_gv = GLA_HEADS * gla_norm_w.shape[1]
    m = bsz * seq_len
    d_main = 2 * d_s5 + 2 * d_gk + 2 * d_gv
    tn = d_s5
    assert d_s5 % LANES == 0 and d_gk == GLA_HEADS * LANES and d_gv == GLA_HEADS * 2 * LANES
    assert d_in == d_main + rank and rank <= LANES and d_main % tn == 0 and 2 * d_gk == tn
    assert seq_len % (S5_T * 8) == 0 and w_out.shape[1] == 2 * d_s5

    n_s5 = d_s5 // LANES
    slab_z = 0
    slab_q = n_s5
    slab_k = slab_q + d_gk // LANES
    slab_v = slab_k + d_gk // LANES
    slab_gz = slab_v + d_gv // LANES

    resid = x.astype(F32).reshape(m, d_model)
    for l in range(depth):
        kb, u_op, v_op, abar = _s5_ops(s5_A_re[l], s5_A_im[l], s5_B_re[l], s5_B_im[l],
                                       s5_C_re[l], s5_C_im[l], s5_log_dt[l])
        dtile = jnp.tile(s5_D[l].astype(F32).reshape(n_s5, 1, LANES), (1, 1, S5_T))

        w_bf = _cast_rows(jnp.swapaxes(w_in, 1, 2), l, tn=4 * LANES)
        u2, proj, g_low = _in_proj(resid, pre_norm_w[l].astype(F32)[None], w_bf,
                                   d_main=d_main, tm=1024, tn=tn)

        g_slabs = _s5(u2, kb, u_op, v_op, abar, dtile, n_seq=bsz)

        gate_up_pad = jnp.pad(gla_gate_up[l].astype(BF16), ((0, LANES - rank), (0, 0)))
        y_gla = _gla(proj, g_low, gate_up_pad, gla_gate_bias[l].astype(F32)[None],
                     gla_norm_w[l].astype(F32)[None], n_seq=bsz, seq_len=seq_len, rows=512,
                     slab_q=slab_q, slab_k=slab_k, slab_v=slab_v, slab_gz=slab_gz)

        resid = _out_proj(g_slabs, proj, y_gla, resid, s5_glu_w[l].astype(BF16),
                          s5_glu_b[l].astype(F32)[None], w_out[l].astype(BF16),
                          post_norm_w[l].astype(F32)[None], tm=512, slab_z=slab_z)
    return resid.reshape(bsz, seq_len, d_model).astype(x.dtype)
```

```python
import functools

import jax
import jax.numpy as jnp
from jax import lax
from jax.experimental import pallas as pl
from jax.experimental.pallas import tpu as pltpu

F32 = jnp.float32
BF16 = jnp.bfloat16

S5_GROUP = 16
GLA_HEADS = 4
GLA_TAU = 16.0
GLA_CHUNK = 64
EPS = 1e-6

LANES = 128
S5_T = 16
S5_UNFOLD_PITCH = 24
IN_PROJ_ROW_CHUNK = 256
OUT_PROJ_ROW_CHUNK = 256
VMEM_LIMIT = 56 * 1024 * 1024


NT_DIMS = (((1,), (1,)), ((), ()))
TN_DIMS = (((0,), (0,)), ((), ()))


def _cmul(ar, ai, br, bi):
    return ar * br - ai * bi, ar * bi + ai * br


def _s5_ops_kernel(tp_ref, np_ref, kb_ref, u_ref, v_ref, ab_ref, *, n_state):
    nt = S5_T
    half = LANES // 2

    def a_bar(are_raw, aim, ldt):
        are = jnp.minimum(are_raw, -1e-4)
        dt = jnp.exp(ldt)
        mag = jnp.exp(are * dt)
        return are, mag * jnp.cos(aim * dt), mag * jnp.sin(aim * dt)

    are, abr, abi = a_bar(tp_ref[0], tp_ref[1], tp_ref[2])
    aim = tp_ref[1]
    den = are * are + aim * aim
    nr = abr - 1.0
    fr = (nr * are + abi * aim) / den
    fi = (abi * are - nr * aim) / den
    xr, xi = _cmul(fr, fi, tp_ref[3], tp_ref[4])

    row = lax.broadcasted_iota(jnp.int32, (LANES, LANES), 0)
    lane = lax.broadcasted_iota(jnp.int32, (LANES, LANES), 1)
    own_half = ((row // S5_GROUP) % 2) == (lane // half)
    same_group = (row // S5_GROUP) == (lane // S5_GROUP)
    first_half = lane < half

    c_stack = jnp.concatenate([np_ref[3], np_ref[4]], axis=0)
    for tau in range(nt):
        lhs = jnp.where(first_half, xr, -xi)
        ker = jnp.dot(lhs, c_stack, preferred_element_type=F32, precision=lax.Precision.HIGHEST)
        kb_ref[tau] = jnp.where(same_group, ker, 0.0).astype(BF16)
        u_ref[nt - 1 - tau, 0] = jnp.where(own_half, xr, 0.0).astype(BF16)
        u_ref[nt - 1 - tau, 1] = jnp.where(own_half, xi, 0.0).astype(BF16)
        if tau + 1 < nt:
            xr, xi = _cmul(xr, xi, abr, abi)

    pr, pi = abr, abi
    for _ in range(4):
        pr, pi = _cmul(pr, pi, pr, pi)
    assert nt == 16
    pairs = LANES // (2 * S5_GROUP)
    for part, val in enumerate((pr, pi)):
        for a in range(pairs):
            r0 = 2 * a * S5_GROUP
            piece = jnp.where(first_half[:1], val[r0:r0 + 1], val[r0 + S5_GROUP:r0 + S5_GROUP + 1])
            c0 = part * (n_state // 2) + a * LANES
            ab_ref[:, c0:c0 + LANES] = piece

    _, nbr, nbi = a_bar(np_ref[0], np_ref[1], np_ref[2])
    cr, ci = np_ref[3], np_ref[4]
    lane_n = lax.broadcasted_iota(jnp.int32, (half, LANES), 1)
    parity = (lane_n // S5_GROUP) % 2
    pr, pi = nbr, nbi
    for t in range(nt):
        wr, wi = _cmul(cr, ci, pr, pi)
        for q in range(2):
            rows = slice(q * half, (q + 1) * half)
            v_ref[t, 0, rows, :] = jnp.where(parity == q, wr, 0.0).astype(BF16)
            v_ref[t, 1, rows, :] = jnp.where(parity == q, -wi, 0.0).astype(BF16)
        if t + 1 < nt:
            pr, pi = _cmul(pr, pi, nbr, nbi)


def _s5_ops(a_re, a_im, b_re, b_im, c_re, c_im, log_dt):
    g, p = a_re.shape
    h = S5_GROUP
    gps = LANES // h
    n_j = g // gps
    assert 2 * p == LANES and g % gps == 0
    n_state = 2 * gps * p
    f = lambda t: t.astype(F32)

    def t_arr(t):
        t = f(t).reshape(n_j, gps, -1, 1, p)
        return jnp.broadcast_to(t, (n_j, gps, h, 2, p)).reshape(n_j, LANES, LANES)

    def n_arr(t):
        t = jnp.swapaxes(f(t).reshape(n_j, gps, -1, p), 1, 3)
        t = jnp.swapaxes(jnp.broadcast_to(t, (n_j, p, h, gps)), 2, 3)
        return t.reshape(n_j, p, LANES)

    ldt = jnp.broadcast_to(f(log_dt)[:, None], (g, p))
    tpar = jnp.stack([t_arr(a_re), t_arr(a_im), t_arr(ldt),
                      t_arr(jnp.swapaxes(b_re, 1, 2)), t_arr(jnp.swapaxes(b_im, 1, 2))], axis=1)
    npar = jnp.stack([n_arr(a_re), n_arr(a_im), n_arr(ldt), n_arr(c_re), n_arr(c_im)], axis=1)
    nt = S5_T
    return pl.pallas_call(
        functools.partial(_s5_ops_kernel, n_state=n_state),
        grid=(n_j,),
        in_specs=[
            pl.BlockSpec((None, 5, LANES, LANES), lambda j: (j, 0, 0, 0)),
            pl.BlockSpec((None, 5, p, LANES), lambda j: (j, 0, 0, 0)),
        ],
        out_specs=[
            pl.BlockSpec((None, nt, LANES, LANES), lambda j: (j, 0, 0, 0)),
            pl.BlockSpec((None, nt, 2, LANES, LANES), lambda j: (j, 0, 0, 0, 0)),
            pl.BlockSpec((None, nt, 2, LANES, LANES), lambda j: (j, 0, 0, 0, 0)),
            pl.BlockSpec((None, 1, n_state), lambda j: (j, 0, 0)),
        ],
        out_shape=[
            jax.ShapeDtypeStruct((n_j, nt, LANES, LANES), BF16),
            jax.ShapeDtypeStruct((n_j, nt, 2, LANES, LANES), BF16),
            jax.ShapeDtypeStruct((n_j, nt, 2, LANES, LANES), BF16),
            jax.ShapeDtypeStruct((n_j, 1, n_state), F32),
        ],
        compiler_params=pltpu.CompilerParams(dimension_semantics=("parallel",)),
        name="s5_ops",
    )(tpar, npar)


def _in_proj_kernel(x_ref, nw_ref, w_ref, wl_ref, u_ref, p_ref, l_ref, h_ref, s_ref):
    j = pl.program_id(1)
    n_slabs = p_ref.shape[0]
    tm = x_ref.shape[0]
    rc = IN_PROJ_ROW_CHUNK
    fold = rc // S5_T

    @pl.when(j == 0)
    def _():
        for c in range(tm // rc):
            rows = slice(c * rc, (c + 1) * rc)
            xf = x_ref[rows, :]
            ms = jnp.mean(xf * xf, axis=-1, keepdims=True)
            hc = (xf * lax.rsqrt(ms + EPS) * nw_ref[...]).astype(BF16)
            h_ref[rows, :] = hc
            res = lax.dot_general(hc, w_ref[...], NT_DIMS, preferred_element_type=F32)
            for s in range(n_slabs):
                s_ref[s, rows, :] = res[:, s * LANES:(s + 1) * LANES]
            for s in range(n_slabs):
                for t in range(S5_T):
                    u_ref[s, c * fold:(c + 1) * fold, t * LANES:(t + 1) * LANES] = (
                        s_ref[s, pl.ds(c * rc + t, fold, stride=S5_T), :].astype(BF16))

    @pl.when(j > 0)
    def _():
        res = lax.dot_general(h_ref[...], w_ref[...], NT_DIMS, preferred_element_type=F32)
        for s in range(n_slabs):
            p_ref[s]= res[:, s * LANES:(s + 1) * LANES].astype(BF16)

    @pl.when(j == pl.num_programs(1) - 1)
    def _():
        l_ref[...] = lax.dot_general(h_ref[...], wl_ref[...], NT_DIMS,
                                     preferred_element_type=F32).astype(BF16)


def _cast_rows_kernel(w_ref, o_ref, *, n_valid):
    tn = o_ref.shape[0]
    row = pl.program_id(0) * tn + lax.broadcasted_iota(jnp.int32, (tn, 1), 0)
    o_ref[...] = jnp.where(row < n_valid, w_ref[...], 0.0).astype(BF16)


def _cast_rows(w3d, layer, *, tn):
    _, n, d = w3d.shape
    n_blocks = pl.cdiv(n, tn)
    return pl.pallas_call(
        functools.partial(_cast_rows_kernel, n_valid=n),
        grid=(n_blocks,),
        in_specs=[pl.BlockSpec((None, tn, d), lambda c: (layer, c, 0))],
        out_specs=pl.BlockSpec((tn, d), lambda c: (c, 0)),
        out_shape=jax.ShapeDtypeStruct((n_blocks * tn, d), BF16),
        compiler_params=pltpu.CompilerParams(dimension_semantics=("parallel",)),
        name="cast_rows",
    )(w3d)


def _in_proj(x2d, norm_w, w_bf, *, d_main, tm, tn):
    m, d = x2d.shape
    n_steps = d_main // tn
    n_slabs = tn // LANES
    return pl.pallas_call(
        _in_proj_kernel,
        grid=(m // tm, n_steps),
        in_specs=[
            pl.BlockSpec((tm, d), lambda i, j: (i, 0)),
            pl.BlockSpec((1, d), lambda i, j: (0, 0)),
            pl.BlockSpec((tn, d), lambda i, j: (j, 0)),
            pl.BlockSpec((LANES, d), lambda i, j: (d_main // LANES, 0)),
        ],
        out_specs=[
            pl.BlockSpec((n_slabs, tm // S5_T, S5_T * LANES), lambda i, j: (0, i, 0)),
            pl.BlockSpec((n_slabs, tm, LANES), lambda i, j: (jnp.maximum(j - 1, 0), i, 0)),
            pl.BlockSpec((tm, LANES), lambda i, j: (i, 0)),
        ],
        out_shape=[
            jax.ShapeDtypeStruct((n_slabs, m // S5_T, S5_T * LANES), BF16),
            jax.ShapeDtypeStruct(((n_steps - 1) * n_slabs, m, LANES), BF16),
            jax.ShapeDtypeStruct((m, LANES), BF16),
        ],
        scratch_shapes=[pltpu.VMEM((tm, d), BF16), pltpu.VMEM((n_slabs, tm, LANES), F32)],
        compiler_params=pltpu.CompilerParams(
            dimension_semantics=("parallel", "arbitrary"),
            vmem_limit_bytes=VMEM_LIMIT),
        name="in_proj",
    )(x2d, norm_w, w_bf, w_bf)


def _s5_kernel(x_ref, kb_ref, u_ref, v_ref, a_ref, d_ref, o_ref,
               m_ref, wz_ref, wy_ref, z_ref, sp_ref, y_ref, *, rows_per_seq, n_seq):
    nt = S5_T
    rows = x_ref.shape[0]
    half = a_ref.shape[-1] // 2
    pairs = half // LANES
    sub = LANES // pairs

    wide = 2 * LANES

    @pl.when(pl.program_id(0) == 0)
    def _():
        m_ref[(nt - 1) * LANES:, :LANES] = jnp.zeros((LANES, LANES), BF16)
        wz_ref[...] = jnp.zeros_like(wz_ref)
        y_ref[...] = jnp.zeros_like(y_ref)

    for s in range(nt):
        for k in range(2):
            lag = nt - 2 + k - s
            if lag >= 0:
                m_ref[s * LANES:(s + 1) * LANES, k * LANES:(k + 1) * LANES] = kb_ref[lag]

    lane = lax.broadcasted_iota(jnp.int32, (LANES, LANES), 1)
    for s in range(nt):
        for part in range(2):
            for a in range(pairs):
                c0 = part * half + a * LANES
                wz_ref[s * LANES + a * sub:s * LANES + (a + 1) * sub, c0:c0 + LANES] = (
                    u_ref[s, part, a * sub:(a + 1) * sub, :])
    for t in range(nt):
        for part in range(2):
            blk = v_ref[t, part]
            for a in range(pairs):
                r0 = part * half + a * LANES
                wy_ref[r0:r0 + LANES, t * LANES:(t + 1) * LANES] = (
                    jnp.where((lane // sub) == a, blk, jnp.zeros_like(blk)))

    z_ref[...] = jnp.dot(x_ref[...], wz_ref[...], preferred_element_type=F32)

    a_re = a_ref[:, :half]
    a_im = a_ref[:, half:]

    def body(c, carry):
        new = []
        for b in range(n_seq):
            s_re, s_im = carry[2 * b], carry[2 * b + 1]
            r = b * rows_per_seq + c
            sp_ref[pl.ds(r, 1), :half] = s_re
            sp_ref[pl.ds(r, 1), half:] = s_im
            z_re = z_ref[pl.ds(r, 1), :half]
            z_im = z_ref[pl.ds(r, 1), half:]
            new.append(a_re * s_re - a_im * s_im + z_re)
            new.append(a_re * s_im + a_im * s_re + z_im)
        return tuple(new)

    zero = jnp.zeros((1, half), F32)
    lax.fori_loop(0, rows_per_seq, body, (zero,) * (2 * n_seq), unroll=8)

    spb = sp_ref[...].astype(BF16)
    pitch = y_ref.shape[0] // rows
    for n in range(nt // 2):
        cols = slice(n * wide, (n + 1) * wide)
        kk = (n + 1) * wide
        acc = jnp.dot(x_ref[:, :kk], m_ref[nt * LANES - kk:, :], preferred_element_type=F32)
        acc = acc + jnp.dot(spb, wy_ref[:, cols], preferred_element_type=F32)
        acc = acc + d_ref[:, cols] * x_ref[:, cols].astype(F32)
        y = jax.nn.gelu(acc)
        for k in range(2):
            y_ref[pl.ds(2 * n + k, rows, stride=pitch), :] = y[:, k * LANES:(k + 1) * LANES]
    tokens = y_ref[...].reshape(rows, pitch, LANES)[:, :nt, :].reshape(rows * nt, LANES)
    o_ref[...] = tokens.astype(BF16)


def _s5(u2, kb, u_op, v_op, abar, dtile, *, n_seq):
    n_j, nt, _, _ = kb.shape
    rows = u2.shape[1]
    width = nt * LANES
    n_state = abar.shape[-1]
    return pl.pallas_call(
        functools.partial(_s5_kernel, rows_per_seq=rows // n_seq, n_seq=n_seq),
        grid=(n_j,),
        in_specs=[
            pl.BlockSpec((None, rows, width), lambda j: (j, 0, 0)),
            pl.BlockSpec((None, nt, LANES, LANES), lambda j: (j, 0, 0, 0)),
            pl.BlockSpec((None, nt, 2, LANES, LANES), lambda j: (j, 0, 0, 0, 0)),
            pl.BlockSpec((None, nt, 2, LANES, LANES), lambda j: (j, 0, 0, 0, 0)),
            pl.BlockSpec((None, 1, n_state), lambda j: (j, 0, 0)),
            pl.BlockSpec((None, 1, width), lambda j: (j, 0, 0)),
        ],
        out_specs=pl.BlockSpec((None, rows * nt, LANES), lambda j: (j, 0, 0)),
        out_shape=jax.ShapeDtypeStruct((n_j, rows * nt, LANES), BF16),
        scratch_shapes=[
            pltpu.VMEM((width, 2 * LANES), BF16),
            pltpu.VMEM((width, n_state), BF16),
            pltpu.VMEM((n_state, width), BF16),
            pltpu.VMEM((rows, n_state), F32),
            pltpu.VMEM((rows, n_state), F32),
            pltpu.VMEM((rows * S5_UNFOLD_PITCH, LANES), F32),
        ],
        compiler_params=pltpu.CompilerParams(
            dimension_semantics=("arbitrary",),
            vmem_limit_bytes=VMEM_LIMIT),
        name="s5",
    )(u2, kb, u_op, v_op, abar, dtile)


def _log_sigmoid(x):
    return jnp.minimum(x, 0.0) - jnp.log(1.0 + jnp.exp(-jnp.abs(x)))


def _gla_kernel(q_ref, k_ref, v_ref, gz_ref, gl_ref, gu_ref, gb_ref, nw_ref, o_ref, st_ref,
                *, chunk):
    n_heads, rows, dk = q_ref.shape
    n_chunks = rows // chunk
    d_gk = n_heads * dk

    @pl.when(pl.program_id(1) == 0)
    def _():
        st_ref[...] = jnp.zeros_like(st_ref)

    zg = jnp.dot(gl_ref[...], gu_ref[...], preferred_element_type=F32) + gb_ref[...]
    lg = _log_sigmoid(zg) * (1.0 / GLA_TAU)

    hi = lg.astype(BF16)
    lo = (lg - hi.astype(F32)).astype(BF16)
    r_id = lax.broadcasted_iota(jnp.int32, (chunk, chunk), 0)
    c_id = lax.broadcasted_iota(jnp.int32, (chunk, chunk), 1)
    causal = c_id <= r_id
    tri = jnp.where(causal, 1.0, 0.0).astype(BF16)
    tri2 = jnp.concatenate([tri, tri], axis=1)
    b_parts, last_parts = [], []
    for n in range(n_chunks):
        rs = slice(n * chunk, (n + 1) * chunk)
        b_n = jnp.dot(tri2, jnp.concatenate([hi[rs], lo[rs]], axis=0), preferred_element_type=F32)
        b_parts.append(b_n)
        last_parts.append(jnp.broadcast_to(b_n[chunk - 1:chunk], (chunk, d_gk)))
    b = jnp.concatenate(b_parts, axis=0)
    e_pos = jnp.exp(b)
    e_neg = jnp.exp(-b)
    decay = jnp.exp(jnp.concatenate(last_parts, axis=0))

    nt_dims = (((1,), (1,)), ((), ()))
    tn_dims = (((0,), (0,)), ((), ()))
    for h in range(n_heads):
        hs = slice(h * dk, (h + 1) * dk)
        q_e = (q_ref[h].astype(F32) * ((dk ** -0.5) * e_pos[:, hs])).astype(BF16)
        k_f = k_ref[h].astype(F32) * e_neg[:, hs]
        k_e = k_f.astype(BF16)
        k_t = (k_f * decay[:, hs]).astype(BF16)
        v = jnp.concatenate([v_ref[2 * h], v_ref[2 * h + 1]], axis=1)

        st = st_ref[h]
        outs = []
        for n in range(n_chunks):
            rs = slice(n * chunk, (n + 1) * chunk)
            attn = lax.dot_general(q_e[rs], k_e[rs], nt_dims, preferred_element_type=F32)
            attn = jnp.where(causal, attn, 0.0).astype(BF16)
            outs.append(jnp.dot(attn, v[rs], preferred_element_type=F32)
                        + lax.dot_general(q_e[rs], st.astype(BF16), nt_dims,
                                          preferred_element_type=F32))
            kv_t = lax.dot_general(v[rs], k_t[rs], tn_dims, preferred_element_type=F32)
            st = decay[n * chunk:n * chunk + 1, hs] * st + kv_t
        st_ref[h] = st

        o = jnp.concatenate(outs, axis=0)
        ms = jnp.mean(o * o, axis=-1, keepdims=True)
        o = o * lax.rsqrt(ms + EPS) * nw_ref[...]
        gz = jnp.concatenate([gz_ref[2 * h], gz_ref[2 * h + 1]], axis=1).astype(F32)
        y = (o * (gz * jax.nn.sigmoid(gz))).astype(BF16)
        o_ref[2 * h] = y[:, :LANES]
        o_ref[2 * h + 1] = y[:, LANES:]


def _gla(proj, g_low, gate_up_pad, gate_bias, norm_w, *, n_seq, seq_len, rows, slab_q, slab_k,
         slab_v, slab_gz):
    m = proj.shape[1]
    nb = seq_len // rows
    nh = GLA_HEADS
    dk = LANES
    dv = 2 * LANES
    tok = lambda b, n: b * nb + n
    return pl.pallas_call(
        functools.partial(_gla_kernel, chunk=GLA_CHUNK),
        grid=(n_seq, nb),
        in_specs=[
            pl.BlockSpec((nh, rows, LANES), lambda b, n: (slab_q // nh, tok(b, n), 0)),
            pl.BlockSpec((nh, rows, LANES), lambda b, n: (slab_k // nh, tok(b, n), 0)),
            pl.BlockSpec((2 * nh, rows, LANES), lambda b, n: (slab_v // (2 * nh), tok(b, n), 0)),
            pl.BlockSpec((2 * nh, rows, LANES), lambda b, n: (slab_gz // (2 * nh), tok(b, n), 0)),
            pl.BlockSpec((rows, LANES), lambda b, n: (tok(b, n), 0)),
            pl.BlockSpec((LANES, nh * dk), lambda b, n: (0, 0)),
            pl.BlockSpec((1, nh * dk), lambda b, n: (0, 0)),
            pl.BlockSpec((1, dv), lambda b, n: (0, 0)),
        ],
        out_specs=pl.BlockSpec((2 * nh, rows, LANES), lambda b, n: (0, tok(b, n), 0)),
        out_shape=jax.ShapeDtypeStruct((2 * nh, m, LANES), BF16),
        scratch_shapes=[pltpu.VMEM((nh, dv, dk), F32)],
        compiler_params=pltpu.CompilerParams(
            dimension_semantics=("parallel", "arbitrary"),
            vmem_limit_bytes=VMEM_LIMIT),
        name="gla",
    )(proj, proj, proj, proj, g_low, gate_up_pad, gate_bias, norm_w)


def _out_proj_kernel(g_ref, z_ref, yg_ref, x_ref, gw_ref, gb_ref, wo_ref, pw_ref, o_ref):
    n_s5 = g_ref.shape[0]
    d_s5 = n_s5 * LANES
    tm = x_ref.shape[0]
    rc = OUT_PROJ_ROW_CHUNK
    for c in range(tm // rc):
        rows = slice(c * rc, (c + 1) * rc)
        g = jnp.concatenate([g_ref[s, rows, :] for s in range(n_s5)], axis=1)
        gate = jax.nn.sigmoid(jnp.dot(g, gw_ref[...], preferred_element_type=F32) + gb_ref[...])
        z = jnp.concatenate([z_ref[s, rows, :] for s in range(n_s5)], axis=1).astype(F32)
        y_s5 = (g.astype(F32) * gate * (z * jax.nn.sigmoid(z))).astype(BF16)
        y_gla = jnp.concatenate([yg_ref[s, rows, :] for s in range(yg_ref.shape[0])], axis=1)
        mixed = (jnp.dot(y_s5, wo_ref[:d_s5, :], preferred_element_type=F32)
                 + jnp.dot(y_gla, wo_ref[d_s5:, :], preferred_element_type=F32))
        ms = jnp.mean(mixed * mixed, axis=-1, keepdims=True)
        o_ref[rows, :] = x_ref[rows, :] + mixed * lax.rsqrt(ms + EPS) * pw_ref[...]


def _out_proj(g_slabs, proj, y_gla, x2d, glu_w, glu_b, w_out, post_w, *, tm, slab_z):
    m, d = x2d.shape
    n_s5 = g_slabs.shape[0]
    n_gla = y_gla.shape[0]
    d_s5 = n_s5 * LANES
    d_mix = w_out.shape[0]
    return pl.pallas_call(
        _out_proj_kernel,
        grid=(m // tm,),
        in_specs=[
            pl.BlockSpec((n_s5, tm, LANES), lambda i: (0, i, 0)),
            pl.BlockSpec((n_s5, tm, LANES), lambda i: (slab_z // n_s5, i, 0)),
            pl.BlockSpec((n_gla, tm, LANES), lambda i: (0, i, 0)),
            pl.BlockSpec((tm, d), lambda i: (i, 0)),
            pl.BlockSpec((d_s5, d_s5), lambda i: (0, 0)),
            pl.BlockSpec((1, d_s5), lambda i: (0, 0)),
            pl.BlockSpec((d_mix, d), lambda i: (0, 0)),
            pl.BlockSpec((1, d), lambda i: (0, 0)),
        ],
        out_specs=pl.BlockSpec((tm, d), lambda i: (i, 0)),
        out_shape=jax.ShapeDtypeStruct((m, d), F32),
        compiler_params=pltpu.CompilerParams(
            dimension_semantics=("parallel",),
            vmem_limit_bytes=VMEM_LIMIT),
        name="out_proj",
    )(g_slabs, proj, y_gla, x2d, glu_w, glu_b, w_out, post_w)


def kernel(x, pre_norm_w, w_in, s5_A_re, s5_A_im, s5_B_re, s5_B_im, s5_C_re, s5_C_im, s5_D,
           s5_log_dt, s5_glu_w, s5_glu_b, gla_gate_up, gla_gate_bias, gla_norm_w, w_out,
           post_norm_w):
    bsz, seq_len, d_model = x.shape
    depth = w_in.shape[0]
    d_in = w_in.shape[2]
    d_s5 = s5_glu_w.shape[1]
    rank, d_gk = gla_gate_up.shape[1:]
    d_gv = GLA_HEADS * gla_norm_w.shape[1]
    m = bsz * seq_len
    d_main = 2 * d_s5 + 2 * d_gk + 2 * d_gv
    tn = d_s5
    assert d_s5 % LANES == 0 and d_gk == GLA_HEADS * LANES and d_gv == GLA_HEADS * 2 * LANES
    assert d_in == d_main + rank and rank <= LANES and d_main % tn == 0 and 2 * d_gk == tn
    assert seq_len % (S5_T * 8) == 0 and w_out.shape[1] == 2 * d_s5

    n_s5 = d_s5 // LANES
    slab_z = 0
    slab_q = n_s5
    slab_k = slab_q + d_gk // LANES
    slab_v = slab_k + d_gk // LANES
    slab_gz = slab_v + d_gv // LANES

    resid = x.astype(F32).reshape(m, d_model)
    for l in range(depth):
        kb, u_op, v_op, abar = _s5_ops(s5_A_re[l], s5_A_im[l], s5_B_re[l], s5_B_im[l],
                                       s5_C_re[l], s5_C_im[l], s5_log_dt[l])
        dtile = jnp.tile(s5_D[l].astype(F32).reshape(n_s5, 1, LANES), (1, 1, S5_T))

        w_bf = _cast_rows(jnp.swapaxes(w_in, 1, 2), l, tn=4 * LANES)
        u2, proj, g_low = _in_proj(resid, pre_norm_w[l].astype(F32)[None], w_bf,
                                   d_main=d_main, tm=1024, tn=tn)

        g_slabs = _s5(u2, kb, u_op, v_op, abar, dtile, n_seq=bsz)

        gate_up_pad = jnp.pad(gla_gate_up[l].astype(BF16), ((0, LANES - rank), (0, 0)))
        y_gla = _gla(proj, g_low, gate_up_pad, gla_gate_bias[l].astype(F32)[None],
                     gla_norm_w[l].astype(F32)[None], n_seq=bsz, seq_len=seq_len, rows=512,
                     slab_q=slab_q, slab_k=slab_k, slab_v=slab_v, slab_gz=slab_gz)

        resid = _out_proj(g_slabs, proj, y_gla, resid, s5_glu_w[l].astype(BF16),
                          s5_glu_b[l].astype(F32)[None], w_out[l].astype(BF16),
                          post_norm_w[l].astype(F32)[None], tm=512, slab_z=slab_z)
    return resid.reshape(bsz, seq_len, d_model).astype(x.dtype)
```

```python
import functools

import jax
import jax.numpy as jnp
from jax import lax
from jax.experimental import pallas as pl
from jax.experimental.pallas import tpu as pltpu

F32 = jnp.float32
BF16 = jnp.bfloat16

S5_GROUP = 16
GLA_HEADS = 4
GLA_TAU = 16.0
GLA_CHUNK = 64
EPS = 1e-6

LANES = 128
S5_T = 16
S5_UNFOLD_PITCH = 24
IN_PROJ_ROW_CHUNK = 256
OUT_PROJ_ROW_CHUNK = 256
VMEM_LIMIT = 56 * 1024 * 1024


NT_DIMS = (((1,), (1,)), ((), ()))
TN_DIMS = (((0,), (0,)), ((), ()))


def _cmul(ar, ai, br, bi):
    return ar * br - ai * bi, ar * bi + ai * br


def _s5_ops_kernel(tp_ref, np_ref, kb_ref, u_ref, v_ref, ab_ref, *, n_state):
    nt = S5_T
    half = LANES // 2

    def a_bar(are_raw, aim, ldt):
        are = jnp.minimum(are_raw, -1e-4)
        dt = jnp.exp(ldt)
        mag = jnp.exp(are * dt)
        return are, mag * jnp.cos(aim * dt), mag * jnp.sin(aim * dt)

    are, abr, abi = a_bar(tp_ref[0], tp_ref[1], tp_ref[2])
    aim = tp_ref[1]
    den = are * are + aim * aim
    nr = abr - 1.0
    fr = (nr * are + abi * aim) / den
    fi = (abi * are - nr * aim) / den
    xr, xi = _cmul(fr, fi, tp_ref[3], tp_ref[4])

    row = lax.broadcasted_iota(jnp.int32, (LANES, LANES), 0)
    lane = lax.broadcasted_iota(jnp.int32, (LANES, LANES), 1)
    own_half = ((row // S5_GROUP) % 2) == (lane // half)
    same_group = (row // S5_GROUP) == (lane // S5_GROUP)
    first_half = lane < half

    c_stack = jnp.concatenate([np_ref[3], np_ref[4]], axis=0)
    for tau in range(nt):
        lhs = jnp.where(first_half, xr, -xi)
        ker = jnp.dot(lhs, c_stack, preferred_element_type=F32, precision=lax.Precision.HIGHEST)
        kb_ref[tau] = jnp.where(same_group, ker, 0.0).astype(BF16)
        u_ref[nt - 1 - tau, 0] = jnp.where(own_half, xr, 0.0).astype(BF16)
        u_ref[nt - 1 - tau, 1] = jnp.where(own_half, xi, 0.0).astype(BF16)
        if tau + 1 < nt:
            xr, xi = _cmul(xr, xi, abr, abi)

    pr, pi = abr, abi
    for _ in range(4):
        pr, pi = _cmul(pr, pi, pr, pi)
    assert nt == 16
    pairs = LANES // (2 * S5_GROUP)
    for part, val in enumerate((pr, pi)):
        for a in range(pairs):
            r0 = 2 * a * S5_GROUP
            piece = jnp.where(first_half[:1], val[r0:r0 + 1], val[r0 + S5_GROUP:r0 + S5_GROUP + 1])
            c0 = part * (n_state // 2) + a * LANES
            ab_ref[:, c0:c0 + LANES] = piece

    _, nbr, nbi = a_bar(np_ref[0], np_ref[1], np_ref[2])
    cr, ci = np_ref[3], np_ref[4]
    lane_n = lax.broadcasted_iota(jnp.int32, (half, LANES), 1)
    parity = (lane_n // S5_GROUP) % 2
    pr, pi = nbr, nbi
    for t in range(nt):
        wr, wi = _cmul(cr, ci, pr, pi)
        for q in range(2):
            rows = slice(q * half, (q + 1) * half)
            v_ref[t, 0, rows, :] = jnp.where(parity == q, wr, 0.0).astype(BF16)
            v_ref[t, 1, rows, :] = jnp.where(parity == q, -wi, 0.0).astype(BF16)
        if t + 1 < nt:
            pr, pi = _cmul(pr, pi, nbr, nbi)


def _s5_ops(a_re, a_im, b_re, b_im, c_re, c_im, log_dt):
    g, p = a_re.shape
    h = S5_GROUP
    gps = LANES // h
    n_j = g // gps
    assert 2 * p == LANES and g % gps == 0
    n_state = 2 * gps * p
    f = lambda t: t.astype(F32)

    def t_arr(t):
        t = f(t).reshape(n_j, gps, -1, 1, p)
        return jnp.broadcast_to(t, (n_j, gps, h, 2, p)).reshape(n_j, LANES, LANES)

    def n_arr(t):
        t = jnp.swapaxes(f(t).reshape(n_j, gps, -1, p), 1, 3)
        t = jnp.swapaxes(jnp.broadcast_to(t, (n_j, p, h, gps)), 2, 3)
        return t.reshape(n_j, p, LANES)

    ldt = jnp.broadcast_to(f(log_dt)[:, None], (g, p))
    tpar = jnp.stack([t_arr(a_re), t_arr(a_im), t_arr(ldt),
                      t_arr(jnp.swapaxes(b_re, 1, 2)), t_arr(jnp.swapaxes(b_im, 1, 2))], axis=1)
    npar = jnp.stack([n_arr(a_re), n_arr(a_im), n_arr(ldt), n_arr(c_re), n_arr(c_im)], axis=1)
    nt = S5_T
    return pl.pallas_call(
        functools.partial(_s5_ops_kernel, n_state=n_state),
        grid=(n_j,),
        in_specs=[
            pl.BlockSpec((None, 5, LANES, LANES), lambda j: (j, 0, 0, 0)),
            pl.BlockSpec((None, 5, p, LANES), lambda j: (j, 0, 0, 0)),
        ],
        out_specs=[
            pl.BlockSpec((None, nt, LANES, LANES), lambda j: (j, 0, 0, 0)),
            pl.BlockSpec((None, nt, 2, LANES, LANES), lambda j: (j, 0, 0, 0, 0)),
            pl.BlockSpec((None, nt, 2, LANES, LANES), lambda j: (j, 0, 0, 0, 0)),
            pl.BlockSpec((None, 1, n_state), lambda j: (j, 0, 0)),
        ],
        out_shape=[
            jax.ShapeDtypeStruct((n_j, nt, LANES, LANES), BF16),
            jax.ShapeDtypeStruct((n_j, nt, 2, LANES, LANES), BF16),
            jax.ShapeDtypeStruct((n_j, nt, 2, LANES, LANES), BF16),
            jax.ShapeDtypeStruct((n_j, 1, n_state), F32),
        ],
        compiler_params=pltpu.CompilerParams(dimension_semantics=("parallel",)),
        name="s5_ops",
    )(tpar, npar)


def _in_proj_kernel(x_ref, nw_ref, w_ref, u_ref, p_ref, l_ref, h_ref, s_ref, *, d_main, tn):
    tm = x_ref.shape[0]
    n_slabs = tn // LANES
    fold = tm // S5_T
    xf = x_ref[...]
    h_ref[...] = (xf * nw_ref[...]).astype(BF16)
    rs = lax.rsqrt(jnp.mean(xf * xf, axis=-1, keepdims=True) + EPS)

    for n in range(d_main // tn):
        res = lax.dot_general(h_ref[...], w_ref[n * tn:(n + 1) * tn, :], NT_DIMS,
                              preferred_element_type=F32) * rs
        if n == 0:
            for s in range(n_slabs):
                s_ref[s] = res[:, s * LANES:(s + 1) * LANES]
            for s in range(n_slabs):
                for t in range(S5_T):
                    u_ref[s, :, t * LANES:(t + 1) * LANES] = (
                        s_ref[s, pl.ds(t, fold, stride=S5_T), :].astype(BF16))
        else:
            for s in range(n_slabs):
                p_ref[(n - 1) * n_slabs + s] = res[:, s * LANES:(s + 1) * LANES].astype(BF16)
    low = lax.dot_general(h_ref[...], w_ref[d_main:d_main + LANES, :], NT_DIMS,
                          preferred_element_type=F32) * rs
    l_ref[...] = low.astype(BF16)


def _cast_rows_kernel(w_ref, o_ref, *, n_valid):
    tn = o_ref.shape[0]
    row = pl.program_id(0) * tn + lax.broadcasted_iota(jnp.int32, (tn, 1), 0)
    o_ref[...] = jnp.where(row < n_valid, w_ref[...], 0.0).astype(BF16)


def _cast_rows(w3d, layer, *, tn):
    _, n, d = w3d.shape
    n_blocks = pl.cdiv(n, tn)
    return pl.pallas_call(
        functools.partial(_cast_rows_kernel, n_valid=n),
        grid=(n_blocks,),
        in_specs=[pl.BlockSpec((None, tn, d), lambda c: (layer, c, 0))],
        out_specs=pl.BlockSpec((tn, d), lambda c: (c, 0)),
        out_shape=jax.ShapeDtypeStruct((n_blocks * tn, d), BF16),
        compiler_params=pltpu.CompilerParams(dimension_semantics=("parallel",)),
        name="cast_rows",
    )(w3d)


def _in_proj(x2d, norm_w, w_bf, *, d_main, tm, tn):
    m, d = x2d.shape
    n_slabs = tn // LANES
    n_proj = (d_main // tn - 1) * n_slabs
    return pl.pallas_call(
        functools.partial(_in_proj_kernel, d_main=d_main, tn=tn),
        grid=(m // tm,),
        in_specs=[
            pl.BlockSpec((tm, d), lambda i: (i, 0)),
            pl.BlockSpec((1, d), lambda i: (0, 0)),
            pl.BlockSpec(w_bf.shape, lambda i: (0, 0), pipeline_mode=pl.Buffered(1)),
        ],
        out_specs=[
            pl.BlockSpec((n_slabs, tm // S5_T, S5_T * LANES), lambda i: (0, i, 0)),
            pl.BlockSpec((n_proj, tm, LANES), lambda i: (0, i, 0)),
            pl.BlockSpec((tm, LANES), lambda i: (i, 0)),
        ],
        out_shape=[
            jax.ShapeDtypeStruct((n_slabs, m // S5_T, S5_T * LANES), BF16),
            jax.ShapeDtypeStruct((n_proj, m, LANES), BF16),
            jax.ShapeDtypeStruct((m, LANES), BF16),
        ],
        scratch_shapes=[pltpu.VMEM((tm, d), BF16), pltpu.VMEM((n_slabs, tm, LANES), F32)],
        compiler_params=pltpu.CompilerParams(
            dimension_semantics=("parallel",),
            vmem_limit_bytes=VMEM_LIMIT),
        name="in_proj",
    )(x2d, norm_w, w_bf)


def _s5_kernel(x_ref, kb_ref, u_ref, v_ref, a_ref, d_ref, o_ref,
               m_ref, wz_ref, wy_ref, z_ref, sp_ref, y_ref, *, rows_per_seq, n_seq):
    nt = S5_T
    rows = x_ref.shape[0]
    half = a_ref.shape[-1] // 2
    pairs = half // LANES
    sub = LANES // pairs

    wide = 2 * LANES

    @pl.when(pl.program_id(0) == 0)
    def _():
        m_ref[(nt - 1) * LANES:, :LANES] = jnp.zeros((LANES, LANES), BF16)
        wz_ref[...] = jnp.zeros_like(wz_ref)
        y_ref[...] = jnp.zeros_like(y_ref)

    for s in range(nt):
        for k in range(2):
            lag = nt - 2 + k - s
            if lag >= 0:
                m_ref[s * LANES:(s + 1) * LANES, k * LANES:(k + 1) * LANES] = kb_ref[lag]

    lane = lax.broadcasted_iota(jnp.int32, (LANES, LANES), 1)
    for s in range(nt):
        for part in range(2):
            for a in range(pairs):
                c0 = part * half + a * LANES
                wz_ref[s * LANES + a * sub:s * LANES + (a + 1) * sub, c0:c0 + LANES] = (
                    u_ref[s, part, a * sub:(a + 1) * sub, :])
    for t in range(nt):
        for part in range(2):
            blk = v_ref[t, part]
            for a in range(pairs):
                r0 = part * half + a * LANES
                wy_ref[r0:r0 + LANES, t * LANES:(t + 1) * LANES] = (
                    jnp.where((lane // sub) == a, blk, jnp.zeros_like(blk)))

    z_ref[...] = jnp.dot(x_ref[...], wz_ref[...], preferred_element_type=F32)

    a_re = a_ref[:, :half]
    a_im = a_ref[:, half:]

    def body(c, carry):
        new = []
        for b in range(n_seq):
            s_re, s_im = carry[2 * b], carry[2 * b + 1]
            r = b * rows_per_seq + c
            sp_ref[pl.ds(r, 1), :half] = s_re
            sp_ref[pl.ds(r, 1), half:] = s_im
            z_re = z_ref[pl.ds(r, 1), :half]
            z_im = z_ref[pl.ds(r, 1), half:]
            new.append(a_re * s_re - a_im * s_im + z_re)
            new.append(a_re * s_im + a_im * s_re + z_im)
        return tuple(new)

    zero = jnp.zeros((1, half), F32)
    lax.fori_loop(0, rows_per_seq, body, (zero,) * (2 * n_seq), unroll=8)

    spb = sp_ref[...].astype(BF16)
    pitch = y_ref.shape[0] // rows
    for n in range(nt // 2):
        cols = slice(n * wide, (n + 1) * wide)
        kk = (n + 1) * wide
        acc = jnp.dot(x_ref[:, :kk], m_ref[nt * LANES - kk:, :], preferred_element_type=F32)
        acc = acc + jnp.dot(spb, wy_ref[:, cols], preferred_element_type=F32)
        acc = acc + d_ref[:, cols] * x_ref[:, cols].astype(F32)
        y = jax.nn.gelu(acc)
        for k in range(2):
            y_ref[pl.ds(2 * n + k, rows, stride=pitch), :] = y[:, k * LANES:(k + 1) * LANES]
    tokens = y_ref[...].reshape(rows, pitch, LANES)[:, :nt, :].reshape(rows * nt, LANES)
    o_ref[...] = tokens.astype(BF16)


def _s5(u2, kb, u_op, v_op, abar, dtile, *, n_seq):
    n_j, nt, _, _ = kb.shape
    rows = u2.shape[1]
    width = nt * LANES
    n_state = abar.shape[-1]
    return pl.pallas_call(
        functools.partial(_s5_kernel, rows_per_seq=rows // n_seq, n_seq=n_seq),
        grid=(n_j,),
        in_specs=[
            pl.BlockSpec((None, rows, width), lambda j: (j, 0, 0)),
            pl.BlockSpec((None, nt, LANES, LANES), lambda j: (j, 0, 0, 0)),
            pl.BlockSpec((None, nt, 2, LANES, LANES), lambda j: (j, 0, 0, 0, 0)),
            pl.BlockSpec((None, nt, 2, LANES, LANES), lambda j: (j, 0, 0, 0, 0)),
            pl.BlockSpec((None, 1, n_state), lambda j: (j, 0, 0)),
            pl.BlockSpec((None, 1, width), lambda j: (j, 0, 0)),
        ],
        out_specs=pl.BlockSpec((None, rows * nt, LANES), lambda j: (j, 0, 0)),
        out_shape=jax.ShapeDtypeStruct((n_j, rows * nt, LANES), BF16),
        scratch_shapes=[
            pltpu.VMEM((width, 2 * LANES), BF16),
            pltpu.VMEM((width, n_state), BF16),
            pltpu.VMEM((n_state, width), BF16),
            pltpu.VMEM((rows, n_state), F32),
            pltpu.VMEM((rows, n_state), F32),
            pltpu.VMEM((rows * S5_UNFOLD_PITCH, LANES), F32),
        ],
        compiler_params=pltpu.CompilerParams(
            dimension_semantics=("arbitrary",),
            vmem_limit_bytes=VMEM_LIMIT),
        name="s5",
    )(u2, kb, u_op, v_op, abar, dtile)


def _log_sigmoid(x):
    return jnp.minimum(x, 0.0) - jnp.log(1.0 + jnp.exp(-jnp.abs(x)))


def _gla_kernel(q_ref, k_ref, v_ref, gz_ref, gl_ref, gu_ref, gb_ref, nw_ref, o_ref, st_ref,
                *, chunk):
    n_heads, rows, dk = q_ref.shape
    n_chunks = rows // chunk
    d_gk = n_heads * dk

    @pl.when(pl.program_id(1) == 0)
    def _():
        st_ref[...] = jnp.zeros_like(st_ref)

    zg = jnp.dot(gl_ref[...], gu_ref[...], preferred_element_type=F32) + gb_ref[...]
    lg = _log_sigmoid(zg) * (1.0 / GLA_TAU)

    hi = lg.astype(BF16)
    lo = (lg - hi.astype(F32)).astype(BF16)
    r_id = lax.broadcasted_iota(jnp.int32, (chunk, chunk), 0)
    c_id = lax.broadcasted_iota(jnp.int32, (chunk, chunk), 1)
    causal = c_id <= r_id
    tri = jnp.where(causal, 1.0, 0.0).astype(BF16)
    tri2 = jnp.concatenate([tri, tri], axis=1)
    b_parts, last_parts = [], []
    for n in range(n_chunks):
        rs = slice(n * chunk, (n + 1) * chunk)
        b_n = jnp.dot(tri2, jnp.concatenate([hi[rs], lo[rs]], axis=0), preferred_element_type=F32)
        b_parts.append(b_n)
        last_parts.append(jnp.broadcast_to(b_n[chunk - 1:chunk], (chunk, d_gk)))
    b = jnp.concatenate(b_parts, axis=0)
    e_pos = jnp.exp(b)
    e_neg = jnp.exp(-b)
    decay = jnp.exp(jnp.concatenate(last_parts, axis=0))

    nt_dims = (((1,), (1,)), ((), ()))
    tn_dims = (((0,), (0,)), ((), ()))
    for h in range(n_heads):
        hs = slice(h * dk, (h + 1) * dk)
        q_e = (q_ref[h].astype(F32) * ((dk ** -0.5) * e_pos[:, hs])).astype(BF16)
        k_f = k_ref[h].astype(F32) * e_neg[:, hs]
        k_e = k_f.astype(BF16)
        k_t = (k_f * decay[:, hs]).astype(BF16)
        v = jnp.concatenate([v_ref[2 * h], v_ref[2 * h + 1]], axis=1)

        st = st_ref[h]
        outs = []
        for n in range(n_chunks):
            rs = slice(n * chunk, (n + 1) * chunk)
            attn = lax.dot_general(q_e[rs], k_e[rs], nt_dims, preferred_element_type=F32)
            attn = jnp.where(causal, attn, 0.0).astype(BF16)
            outs.append(jnp.dot(attn, v[rs], preferred_element_type=F32)
                        + lax.dot_general(q_e[rs], st.astype(BF16), nt_dims,
                                          preferred_element_type=F32))
            kv_t = lax.dot_general(v[rs], k_t[rs], tn_dims, preferred_element_type=F32)
            st = decay[n * chunk:n * chunk + 1, hs] * st + kv_t
        st_ref[h] = st

        o = jnp.concatenate(outs, axis=0)
        ms = jnp.mean(o * o, axis=-1, keepdims=True)
        o = o * lax.rsqrt(ms + EPS) * nw_ref[...]
        gz = jnp.concatenate([gz_ref[2 * h], gz_ref[2 * h + 1]], axis=1).astype(F32)
        y = (o * (gz * jax.nn.sigmoid(gz))).astype(BF16)
        o_ref[2 * h] = y[:, :LANES]
        o_ref[2 * h + 1] = y[:, LANES:]


def _gla(proj, g_low, gate_up_pad, gate_bias, norm_w, *, n_seq, seq_len, rows, slab_q, slab_k,
         slab_v, slab_gz):
    m = proj.shape[1]
    nb = seq_len // rows
    nh = GLA_HEADS
    dk = LANES
    dv = 2 * LANES
    tok = lambda b, n: b * nb + n
    return pl.pallas_call(
        functools.partial(_gla_kernel, chunk=GLA_CHUNK),
        grid=(n_seq, nb),
        in_specs=[
            pl.BlockSpec((nh, rows, LANES), lambda b, n: (slab_q // nh, tok(b, n), 0)),
            pl.BlockSpec((nh, rows, LANES), lambda b, n: (slab_k // nh, tok(b, n), 0)),
            pl.BlockSpec((2 * nh, rows, LANES), lambda b, n: (slab_v // (2 * nh), tok(b, n), 0)),
            pl.BlockSpec((2 * nh, rows, LANES), lambda b, n: (slab_gz // (2 * nh), tok(b, n), 0)),
            pl.BlockSpec((rows, LANES), lambda b, n: (tok(b, n), 0)),
            pl.BlockSpec((LANES, nh * dk), lambda b, n: (0, 0)),
            pl.BlockSpec((1, nh * dk), lambda b, n: (0, 0)),
            pl.BlockSpec((1, dv), lambda b, n: (0, 0)),
        ],
        out_specs=pl.BlockSpec((2 * nh, rows, LANES), lambda b, n: (0, tok(b, n), 0)),
        out_shape=jax.ShapeDtypeStruct((2 * nh, m, LANES), BF16),
        scratch_shapes=[pltpu.VMEM((nh, dv, dk), F32)],
        compiler_params=pltpu.CompilerParams(
            dimension_semantics=("parallel", "arbitrary"),
            vmem_limit_bytes=VMEM_LIMIT),
        name="gla",
    )(proj, proj, proj, proj, g_low, gate_up_pad, gate_bias, norm_w)


def _out_proj_kernel(g_ref, z_ref, yg_ref, x_ref, gw_ref, gb_ref, wo_ref, pw_ref, o_ref):
    n_s5 = g_ref.shape[0]
    d_s5 = n_s5 * LANES
    tm = x_ref.shape[0]
    rc = OUT_PROJ_ROW_CHUNK
    for c in range(tm // rc):
        rows = slice(c * rc, (c + 1) * rc)
        g = jnp.concatenate([g_ref[s, rows, :] for s in range(n_s5)], axis=1)
        gate = jax.nn.sigmoid(jnp.dot(g, gw_ref[...], preferred_element_type=F32) + gb_ref[...])
        z = jnp.concatenate([z_ref[s, rows, :] for s in range(n_s5)], axis=1).astype(F32)
        y_s5 = (g.astype(F32) * gate * (z * jax.nn.sigmoid(z))).astype(BF16)
        y_gla = jnp.concatenate([yg_ref[s, rows, :] for s in range(yg_ref.shape[0])], axis=1)
        mixed = (jnp.dot(y_s5, wo_ref[:d_s5, :], preferred_element_type=F32)
                 + jnp.dot(y_gla, wo_ref[d_s5:, :], preferred_element_type=F32))
        ms = jnp.mean(mixed * mixed, axis=-1, keepdims=True)
        o_ref[rows, :] = x_ref[rows, :] + mixed * lax.rsqrt(ms + EPS) * pw_ref[...]


def _out_proj(g_slabs, proj, y_gla, x2d, glu_w, glu_b, w_out, post_w, *, tm, slab_z):
    m, d = x2d.shape
    n_s5 = g_slabs.shape[0]
    n_gla = y_gla.shape[0]
    d_s5 = n_s5 * LANES
    d_mix = w_out.shape[0]
    return pl.pallas_call(
        _out_proj_kernel,
        grid=(m // tm,),
        in_specs=[
            pl.BlockSpec((n_s5, tm, LANES), lambda i: (0, i, 0)),
            pl.BlockSpec((n_s5, tm, LANES), lambda i: (slab_z // n_s5, i, 0)),
            pl.BlockSpec((n_gla, tm, LANES), lambda i: (0, i, 0)),
            pl.BlockSpec((tm, d), lambda i: (i, 0)),
            pl.BlockSpec((d_s5, d_s5), lambda i: (0, 0)),
            pl.BlockSpec((1, d_s5), lambda i: (0, 0)),
            pl.BlockSpec((d_mix, d), lambda i: (0, 0)),
            pl.BlockSpec((1, d), lambda i: (0, 0)),
        ],
        out_specs=pl.BlockSpec((tm, d), lambda i: (i, 0)),
        out_shape=jax.ShapeDtypeStruct((m, d), F32),
        compiler_params=pltpu.CompilerParams(
            dimension_semantics=("parallel",),
            vmem_limit_bytes=VMEM_LIMIT),
        name="out_proj",
    )(g_slabs, proj, y_gla, x2d, glu_w, glu_b, w_out, post_w)


def kernel(x, pre_norm_w, w_in, s5_A_re, s5_A_im, s5_B_re, s5_B_im, s5_C_re, s5_C_im, s5_D,
           s5_log_dt, s5_glu_w, s5_glu_b, gla_gate_up, gla_gate_bias, gla_norm_w, w_out,
           post_norm_w):
    bsz, seq_len, d_model = x.shape
    depth = w_in.shape[0]
    d_in = w_in.shape[2]
    d_s5 = s5_glu_w.shape[1]
    rank, d_gk = gla_gate_up.shape[1:]
    d_gv = GLA_HEADS * gla_norm_w.shape[1]
    m = bsz * seq_len
    d_main = 2 * d_s5 + 2 * d_gk + 2 * d_gv
    tn = d_s5
    assert d_s5 % LANES == 0 and d_gk == GLA_HEADS * LANES and d_gv == GLA_HEADS * 2 * LANES
    assert d_in == d_main + rank and rank <= LANES and d_main % tn == 0 and 2 * d_gk == tn
    assert seq_len % (S5_T * 8) == 0 and w_out.shape[1] == 2 * d_s5

    n_s5 = d_s5 // LANES
    slab_z = 0
    slab_q = n_s5
    slab_k = slab_q + d_gk // LANES
    slab_v = slab_k + d_gk // LANES
    slab_gz = slab_v + d_gv // LANES

    resid = x.astype(F32).reshape(m, d_model)
    for l in range(depth):
        kb, u_op, v_op, abar = _s5_ops(s5_A_re[l], s5_A_im[l], s5_B_re[l], s5_B_im[l],
                                       s5_C_re[l], s5_C_im[l], s5_log_dt[l])
        dtile = jnp.tile(s5_D[l].astype(F32).reshape(n_s5, 1, LANES), (1, 1, S5_T))

        w_bf = _cast_rows(jnp.swapaxes(w_in, 1, 2), l, tn=4 * LANES)
        u2, proj, g_low = _in_proj(resid, pre_norm_w[l].astype(F32)[None], w_bf,
                                   d_main=d_main, tm=512, tn=tn)

        g_slabs = _s5(u2, kb, u_op, v_op, abar, dtile, n_seq=bsz)

        gate_up_pad = jnp.pad(gla_gate_up[l].astype(BF16), ((0, LANES - rank), (0, 0)))
        y_gla = _gla(proj, g_low, gate_up_pad, gla_gate_bias[l].astype(F32)[None],
                     gla_norm_w[l].astype(F32)[None], n_seq=bsz, seq_len=seq_len, rows=512,
                     slab_q=slab_q, slab_k=slab_k, slab_v=slab_v, slab_gz=slab_gz)

        resid = _out_proj(g_slabs, proj, y_gla, resid, s5_glu_w[l].astype(BF16),
                          s5_glu_b[l].astype(F32)[None], w_out[l].astype(BF16),
                          post_norm_w[l].astype(F32)[None], tm=512, slab_z=slab_z)
    return resid.reshape(bsz, seq_len, d_model).astype(x.dtype)
```

```python
import functools

import jax
import jax.numpy as jnp
from jax import lax
from jax.experimental import pallas as pl
from jax.experimental.pallas import tpu as pltpu

F32 = jnp.float32
BF16 = jnp.bfloat16

S5_GROUP = 16
GLA_HEADS = 4
GLA_TAU = 16.0
GLA_CHUNK = 64
EPS = 1e-6

LANES = 128
S5_T = 16
S5_UNFOLD_PITCH = 24
IN_PROJ_ROW_CHUNK = 256
OUT_PROJ_ROW_CHUNK = 256
VMEM_LIMIT = 56 * 1024 * 1024


NT_DIMS = (((1,), (1,)), ((), ()))
TN_DIMS = (((0,), (0,)), ((), ()))


def _cmul(ar, ai, br, bi):
    return ar * br - ai * bi, ar * bi + ai * br


def _s5_ops_kernel(tp_ref, np_ref, kb_ref, u_ref, v_ref, ab_ref, *, n_state):
    nt = S5_T
    half = LANES // 2

    def a_bar(are_raw, aim, ldt):
        are = jnp.minimum(are_raw, -1e-4)
        dt = jnp.exp(ldt)
        mag = jnp.exp(are * dt)
        return are, mag * jnp.cos(aim * dt), mag * jnp.sin(aim * dt)

    are, abr, abi = a_bar(tp_ref[0], tp_ref[1], tp_ref[2])
    aim = tp_ref[1]
    den = are * are + aim * aim
    nr = abr - 1.0
    fr = (nr * are + abi * aim) / den
    fi = (abi * are - nr * aim) / den
    xr, xi = _cmul(fr, fi, tp_ref[3], tp_ref[4])

    row = lax.broadcasted_iota(jnp.int32, (LANES, LANES), 0)
    lane = lax.broadcasted_iota(jnp.int32, (LANES, LANES), 1)
    own_half = ((row // S5_GROUP) % 2) == (lane // half)
    same_group = (row // S5_GROUP) == (lane // S5_GROUP)
    first_half = lane < half

    c_stack = jnp.concatenate([np_ref[3], np_ref[4]], axis=0)
    for tau in range(nt):
        lhs = jnp.where(first_half, xr, -xi)
        ker = jnp.dot(lhs, c_stack, preferred_element_type=F32, precision=lax.Precision.HIGHEST)
        kb_ref[tau] = jnp.where(same_group, ker, 0.0).astype(BF16)
        u_ref[nt - 1 - tau, 0] = jnp.where(own_half, xr, 0.0).astype(BF16)
        u_ref[nt - 1 - tau, 1] = jnp.where(own_half, xi, 0.0).astype(BF16)
        if tau + 1 < nt:
            xr, xi = _cmul(xr, xi, abr, abi)

    pr, pi = abr, abi
    for _ in range(4):
        pr, pi = _cmul(pr, pi, pr, pi)
    assert nt == 16
    pairs = LANES // (2 * S5_GROUP)
    for part, val in enumerate((pr, pi)):
        for a in range(pairs):
            r0 = 2 * a * S5_GROUP
            piece = jnp.where(first_half[:1], val[r0:r0 + 1], val[r0 + S5_GROUP:r0 + S5_GROUP + 1])
            c0 = part * (n_state // 2) + a * LANES
            ab_ref[:, c0:c0 + LANES] = piece

    _, nbr, nbi = a_bar(np_ref[0], np_ref[1], np_ref[2])
    cr, ci = np_ref[3], np_ref[4]
    lane_n = lax.broadcasted_iota(jnp.int32, (half, LANES), 1)
    parity = (lane_n // S5_GROUP) % 2
    pr, pi = nbr, nbi
    for t in range(nt):
        wr, wi = _cmul(cr, ci, pr, pi)
        for q in range(2):
            rows = slice(q * half, (q + 1) * half)
            v_ref[t, 0, rows, :] = jnp.where(parity == q, wr, 0.0).astype(BF16)
            v_ref[t, 1, rows, :] = jnp.where(parity == q, -wi, 0.0).astype(BF16)
        if t + 1 < nt:
            pr, pi = _cmul(pr, pi, nbr, nbi)


def _s5_ops(a_re, a_im, b_re, b_im, c_re, c_im, log_dt):
    g, p = a_re.shape
    h = S5_GROUP
    gps = LANES // h
    n_j = g // gps
    assert 2 * p == LANES and g % gps == 0
    n_state = 2 * gps * p
    f = lambda t: t.astype(F32)

    def t_arr(t):
        t = f(t).reshape(n_j, gps, -1, 1, p)
        return jnp.broadcast_to(t, (n_j, gps, h, 2, p)).reshape(n_j, LANES, LANES)

    def n_arr(t):
        t = jnp.swapaxes(f(t).reshape(n_j, gps, -1, p), 1, 3)
        t = jnp.swapaxes(jnp.broadcast_to(t, (n_j, p, h, gps)), 2, 3)
        return t.reshape(n_j, p, LANES)

    ldt = jnp.broadcast_to(f(log_dt)[:, None], (g, p))
    tpar = jnp.stack([t_arr(a_re), t_arr(a_im), t_arr(ldt),
                      t_arr(jnp.swapaxes(b_re, 1, 2)), t_arr(jnp.swapaxes(b_im, 1, 2))], axis=1)
    npar = jnp.stack([n_arr(a_re), n_arr(a_im), n_arr(ldt), n_arr(c_re), n_arr(c_im)], axis=1)
    nt = S5_T
    return pl.pallas_call(
        functools.partial(_s5_ops_kernel, n_state=n_state),
        grid=(n_j,),
        in_specs=[
            pl.BlockSpec((None, 5, LANES, LANES), lambda j: (j, 0, 0, 0)),
            pl.BlockSpec((None, 5, p, LANES), lambda j: (j, 0, 0, 0)),
        ],
        out_specs=[
            pl.BlockSpec((None, nt, LANES, LANES), lambda j: (j, 0, 0, 0)),
            pl.BlockSpec((None, nt, 2, LANES, LANES), lambda j: (j, 0, 0, 0, 0)),
            pl.BlockSpec((None, nt, 2, LANES, LANES), lambda j: (j, 0, 0, 0, 0)),
            pl.BlockSpec((None, 1, n_state), lambda j: (j, 0, 0)),
        ],
        out_shape=[
            jax.ShapeDtypeStruct((n_j, nt, LANES, LANES), BF16),
            jax.ShapeDtypeStruct((n_j, nt, 2, LANES, LANES), BF16),
            jax.ShapeDtypeStruct((n_j, nt, 2, LANES, LANES), BF16),
            jax.ShapeDtypeStruct((n_j, 1, n_state), F32),
        ],
        compiler_params=pltpu.CompilerParams(dimension_semantics=("parallel",)),
        name="s5_ops",
    )(tpar, npar)


def _in_proj_kernel(x_ref, nw_ref, w_ref, u_ref, p_ref, l_ref, h_ref, s_ref, *, d_main, tn):
    tm = x_ref.shape[0]
    n_slabs = tn // LANES
    fold = tm // S5_T
    xf = x_ref[...]
    h_ref[...] = (xf * nw_ref[...]).astype(BF16)
    rs = lax.rsqrt(jnp.mean(xf * xf, axis=-1, keepdims=True) + EPS)

    for n in range(d_main // tn):
        res = lax.dot_general(h_ref[...], w_ref[n * tn:(n + 1) * tn, :], NT_DIMS,
                              preferred_element_type=F32) * rs
        if n == 0:
            for s in range(n_slabs):
                s_ref[s] = res[:, s * LANES:(s + 1) * LANES]
            for s in range(n_slabs):
                for t in range(S5_T):
                    u_ref[s, :, t * LANES:(t + 1) * LANES] = (
                        s_ref[s, pl.ds(t, fold, stride=S5_T), :].astype(BF16))
        else:
            for s in range(n_slabs):
                p_ref[(n - 1) * n_slabs + s] = res[:, s * LANES:(s + 1) * LANES].astype(BF16)
    low = lax.dot_general(h_ref[...], w_ref[d_main:d_main + LANES, :], NT_DIMS,
                          preferred_element_type=F32) * rs
    l_ref[...] = low.astype(BF16)


def _cast_rows_kernel(w_ref, o_ref, *, n_valid):
    tn = o_ref.shape[0]
    row = pl.program_id(0) * tn + lax.broadcasted_iota(jnp.int32, (tn, 1), 0)
    o_ref[...] = jnp.where(row < n_valid, w_ref[...], 0.0).astype(BF16)


def _cast_rows(w3d, layer, *, tn):
    _, n, d = w3d.shape
    n_blocks = pl.cdiv(n, tn)
    return pl.pallas_call(
        functools.partial(_cast_rows_kernel, n_valid=n),
        grid=(n_blocks,),
        in_specs=[pl.BlockSpec((None, tn, d), lambda c: (layer, c, 0))],
        out_specs=pl.BlockSpec((tn, d), lambda c: (c, 0)),
        out_shape=jax.ShapeDtypeStruct((n_blocks * tn, d), BF16),
        compiler_params=pltpu.CompilerParams(dimension_semantics=("parallel",)),
        name="cast_rows",
    )(w3d)


def _in_proj(x2d, norm_w, w_bf, *, d_main, tm, tn):
    m, d = x2d.shape
    n_slabs = tn // LANES
    n_proj = (d_main // tn - 1) * n_slabs
    return pl.pallas_call(
        functools.partial(_in_proj_kernel, d_main=d_main, tn=tn),
        grid=(m // tm,),
        in_specs=[
            pl.BlockSpec((tm, d), lambda i: (i, 0)),
            pl.BlockSpec((1, d), lambda i: (0, 0)),
            pl.BlockSpec(w_bf.shape, lambda i: (0, 0), pipeline_mode=pl.Buffered(1)),
        ],
        out_specs=[
            pl.BlockSpec((n_slabs, tm // S5_T, S5_T * LANES), lambda i: (0, i, 0)),
            pl.BlockSpec((n_proj, tm, LANES), lambda i: (0, i, 0)),
            pl.BlockSpec((tm, LANES), lambda i: (i, 0)),
        ],
        out_shape=[
            jax.ShapeDtypeStruct((n_slabs, m // S5_T, S5_T * LANES), BF16),
            jax.ShapeDtypeStruct((n_proj, m, LANES), BF16),
            jax.ShapeDtypeStruct((m, LANES), BF16),
        ],
        scratch_shapes=[pltpu.VMEM((tm, d), BF16), pltpu.VMEM((n_slabs, tm, LANES), F32)],
        compiler_params=pltpu.CompilerParams(
            dimension_semantics=("parallel",),
            vmem_limit_bytes=VMEM_LIMIT),
        name="in_proj",
    )(x2d, norm_w, w_bf)


def _s5_kernel(x_ref, kb_ref, u_ref, v_ref, a_ref, d_ref, o_ref,
               m_ref, wz_ref, wy_ref, z_ref, sp_ref, y_ref, *, rows_per_seq, n_seq):
    nt = S5_T
    rows = x_ref.shape[0]
    half = a_ref.shape[-1] // 2
    pairs = half // LANES
    sub = LANES // pairs

    wide = 2 * LANES

    @pl.when(pl.program_id(0) == 0)
    def _():
        m_ref[(nt - 1) * LANES:, :LANES] = jnp.zeros((LANES, LANES), BF16)
        wz_ref[...] = jnp.zeros_like(wz_ref)
        y_ref[...] = jnp.zeros_like(y_ref)

    for s in range(nt):
        for k in range(2):
            lag = nt - 2 + k - s
            if lag >= 0:
                m_ref[s * LANES:(s + 1) * LANES, k * LANES:(k + 1) * LANES] = kb_ref[lag]

    lane = lax.broadcasted_iota(jnp.int32, (LANES, LANES), 1)
    for s in range(nt):
        for part in range(2):
            for a in range(pairs):
                c0 = part * half + a * LANES
                wz_ref[s * LANES + a * sub:s * LANES + (a + 1) * sub, c0:c0 + LANES] = (
                    u_ref[s, part, a * sub:(a + 1) * sub, :])
    for t in range(nt):
        for part in range(2):
            blk = v_ref[t, part]
            for a in range(pairs):
                r0 = part * half + a * LANES
                wy_ref[r0:r0 + LANES, t * LANES:(t + 1) * LANES] = (
                    jnp.where((lane // sub) == a, blk, jnp.zeros_like(blk)))

    z_ref[...] = jnp.dot(x_ref[...], wz_ref[...], preferred_element_type=F32)

    a_re = a_ref[:, :half]
    a_im = a_ref[:, half:]

    def body(c, carry):
        new = []
        for b in range(n_seq):
            s_re, s_im = carry[2 * b], carry[2 * b + 1]
            r = b * rows_per_seq + c
            sp_ref[pl.ds(r, 1), :half] = s_re
            sp_ref[pl.ds(r, 1), half:] = s_im
            z_re = z_ref[pl.ds(r, 1), :half]
            z_im = z_ref[pl.ds(r, 1), half:]
            new.append(a_re * s_re - a_im * s_im + z_re)
            new.append(a_re * s_im + a_im * s_re + z_im)
        return tuple(new)

    zero = jnp.zeros((1, half), F32)
    lax.fori_loop(0, rows_per_seq, body, (zero,) * (2 * n_seq), unroll=8)

    spb = sp_ref[...].astype(BF16)
    pitch = y_ref.shape[0] // rows
    for n in range(nt // 2):
        cols = slice(n * wide, (n + 1) * wide)
        kk = (n + 1) * wide
        acc = jnp.dot(x_ref[:, :kk], m_ref[nt * LANES - kk:, :], preferred_element_type=F32)
        acc = acc + jnp.dot(spb, wy_ref[:, cols], preferred_element_type=F32)
        acc = acc + d_ref[:, cols] * x_ref[:, cols].astype(F32)
        y = jax.nn.gelu(acc)
        for k in range(2):
            y_ref[pl.ds(2 * n + k, rows, stride=pitch), :] = y[:, k * LANES:(k + 1) * LANES]
    tokens = y_ref[...].reshape(rows, pitch, LANES)[:, :nt, :].reshape(rows * nt, LANES)
    o_ref[...] = tokens.astype(BF16)


def _s5(u2, kb, u_op, v_op, abar, dtile, *, n_seq):
    n_j, nt, _, _ = kb.shape
    rows = u2.shape[1]
    width = nt * LANES
    n_state = abar.shape[-1]
    return pl.pallas_call(
        functools.partial(_s5_kernel, rows_per_seq=rows // n_seq, n_seq=n_seq),
        grid=(n_j,),
        in_specs=[
            pl.BlockSpec((None, rows, width), lambda j: (j, 0, 0)),
            pl.BlockSpec((None, nt, LANES, LANES), lambda j: (j, 0, 0, 0)),
            pl.BlockSpec((None, nt, 2, LANES, LANES), lambda j: (j, 0, 0, 0, 0)),
            pl.BlockSpec((None, nt, 2, LANES, LANES), lambda j: (j, 0, 0, 0, 0)),
            pl.BlockSpec((None, 1, n_state), lambda j: (j, 0, 0)),
            pl.BlockSpec((None, 1, width), lambda j: (j, 0, 0)),
        ],
        out_specs=pl.BlockSpec((None, rows * nt, LANES), lambda j: (j, 0, 0)),
        out_shape=jax.ShapeDtypeStruct((n_j, rows * nt, LANES), BF16),
        scratch_shapes=[
            pltpu.VMEM((width, 2 * LANES), BF16),
            pltpu.VMEM((width, n_state), BF16),
            pltpu.VMEM((n_state, width), BF16),
            pltpu.VMEM((rows, n_state), F32),
            pltpu.VMEM((rows, n_state), F32),
            pltpu.VMEM((rows * S5_UNFOLD_PITCH, LANES), F32),
        ],
        compiler_params=pltpu.CompilerParams(
            dimension_semantics=("arbitrary",),
            vmem_limit_bytes=VMEM_LIMIT),
        name="s5",
    )(u2, kb, u_op, v_op, abar, dtile)


def _log_sigmoid(x):
    return jnp.minimum(x, 0.0) - jnp.log(1.0 + jnp.exp(-jnp.abs(x)))


def _gla_kernel(q_ref, k_ref, v_ref, gz_ref, gl_ref, gu_ref, gb_ref, nw_ref, o_ref, st_ref,
                *, chunk):
    n_heads, rows, dk = q_ref.shape
    n_chunks = rows // chunk
    d_gk = n_heads * dk

    @pl.when(pl.program_id(1) == 0)
    def _():
        st_ref[...] = jnp.zeros_like(st_ref)

    zg = jnp.dot(gl_ref[...], gu_ref[...], preferred_element_type=F32) + gb_ref[...]
    lg = _log_sigmoid(zg) * (1.0 / GLA_TAU)

    hi = lg.astype(BF16)
    lo = (lg - hi.astype(F32)).astype(BF16)
    r_id = lax.broadcasted_iota(jnp.int32, (chunk, chunk), 0)
    c_id = lax.broadcasted_iota(jnp.int32, (chunk, chunk), 1)
    causal = c_id <= r_id
    tri = jnp.where(causal, 1.0, 0.0).astype(BF16)
    tri2 = jnp.concatenate([tri, tri], axis=1)
    b_parts, last_parts = [], []
    for n in range(n_chunks):
        rs = slice(n * chunk, (n + 1) * chunk)
        b_n = jnp.dot(tri2, jnp.concatenate([hi[rs], lo[rs]], axis=0), preferred_element_type=F32)
        b_parts.append(b_n)
        last_parts.append(jnp.broadcast_to(b_n[chunk - 1:chunk], (chunk, d_gk)))
    b = jnp.concatenate(b_parts, axis=0)
    e_pos = jnp.exp(b)
    e_neg = jnp.exp(-b)
    decay = jnp.exp(jnp.concatenate(last_parts, axis=0))

    nt_dims = (((1,), (1,)), ((), ()))
    tn_dims = (((0,), (0,)), ((), ()))
    for h in range(n_heads):
        hs = slice(h * dk, (h + 1) * dk)
        q_e = (q_ref[h].astype(F32) * ((dk ** -0.5) * e_pos[:, hs])).astype(BF16)
        k_f = k_ref[h].astype(F32) * e_neg[:, hs]
        k_e = k_f.astype(BF16)
        k_t = (k_f * decay[:, hs]).astype(BF16)
        v = jnp.concatenate([v_ref[2 * h], v_ref[2 * h + 1]], axis=1)

        chunks = [slice(n * chunk, (n + 1) * chunk) for n in range(n_chunks)]
        attn = [lax.dot_general(q_e[rs], k_e[rs], nt_dims, preferred_element_type=F32)
                for rs in chunks]
        kv_t = [lax.dot_general(v[rs], k_t[rs], tn_dims, preferred_element_type=F32)
                for rs in chunks]
        attn = [jnp.where(causal, a, 0.0).astype(BF16) for a in attn]
        st = st_ref[h]
        st_in = []
        for n in range(n_chunks):
            st_in.append(st.astype(BF16))
            st = decay[n * chunk:n * chunk + 1, hs] * st + kv_t[n]
        st_ref[h] = st
        outs = [jnp.dot(attn[n], v[rs], preferred_element_type=F32)
                + lax.dot_general(q_e[rs], st_in[n], nt_dims, preferred_element_type=F32)
                for n, rs in enumerate(chunks)]

        o = jnp.concatenate(outs, axis=0)
        ms = jnp.mean(o * o, axis=-1, keepdims=True)
        o = o * lax.rsqrt(ms + EPS) * nw_ref[...]
        gz = jnp.concatenate([gz_ref[2 * h], gz_ref[2 * h + 1]], axis=1).astype(F32)
        y = (o * (gz * jax.nn.sigmoid(gz))).astype(BF16)
        o_ref[2 * h] = y[:, :LANES]
        o_ref[2 * h + 1] = y[:, LANES:]


def _gla(proj, g_low, gate_up_pad, gate_bias, norm_w, *, n_seq, seq_len, rows, slab_q, slab_k,
         slab_v, slab_gz):
    m = proj.shape[1]
    nb = seq_len // rows
    nh = GLA_HEADS
    dk = LANES
    dv = 2 * LANES
    tok = lambda b, n: b * nb + n
    return pl.pallas_call(
        functools.partial(_gla_kernel, chunk=GLA_CHUNK),
        grid=(n_seq, nb),
        in_specs=[
            pl.BlockSpec((nh, rows, LANES), lambda b, n: (slab_q // nh, tok(b, n), 0)),
            pl.BlockSpec((nh, rows, LANES), lambda b, n: (slab_k // nh, tok(b, n), 0)),
            pl.BlockSpec((2 * nh, rows, LANES), lambda b, n: (slab_v // (2 * nh), tok(b, n), 0)),
            pl.BlockSpec((2 * nh, rows, LANES), lambda b, n: (slab_gz // (2 * nh), tok(b, n), 0)),
            pl.BlockSpec((rows, LANES), lambda b, n: (tok(b, n), 0)),
            pl.BlockSpec((LANES, nh * dk), lambda b, n: (0, 0)),
            pl.BlockSpec((1, nh * dk), lambda b, n: (0, 0)),
            pl.BlockSpec((1, dv), lambda b, n: (0, 0)),
        ],
        out_specs=pl.BlockSpec((2 * nh, rows, LANES), lambda b, n: (0, tok(b, n), 0)),
        out_shape=jax.ShapeDtypeStruct((2 * nh, m, LANES), BF16),
        scratch_shapes=[pltpu.VMEM((nh, dv, dk), F32)],
        compiler_params=pltpu.CompilerParams(
            dimension_semantics=("parallel", "arbitrary"),
            vmem_limit_bytes=VMEM_LIMIT),
        name="gla",
    )(proj, proj, proj, proj, g_low, gate_up_pad, gate_bias, norm_w)


def _out_proj_kernel(g_ref, z_ref, yg_ref, x_ref, gw_ref, gb_ref, wo_ref, pw_ref, o_ref):
    n_s5 = g_ref.shape[0]
    d_s5 = n_s5 * LANES
    tm = x_ref.shape[0]
    rc = OUT_PROJ_ROW_CHUNK
    for c in range(tm // rc):
        rows = slice(c * rc, (c + 1) * rc)
        g = jnp.concatenate([g_ref[s, rows, :] for s in range(n_s5)], axis=1)
        gate = jax.nn.sigmoid(jnp.dot(g, gw_ref[...], preferred_element_type=F32) + gb_ref[...])
        z = jnp.concatenate([z_ref[s, rows, :] for s in range(n_s5)], axis=1).astype(F32)
        y_s5 = (g.astype(F32) * gate * (z * jax.nn.sigmoid(z))).astype(BF16)
        y_gla = jnp.concatenate([yg_ref[s, rows, :] for s in range(yg_ref.shape[0])], axis=1)
        mixed = (jnp.dot(y_s5, wo_ref[:d_s5, :], preferred_element_type=F32)
                 + jnp.dot(y_gla, wo_ref[d_s5:, :], preferred_element_type=F32))
        ms = jnp.mean(mixed * mixed, axis=-1, keepdims=True)
        o_ref[rows, :] = x_ref[rows, :] + mixed * lax.rsqrt(ms + EPS) * pw_ref[...]


def _out_proj(g_slabs, proj, y_gla, x2d, glu_w, glu_b, w_out, post_w, *, tm, slab_z):
    m, d = x2d.shape
    n_s5 = g_slabs.shape[0]
    n_gla = y_gla.shape[0]
    d_s5 = n_s5 * LANES
    d_mix = w_out.shape[0]
    return pl.pallas_call(
        _out_proj_kernel,
        grid=(m // tm,),
        in_specs=[
            pl.BlockSpec((n_s5, tm, LANES), lambda i: (0, i, 0)),
            pl.BlockSpec((n_s5, tm, LANES), lambda i: (slab_z // n_s5, i, 0)),
            pl.BlockSpec((n_gla, tm, LANES), lambda i: (0, i, 0)),
            pl.BlockSpec((tm, d), lambda i: (i, 0)),
            pl.BlockSpec((d_s5, d_s5), lambda i: (0, 0)),
            pl.BlockSpec((1, d_s5), lambda i: (0, 0)),
            pl.BlockSpec((d_mix, d), lambda i: (0, 0)),
            pl.BlockSpec((1, d), lambda i: (0, 0)),
        ],
        out_specs=pl.BlockSpec((tm, d), lambda i: (i, 0)),
        out_shape=jax.ShapeDtypeStruct((m, d), F32),
        compiler_params=pltpu.CompilerParams(
            dimension_semantics=("parallel",),
            vmem_limit_bytes=VMEM_LIMIT),
        name="out_proj",
    )(g_slabs, proj, y_gla, x2d, glu_w, glu_b, w_out, post_w)


def kernel(x, pre_norm_w, w_in, s5_A_re, s5_A_im, s5_B_re, s5_B_im, s5_C_re, s5_C_im, s5_D,
           s5_log_dt, s5_glu_w, s5_glu_b, gla_gate_up, gla_gate_bias, gla_norm_w, w_out,
           post_norm_w):
    bsz, seq_len, d_model = x.shape
    depth = w_in.shape[0]
    d_in = w_in.shape[2]
    d_s5 = s5_glu_w.shape[1]
    rank, d_gk = gla_gate_up.shape[1:]
    d_gv = GLA_HEADS * gla_norm_w.shape[1]
    m = bsz * seq_len
    d_main = 2 * d_s5 + 2 * d_gk + 2 * d_gv
    tn = d_s5
    assert d_s5 % LANES == 0 and d_gk == GLA_HEADS * LANES and d_gv == GLA_HEADS * 2 * LANES
    assert d_in == d_main + rank and rank <= LANES and d_main % tn == 0 and 2 * d_gk == tn
    assert seq_len % (S5_T * 8) == 0 and w_out.shape[1] == 2 * d_s5

    n_s5 = d_s5 // LANES
    slab_z = 0
    slab_q = n_s5
    slab_k = slab_q + d_gk // LANES
    slab_v = slab_k + d_gk // LANES
    slab_gz = slab_v + d_gv // LANES

    resid = x.astype(F32).reshape(m, d_model)
    for l in range(depth):
        kb, u_op, v_op, abar = _s5_ops(s5_A_re[l], s5_A_im[l], s5_B_re[l], s5_B_im[l],
                                       s5_C_re[l], s5_C_im[l], s5_log_dt[l])
        dtile = jnp.tile(s5_D[l].astype(F32).reshape(n_s5, 1, LANES), (1, 1, S5_T))

        w_bf = _cast_rows(jnp.swapaxes(w_in, 1, 2), l, tn=4 * LANES)
        u2, proj, g_low = _in_proj(resid, pre_norm_w[l].astype(F32)[None], w_bf,
                                   d_main=d_main, tm=512, tn=tn)

        g_slabs = _s5(u2, kb, u_op, v_op, abar, dtile, n_seq=bsz)

        gate_up_pad = jnp.pad(gla_gate_up[l].astype(BF16), ((0, LANES - rank), (0, 0)))
        y_gla = _gla(proj, g_low, gate_up_pad, gla_gate_bias[l].astype(F32)[None],
                     gla_norm_w[l].astype(F32)[None], n_seq=bsz, seq_len=seq_len, rows=512,
                     slab_q=slab_q, slab_k=slab_k, slab_v=slab_v, slab_gz=slab_gz)

        resid = _out_proj(g_slabs, proj, y_gla, resid, s5_glu_w[l].astype(BF16),
                          s5_glu_b[l].astype(F32)[None], w_out[l].astype(BF16),
                          post_norm_w[l].astype(F32)[None], tm=512, slab_z=slab_z)
    return resid.reshape(bsz, seq_len, d_model).astype(x.dtype)
```

```python
import functools

import jax
import jax.numpy as jnp
from jax import lax
from jax.experimental import pallas as pl
from jax.experimental.pallas import tpu as pltpu

F32 = jnp.float32
BF16 = jnp.bfloat16

S5_GROUP = 16
GLA_HEADS = 4
GLA_TAU = 16.0
GLA_CHUNK = 64
EPS = 1e-6

LANES = 128
S5_T = 16
S5_UNFOLD_PITCH = 24
OUT_PROJ_ROW_CHUNK = 256
VMEM_LIMIT = 56 * 1024 * 1024


NT_DIMS = (((1,), (1,)), ((), ()))
TN_DIMS = (((0,), (0,)), ((), ()))


def _cmul(ar, ai, br, bi):
    return ar * br - ai * bi, ar * bi + ai * br


def _s5_ops_kernel(tp_ref, np_ref, kb_ref, u_ref, v_ref, ab_ref, *, n_state):
    nt = S5_T
    ns = nt // 2
    half = LANES // 2

    def a_bar(are_raw, aim, ldt):
        are = jnp.minimum(are_raw, -1e-4)
        dt = jnp.exp(ldt)
        mag = jnp.exp(are * dt)
        return are, mag * jnp.cos(aim * dt), mag * jnp.sin(aim * dt)

    are, abr, abi = a_bar(tp_ref[0], tp_ref[1], tp_ref[2])
    aim = tp_ref[1]
    den = are * are + aim * aim
    nr = abr - 1.0
    fr = (nr * are + abi * aim) / den
    fi = (abi * are - nr * aim) / den
    xr, xi = _cmul(fr, fi, tp_ref[3], tp_ref[4])

    row = lax.broadcasted_iota(jnp.int32, (LANES, LANES), 0)
    lane = lax.broadcasted_iota(jnp.int32, (LANES, LANES), 1)
    own_half = ((row // S5_GROUP) % 2) == (lane // half)
    same_group = ((row % half) // S5_GROUP) == ((lane % half) // S5_GROUP)
    first_half = lane < half

    c_stack = jnp.concatenate([np_ref[3], np_ref[4]], axis=0)
    c_swap = pltpu.roll(c_stack, half, axis=1)
    kers, kers_swap, xs = [], [], []
    for tau in range(nt):
        lhs = jnp.where(first_half, xr, -xi)
        kers.append(jnp.dot(lhs, c_stack, preferred_element_type=F32,
                            precision=lax.Precision.HIGHEST))
        kers_swap.append(jnp.dot(lhs, c_swap, preferred_element_type=F32,
                                 precision=lax.Precision.HIGHEST))
        xs.append((jnp.where(own_half, xr, 0.0).astype(BF16),
                   jnp.where(own_half, xi, 0.0).astype(BF16)))
        if tau + 1 < nt:
            xr, xi = _cmul(xr, xi, abr, abi)

    zero = jnp.zeros((half, LANES), F32)
    first_half_rows = lax.broadcasted_iota(jnp.int32, (half, LANES), 1) < half
    for hh in range(2):
        rows = slice(hh * half, (hh + 1) * half)
        low, high = (kers, kers_swap) if hh == 0 else (kers_swap, kers)
        pick = lambda src, tau: src[tau][rows, :] if tau >= 0 else zero
        for d in range(ns):
            top = jnp.where(first_half_rows, pick(low, 2 * d), pick(high, 2 * d + 1))
            bot = jnp.where(first_half_rows, pick(low, 2 * d - 1), pick(high, 2 * d))
            blk = jnp.concatenate([top, bot], axis=0)
            kb_ref[hh, d] = jnp.where(same_group, blk, 0.0).astype(BF16)
        for s2 in range(ns):
            for part in range(2):
                u_ref[hh, s2, part] = jnp.concatenate(
                    [xs[nt - 1 - 2 * s2][part][rows, :], xs[nt - 2 - 2 * s2][part][rows, :]], axis=0)

    pr, pi = abr, abi
    for _ in range(4):
        pr, pi = _cmul(pr, pi, pr, pi)
    assert nt == 16
    pairs = LANES // (2 * S5_GROUP)
    for part, val in enumerate((pr, pi)):
        for a in range(pairs):
            r0 = 2 * a * S5_GROUP
            piece = jnp.where(first_half[:1], val[r0:r0 + 1], val[r0 + S5_GROUP:r0 + S5_GROUP + 1])
            c0 = part * (n_state // 2) + a * LANES
            ab_ref[:, c0:c0 + LANES] = piece

    _, nbr, nbi = a_bar(np_ref[0], np_ref[1], np_ref[2])
    cr, ci = np_ref[3], np_ref[4]
    lane_n = lax.broadcasted_iota(jnp.int32, (half, LANES), 1)
    parity = (lane_n // S5_GROUP) % 2
    first_n = lane_n < half
    pr, pi = nbr, nbi
    outs = []
    for t in range(nt):
        wr, wi = _cmul(cr, ci, pr, pi)
        outs.append((wr, -wi))
        if t + 1 < nt:
            pr, pi = _cmul(pr, pi, nbr, nbi)
    for t2 in range(ns):
        for part in range(2):
            even, odd = outs[2 * t2][part], outs[2 * t2 + 1][part]
            packed = (jnp.where(first_n, even, pltpu.roll(odd, half, axis=1)),
                      jnp.where(first_n, pltpu.roll(even, half, axis=1), odd))
            for hh in range(2):
                for q in range(2):
                    v_ref[hh, t2, part, q * half:(q + 1) * half, :] = (
                        jnp.where(parity == q, packed[hh], 0.0).astype(BF16))


def _s5_ops(a_re, a_im, b_re, b_im, c_re, c_im, log_dt):
    g, p = a_re.shape
    h = S5_GROUP
    gps = LANES // h
    n_j = g // gps
    assert 2 * p == LANES and g % gps == 0
    n_state = 2 * gps * p
    f = lambda t: t.astype(F32)

    def t_arr(t):
        t = f(t).reshape(n_j, gps, -1, 1, p)
        return jnp.broadcast_to(t, (n_j, gps, h, 2, p)).reshape(n_j, LANES, LANES)

    def n_arr(t):
        t = jnp.swapaxes(f(t).reshape(n_j, gps, -1, p), 1, 3)
        t = jnp.swapaxes(jnp.broadcast_to(t, (n_j, p, h, gps)), 2, 3)
        return t.reshape(n_j, p, LANES)

    ldt = jnp.broadcast_to(f(log_dt)[:, None], (g, p))
    tpar = jnp.stack([t_arr(a_re), t_arr(a_im), t_arr(ldt),
                      t_arr(jnp.swapaxes(b_re, 1, 2)), t_arr(jnp.swapaxes(b_im, 1, 2))], axis=1)
    npar = jnp.stack([n_arr(a_re), n_arr(a_im), n_arr(ldt), n_arr(c_re), n_arr(c_im)], axis=1)
    ns = S5_T // 2
    return pl.pallas_call(
        functools.partial(_s5_ops_kernel, n_state=n_state),
        grid=(n_j,),
        in_specs=[
            pl.BlockSpec((None, 5, LANES, LANES), lambda j: (j, 0, 0, 0)),
            pl.BlockSpec((None, 5, p, LANES), lambda j: (j, 0, 0, 0)),
        ],
        out_specs=[
            pl.BlockSpec((None, 2, ns, LANES, LANES), lambda j: (j, 0, 0, 0, 0)),
            pl.BlockSpec((None, 2, ns, 2, LANES, LANES), lambda j: (j, 0, 0, 0, 0, 0)),
            pl.BlockSpec((None, 2, ns, 2, LANES, LANES), lambda j: (j, 0, 0, 0, 0, 0)),
            pl.BlockSpec((None, 1, n_state), lambda j: (j, 0, 0)),
        ],
        out_shape=[
            jax.ShapeDtypeStruct((n_j, 2, ns, LANES, LANES), BF16),
            jax.ShapeDtypeStruct((n_j, 2, ns, 2, LANES, LANES), BF16),
            jax.ShapeDtypeStruct((n_j, 2, ns, 2, LANES, LANES), BF16),
            jax.ShapeDtypeStruct((n_j, 1, n_state), F32),
        ],
        compiler_params=pltpu.CompilerParams(dimension_semantics=("parallel",)),
        name="s5_ops",
    )(tpar, npar)


def _in_proj_kernel(x_ref, nw_ref, w_ref, u_ref, p_ref, l_ref, h_ref, s_ref, *, d_main, tn):
    tm = x_ref.shape[0]
    n_slabs = tn // LANES
    fold = tm // S5_T
    xf = x_ref[...]
    h_ref[...] = (xf * nw_ref[...]).astype(BF16)
    rs = lax.rsqrt(jnp.mean(xf * xf, axis=-1, keepdims=True) + EPS)

    for n in range(d_main // tn):
        res = lax.dot_general(h_ref[...], w_ref[n * tn:(n + 1) * tn, :], NT_DIMS,
                              preferred_element_type=F32) * rs
        if n == 0:
            for s in range(n_slabs):
                s_ref[s] = res[:, s * LANES:(s + 1) * LANES]
            low = lax.broadcasted_iota(jnp.int32, (fold, LANES), 1) < LANES // 2
            for s in range(n_slabs):
                for t2 in range(S5_T // 2):
                    even = s_ref[s, pl.ds(2 * t2, fold, stride=S5_T), :]
                    odd = s_ref[s, pl.ds(2 * t2 + 1, fold, stride=S5_T), :]
                    cols = slice(t2 * LANES, (t2 + 1) * LANES)
                    u_ref[2 * s, :, cols] = (
                        jnp.where(low, even, pltpu.roll(odd, LANES // 2, axis=1)).astype(BF16))
                    u_ref[2 * s + 1, :, cols] = (
                        jnp.where(low, pltpu.roll(even, LANES // 2, axis=1), odd).astype(BF16))
        else:
            for s in range(n_slabs):
                p_ref[(n - 1) * n_slabs + s] = res[:, s * LANES:(s + 1) * LANES].astype(BF16)
    low = lax.dot_general(h_ref[...], w_ref[d_main:d_main + LANES, :], NT_DIMS,
                          preferred_element_type=F32) * rs
    l_ref[...] = low.astype(BF16)


def _cast_rows_kernel(w_ref, o_ref, *, n_valid):
    tn = o_ref.shape[0]
    row = pl.program_id(0) * tn + lax.broadcasted_iota(jnp.int32, (tn, 1), 0)
    o_ref[...] = jnp.where(row < n_valid, w_ref[...], 0.0).astype(BF16)


def _cast_rows(w3d, layer, *, tn):
    _, n, d = w3d.shape
    n_blocks = pl.cdiv(n, tn)
    return pl.pallas_call(
        functools.partial(_cast_rows_kernel, n_valid=n),
        grid=(n_blocks,),
        in_specs=[pl.BlockSpec((None, tn, d), lambda c: (layer, c, 0))],
        out_specs=pl.BlockSpec((tn, d), lambda c: (c, 0)),
        out_shape=jax.ShapeDtypeStruct((n_blocks * tn, d), BF16),
        compiler_params=pltpu.CompilerParams(dimension_semantics=("parallel",)),
        name="cast_rows",
    )(w3d)


def _in_proj(x2d, norm_w, w_bf, *, d_main, tm, tn):
    m, d = x2d.shape
    n_slabs = tn // LANES
    n_proj = (d_main // tn - 1) * n_slabs
    return pl.pallas_call(
        functools.partial(_in_proj_kernel, d_main=d_main, tn=tn),
        grid=(m // tm,),
        in_specs=[
            pl.BlockSpec((tm, d), lambda i: (i, 0)),
            pl.BlockSpec((1, d), lambda i: (0, 0)),
            pl.BlockSpec(w_bf.shape, lambda i: (0, 0), pipeline_mode=pl.Buffered(1)),
        ],
        out_specs=[
            pl.BlockSpec((2 * n_slabs, tm // S5_T, S5_T * LANES // 2), lambda i: (0, i, 0)),
            pl.BlockSpec((n_proj, tm, LANES), lambda i: (0, i, 0)),
            pl.BlockSpec((tm, LANES), lambda i: (i, 0)),
        ],
        out_shape=[
            jax.ShapeDtypeStruct((2 * n_slabs, m // S5_T, S5_T * LANES // 2), BF16),
            jax.ShapeDtypeStruct((n_proj, m, LANES), BF16),
            jax.ShapeDtypeStruct((m, LANES), BF16),
        ],
        scratch_shapes=[pltpu.VMEM((tm, d), BF16), pltpu.VMEM((n_slabs, tm, LANES), F32)],
        compiler_params=pltpu.CompilerParams(
            dimension_semantics=("parallel",),
            vmem_limit_bytes=VMEM_LIMIT),
        name="in_proj",
    )(x2d, norm_w, w_bf)


def _s5_kernel(x_ref, kb_ref, u_ref, v_ref, a_ref, d_ref, o_ref,
               m_ref, wz_ref, wy_ref, z_ref, sp_ref, y_ref, *, rows_per_seq, n_seq):
    ns = S5_T // 2
    rows = x_ref.shape[1]
    half = a_ref.shape[-1] // 2
    hstate = half // 2
    pairs = hstate // LANES
    sub = LANES // (2 * pairs)
    wide = 2 * LANES

    @pl.when(pl.program_id(0) == 0)
    def _():
        m_ref[:, (ns - 1) * LANES:, :LANES] = jnp.zeros((2, LANES, LANES), BF16)
        wz_ref[...] = jnp.zeros_like(wz_ref)
        y_ref[...] = jnp.zeros_like(y_ref)

    lane = lax.broadcasted_iota(jnp.int32, (LANES, LANES), 1)
    for hh in range(2):
        for s in range(ns):
            for k in range(2):
                lag = ns - 2 + k - s
                if lag >= 0:
                    m_ref[hh, s * LANES:(s + 1) * LANES, k * LANES:(k + 1) * LANES] = kb_ref[hh, lag]
        for s in range(ns):
            for part in range(2):
                for rho in range(2):
                    for a in range(pairs):
                        r0 = rho * (LANES // 2) + a * sub
                        c0 = part * hstate + a * LANES
                        wz_ref[hh, s * LANES + r0:s * LANES + r0 + sub, c0:c0 + LANES] = (
                            u_ref[hh, s, part, r0:r0 + sub, :])
        for t in range(ns):
            for part in range(2):
                blk = v_ref[hh, t, part]
                for a in range(pairs):
                    r0 = part * hstate + a * LANES
                    wy_ref[hh, r0:r0 + LANES, t * LANES:(t + 1) * LANES] = (
                        jnp.where(((lane % (LANES // 2)) // sub) == a, blk, jnp.zeros_like(blk)))
        zh = jnp.dot(x_ref[hh], wz_ref[hh], preferred_element_type=F32)
        z_ref[:, hh * hstate:(hh + 1) * hstate] = zh[:, :hstate]
        z_ref[:, half + hh * hstate:half + (hh + 1) * hstate] = zh[:, hstate:]

    a_re = a_ref[:, :half]
    a_im = a_ref[:, half:]

    def body(c, carry):
        new = []
        for b in range(n_seq):
            s_re, s_im = carry[2 * b], carry[2 * b + 1]
            r = b * rows_per_seq + c
            sp_ref[pl.ds(r, 1), :half] = s_re
            sp_ref[pl.ds(r, 1), half:] = s_im
            z_re = z_ref[pl.ds(r, 1), :half]
            z_im = z_ref[pl.ds(r, 1), half:]
            new.append(a_re * s_re - a_im * s_im + z_re)
            new.append(a_re * s_im + a_im * s_re + z_im)
        return tuple(new)

    zero = jnp.zeros((1, half), F32)
    lax.fori_loop(0, rows_per_seq, body, (zero,) * (2 * n_seq), unroll=8)

    spb = [jnp.concatenate([sp_ref[:, hh * hstate:(hh + 1) * hstate],
                            sp_ref[:, half + hh * hstate:half + (hh + 1) * hstate]],
                           axis=1).astype(BF16) for hh in range(2)]
    pitch = y_ref.shape[0] // rows
    low = lax.broadcasted_iota(jnp.int32, (rows, LANES), 1) < LANES // 2
    for n in range(ns // 2):
        cols = slice(n * wide, (n + 1) * wide)
        kk = (n + 1) * wide
        ys = []
        for hh in range(2):
            acc = jnp.dot(x_ref[hh, :, :kk], m_ref[hh, ns * LANES - kk:, :],
                          preferred_element_type=F32)
            acc = acc + jnp.dot(spb[hh], wy_ref[hh, :, cols], preferred_element_type=F32)
            acc = acc + d_ref[hh, :, cols] * x_ref[hh, :, cols].astype(F32)
            ys.append(jax.nn.gelu(acc))
        for k in range(2):
            y0 = ys[0][:, k * LANES:(k + 1) * LANES]
            y1 = ys[1][:, k * LANES:(k + 1) * LANES]
            t = 2 * (2 * n + k)
            y_ref[pl.ds(t, rows, stride=pitch), :] = (
                jnp.where(low, y0, pltpu.roll(y1, LANES // 2, axis=1)))
            y_ref[pl.ds(t + 1, rows, stride=pitch), :] = (
                jnp.where(low, pltpu.roll(y0, LANES // 2, axis=1), y1))
    tokens = y_ref[...].reshape(rows, pitch, LANES)[:, :S5_T, :].reshape(rows * S5_T, LANES)
    o_ref[...] = tokens.astype(BF16)


def _s5(u2, kb, u_op, v_op, abar, dtile, *, n_seq):
    n_j, _, ns, _, _ = kb.shape
    rows = u2.shape[1]
    width = ns * LANES
    n_state = abar.shape[-1]
    return pl.pallas_call(
        functools.partial(_s5_kernel, rows_per_seq=rows // n_seq, n_seq=n_seq),
        grid=(n_j,),
        in_specs=[
            pl.BlockSpec((2, rows, width), lambda j: (j, 0, 0)),
            pl.BlockSpec((None, 2, ns, LANES, LANES), lambda j: (j, 0, 0, 0, 0)),
            pl.BlockSpec((None, 2, ns, 2, LANES, LANES), lambda j: (j, 0, 0, 0, 0, 0)),
            pl.BlockSpec((None, 2, ns, 2, LANES, LANES), lambda j: (j, 0, 0, 0, 0, 0)),
            pl.BlockSpec((None, 1, n_state), lambda j: (j, 0, 0)),
            pl.BlockSpec((None, 2, 1, width), lambda j: (j, 0, 0, 0)),
        ],
        out_specs=pl.BlockSpec((None, rows * S5_T, LANES), lambda j: (j, 0, 0)),
        out_shape=jax.ShapeDtypeStruct((n_j, rows * S5_T, LANES), BF16),
        scratch_shapes=[
            pltpu.VMEM((2, width, 2 * LANES), BF16),
            pltpu.VMEM((2, width, n_state // 2), BF16),
            pltpu.VMEM((2, n_state // 2, width), BF16),
            pltpu.VMEM((rows, n_state), F32),
            pltpu.VMEM((rows, n_state), F32),
            pltpu.VMEM((rows * S5_UNFOLD_PITCH, LANES), F32),
        ],
        compiler_params=pltpu.CompilerParams(
            dimension_semantics=("arbitrary",),
            vmem_limit_bytes=VMEM_LIMIT),
        name="s5",
    )(u2, kb, u_op, v_op, abar, dtile)


def _log_sigmoid(x):
    return jnp.minimum(x, 0.0) - jnp.log(1.0 + jnp.exp(-jnp.abs(x)))


def _gla_kernel(q_ref, k_ref, v_ref, gz_ref, gl_ref, gu_ref, gb_ref, nw_ref, o_ref, st_ref,
                *, chunk):
    n_heads, rows, dk = q_ref.shape
    n_chunks = rows // chunk
    d_gk = n_heads * dk

    @pl.when(pl.program_id(1) == 0)
    def _():
        st_ref[...] = jnp.zeros_like(st_ref)

    zg = jnp.dot(gl_ref[...], gu_ref[...], preferred_element_type=F32) + gb_ref[...]
    lg = _log_sigmoid(zg) * (1.0 / GLA_TAU)

    hi = lg.astype(BF16)
    lo = (lg - hi.astype(F32)).astype(BF16)
    r_id = lax.broadcasted_iota(jnp.int32, (chunk, chunk), 0)
    c_id = lax.broadcasted_iota(jnp.int32, (chunk, chunk), 1)
    causal = c_id <= r_id
    tri = jnp.where(causal, 1.0, 0.0).astype(BF16)
    tri2 = jnp.concatenate([tri, tri], axis=1)
    b_parts, last_parts = [], []
    for n in range(n_chunks):
        rs = slice(n * chunk, (n + 1) * chunk)
        b_n = jnp.dot(tri2, jnp.concatenate([hi[rs], lo[rs]], axis=0), preferred_element_type=F32)
        b_parts.append(b_n)
        last_parts.append(jnp.broadcast_to(b_n[chunk - 1:chunk], (chunk, d_gk)))
    b = jnp.concatenate(b_parts, axis=0)
    e_pos = jnp.exp(b)
    e_neg = jnp.exp(-b)
    decay = jnp.exp(jnp.concatenate(last_parts, axis=0))

    for h in range(n_heads):
        hs = slice(h * dk, (h + 1) * dk)
        q_e = (q_ref[h].astype(F32) * ((dk ** -0.5) * e_pos[:, hs])).astype(BF16)
        k_f = k_ref[h].astype(F32) * e_neg[:, hs]
        k_e = k_f.astype(BF16)
        k_t = (k_f * decay[:, hs]).astype(BF16)
        v = jnp.concatenate([v_ref[2 * h], v_ref[2 * h + 1]], axis=1)

        chunks = [slice(n * chunk, (n + 1) * chunk) for n in range(n_chunks)]
        attn = [lax.dot_general(q_e[rs], k_e[rs], NT_DIMS, preferred_element_type=F32)
                for rs in chunks]
        kv_t = [lax.dot_general(v[rs], k_t[rs], TN_DIMS, preferred_element_type=F32)
                for rs in chunks]
        attn = [jnp.where(causal, a, 0.0).astype(BF16) for a in attn]
        st = st_ref[h]
        st_in = []
        for n in range(n_chunks):
            st_in.append(st.astype(BF16))
            st = decay[n * chunk:n * chunk + 1, hs] * st + kv_t[n]
        st_ref[h] = st
        outs = [jnp.dot(attn[n], v[rs], preferred_element_type=F32)
                + lax.dot_general(q_e[rs], st_in[n], NT_DIMS, preferred_element_type=F32)
                for n, rs in enumerate(chunks)]

        o = jnp.concatenate(outs, axis=0)
        ms = jnp.mean(o * o, axis=-1, keepdims=True)
        o = o * lax.rsqrt(ms + EPS) * nw_ref[...]
        gz = jnp.concatenate([gz_ref[2 * h], gz_ref[2 * h + 1]], axis=1).astype(F32)
        y = (o * (gz * jax.nn.sigmoid(gz))).astype(BF16)
        o_ref[2 * h] = y[:, :LANES]
        o_ref[2 * h + 1] = y[:, LANES:]


def _gla(proj, g_low, gate_up_pad, gate_bias, norm_w, *, n_seq, seq_len, rows, slab_q, slab_k,
         slab_v, slab_gz):
    m = proj.shape[1]
    nb = seq_len // rows
    nh = GLA_HEADS
    dk = LANES
    dv = 2 * LANES
    tok = lambda b, n: b * nb + n
    return pl.pallas_call(
        functools.partial(_gla_kernel, chunk=GLA_CHUNK),
        grid=(n_seq, nb),
        in_specs=[
            pl.BlockSpec((nh, rows, LANES), lambda b, n: (slab_q // nh, tok(b, n), 0)),
            pl.BlockSpec((nh, rows, LANES), lambda b, n: (slab_k // nh, tok(b, n), 0)),
            pl.BlockSpec((2 * nh, rows, LANES), lambda b, n: (slab_v // (2 * nh), tok(b, n), 0)),
            pl.BlockSpec((2 * nh, rows, LANES), lambda b, n: (slab_gz // (2 * nh), tok(b, n), 0)),
            pl.BlockSpec((rows, LANES), lambda b, n: (tok(b, n), 0)),
            pl.BlockSpec((LANES, nh * dk), lambda b, n: (0, 0)),
            pl.BlockSpec((1, nh * dk), lambda b, n: (0, 0)),
            pl.BlockSpec((1, dv), lambda b, n: (0, 0)),
        ],
        out_specs=pl.BlockSpec((2 * nh, rows, LANES), lambda b, n: (0, tok(b, n), 0)),
        out_shape=jax.ShapeDtypeStruct((2 * nh, m, LANES), BF16),
        scratch_shapes=[pltpu.VMEM((nh, dv, dk), F32)],
        compiler_params=pltpu.CompilerParams(
            dimension_semantics=("parallel", "arbitrary"),
            vmem_limit_bytes=VMEM_LIMIT),
        name="gla",
    )(proj, proj, proj, proj, g_low, gate_up_pad, gate_bias, norm_w)


def _out_proj_kernel(g_ref, z_ref, yg_ref, x_ref, gw_ref, gb_ref, wo_ref, pw_ref, o_ref):
    n_s5 = g_ref.shape[0]
    d_s5 = n_s5 * LANES
    tm = x_ref.shape[0]
    rc = OUT_PROJ_ROW_CHUNK
    for c in range(tm // rc):
        rows = slice(c * rc, (c + 1) * rc)
        g = jnp.concatenate([g_ref[s, rows, :] for s in range(n_s5)], axis=1)
        gate = jax.nn.sigmoid(jnp.dot(g, gw_ref[...], preferred_element_type=F32) + gb_ref[...])
        z = jnp.concatenate([z_ref[s, rows, :] for s in range(n_s5)], axis=1).astype(F32)
        y_s5 = (g.astype(F32) * gate * (z * jax.nn.sigmoid(z))).astype(BF16)
        y_gla = jnp.concatenate([yg_ref[s, rows, :] for s in range(yg_ref.shape[0])], axis=1)
        mixed = (jnp.dot(y_s5, wo_ref[:d_s5, :], preferred_element_type=F32)
                 + jnp.dot(y_gla, wo_ref[d_s5:, :], preferred_element_type=F32))
        ms = jnp.mean(mixed * mixed, axis=-1, keepdims=True)
        o_ref[rows, :] = x_ref[rows, :] + mixed * lax.rsqrt(ms + EPS) * pw_ref[...]


def _out_proj(g_slabs, proj, y_gla, x2d, glu_w, glu_b, w_out, post_w, *, tm, slab_z):
    m, d = x2d.shape
    n_s5 = g_slabs.shape[0]
    n_gla = y_gla.shape[0]
    d_s5 = n_s5 * LANES
    d_mix = w_out.shape[0]
    return pl.pallas_call(
        _out_proj_kernel,
        grid=(m // tm,),
        in_specs=[
            pl.BlockSpec((n_s5, tm, LANES), lambda i: (0, i, 0)),
            pl.BlockSpec((n_s5, tm, LANES), lambda i: (slab_z // n_s5, i, 0)),
            pl.BlockSpec((n_gla, tm, LANES), lambda i: (0, i, 0)),
            pl.BlockSpec((tm, d), lambda i: (i, 0)),
            pl.BlockSpec((d_s5, d_s5), lambda i: (0, 0)),
            pl.BlockSpec((1, d_s5), lambda i: (0, 0)),
            pl.BlockSpec((d_mix, d), lambda i: (0, 0)),
            pl.BlockSpec((1, d), lambda i: (0, 0)),
        ],
        out_specs=pl.BlockSpec((tm, d), lambda i: (i, 0)),
        out_shape=jax.ShapeDtypeStruct((m, d), F32),
        compiler_params=pltpu.CompilerParams(
            dimension_semantics=("parallel",),
            vmem_limit_bytes=VMEM_LIMIT),
        name="out_proj",
    )(g_slabs, proj, y_gla, x2d, glu_w, glu_b, w_out, post_w)


def kernel(x, pre_norm_w, w_in, s5_A_re, s5_A_im, s5_B_re, s5_B_im, s5_C_re, s5_C_im, s5_D,
           s5_log_dt, s5_glu_w, s5_glu_b, gla_gate_up, gla_gate_bias, gla_norm_w, w_out,
           post_norm_w):
    bsz, seq_len, d_model = x.shape
    depth = w_in.shape[0]
    d_in = w_in.shape[2]
    d_s5 = s5_glu_w.shape[1]
    rank, d_gk = gla_gate_up.shape[1:]
    d_gv = GLA_HEADS * gla_norm_w.shape[1]
    m = bsz * seq_len
    d_main = 2 * d_s5 + 2 * d_gk + 2 * d_gv
    tn = d_s5
    assert d_s5 % LANES == 0 and d_gk == GLA_HEADS * LANES and d_gv == GLA_HEADS * 2 * LANES
    assert d_in == d_main + rank and rank <= LANES and d_main % tn == 0 and 2 * d_gk == tn
    assert seq_len % (S5_T * 8) == 0 and w_out.shape[1] == 2 * d_s5

    n_s5 = d_s5 // LANES
    slab_z = 0
    slab_q = n_s5
    slab_k = slab_q + d_gk // LANES
    slab_v = slab_k + d_gk // LANES
    slab_gz = slab_v + d_gv // LANES

    resid = x.astype(F32).reshape(m, d_model)
    for l in range(depth):
        kb, u_op, v_op, abar = _s5_ops(s5_A_re[l], s5_A_im[l], s5_B_re[l], s5_B_im[l],
                                       s5_C_re[l], s5_C_im[l], s5_log_dt[l])
        dtile = jnp.tile(s5_D[l].astype(F32).reshape(n_s5, 2, 1, LANES // 2), (1, 1, 1, S5_T))

        w_bf = _cast_rows(jnp.swapaxes(w_in, 1, 2), l, tn=4 * LANES)
        u2, proj, g_low = _in_proj(resid, pre_norm_w[l].astype(F32)[None], w_bf,
                                   d_main=d_main, tm=512, tn=tn)

        g_slabs = _s5(u2, kb, u_op, v_op, abar, dtile, n_seq=bsz)

        gate_up_pad = jnp.pad(gla_gate_up[l].astype(BF16), ((0, LANES - rank), (0, 0)))
        y_gla = _gla(proj, g_low, gate_up_pad, gla_gate_bias[l].astype(F32)[None],
                     gla_norm_w[l].astype(F32)[None], n_seq=bsz, seq_len=seq_len, rows=512,
                     slab_q=slab_q, slab_k=slab_k, slab_v=slab_v, slab_gz=slab_gz)

        resid = _out_proj(g_slabs, proj, y_gla, resid, s5_glu_w[l].astype(BF16),
                          s5_glu_b[l].astype(F32)[None], w_out[l].astype(BF16),
                          post_norm_w[l].astype(F32)[None], tm=512, slab_z=slab_z)
    return resid.reshape(bsz, seq_len, d_model).astype(x.dtype)
```

```python
import functools

import jax
import jax.numpy as jnp
from jax import lax
from jax.experimental import pallas as pl
from jax.experimental.pallas import tpu as pltpu

F32 = jnp.float32
BF16 = jnp.bfloat16

S5_GROUP = 16
GLA_HEADS = 4
GLA_TAU = 16.0
GLA_CHUNK = 64
EPS = 1e-6

LANES = 128
S5_T = 16
S5_UNFOLD_PITCH = 24
OUT_PROJ_ROW_CHUNK = 256
IN_PROJ_W_CHUNK = 256
VMEM_LIMIT = 56 * 1024 * 1024


NT_DIMS = (((1,), (1,)), ((), ()))
TN_DIMS = (((0,), (0,)), ((), ()))


def _cmul(ar, ai, br, bi):
    return ar * br - ai * bi, ar * bi + ai * br


def _s5_ops_kernel(tp_ref, np_ref, kb_ref, u_ref, v_ref, ab_ref, *, n_state):
    nt = S5_T
    ns = nt // 2
    half = LANES // 2

    def a_bar(are_raw, aim, ldt):
        are = jnp.minimum(are_raw, -1e-4)
        dt = jnp.exp(ldt)
        mag = jnp.exp(are * dt)
        return are, mag * jnp.cos(aim * dt), mag * jnp.sin(aim * dt)

    are, abr, abi = a_bar(tp_ref[0], tp_ref[1], tp_ref[2])
    aim = tp_ref[1]
    den = are * are + aim * aim
    nr = abr - 1.0
    fr = (nr * are + abi * aim) / den
    fi = (abi * are - nr * aim) / den
    xr, xi = _cmul(fr, fi, tp_ref[3], tp_ref[4])

    row = lax.broadcasted_iota(jnp.int32, (LANES, LANES), 0)
    lane = lax.broadcasted_iota(jnp.int32, (LANES, LANES), 1)
    own_half = ((row // S5_GROUP) % 2) == (lane // half)
    same_group = ((row % half) // S5_GROUP) == ((lane % half) // S5_GROUP)
    first_half = lane < half

    def split(t):
        hi = t.astype(BF16)
        return hi, (t - hi.astype(F32)).astype(BF16)

    c_stack = jnp.concatenate([np_ref[3], np_ref[4]], axis=0)
    c_swap = pltpu.roll(c_stack, half, axis=1)
    c_hi, c_lo = split(jnp.concatenate([c_stack, c_swap], axis=1))
    c_terms = jnp.concatenate([c_hi, c_lo, c_hi], axis=0)
    kers, kers_swap, xs = [], [], []
    for tau in range(nt):
        l_hi, l_lo = split(jnp.where(first_half, xr, -xi))
        ker = jnp.dot(jnp.concatenate([l_hi, l_hi, l_lo], axis=1), c_terms,
                      preferred_element_type=F32)
        kers.append(ker[:, :LANES])
        kers_swap.append(ker[:, LANES:])
        xs.append((jnp.where(own_half, xr, 0.0).astype(BF16),
                   jnp.where(own_half, xi, 0.0).astype(BF16)))
        if tau + 1 < nt:
            xr, xi = _cmul(xr, xi, abr, abi)

    zero = jnp.zeros((half, LANES), F32)
    first_half_rows = lax.broadcasted_iota(jnp.int32, (half, LANES), 1) < half
    for hh in range(2):
        rows = slice(hh * half, (hh + 1) * half)
        low, high = (kers, kers_swap) if hh == 0 else (kers_swap, kers)
        pick = lambda src, tau: src[tau][rows, :] if tau >= 0 else zero
        for d in range(ns):
            top = jnp.where(first_half_rows, pick(low, 2 * d), pick(high, 2 * d + 1))
            bot = jnp.where(first_half_rows, pick(low, 2 * d - 1), pick(high, 2 * d))
            blk = jnp.concatenate([top, bot], axis=0)
            kb_ref[hh, d] = jnp.where(same_group, blk, 0.0).astype(BF16)
        for s2 in range(ns):
            for part in range(2):
                u_ref[hh, s2, part] = jnp.concatenate(
                    [xs[nt - 1 - 2 * s2][part][rows, :], xs[nt - 2 - 2 * s2][part][rows, :]], axis=0)

    pr, pi = abr, abi
    for _ in range(4):
        pr, pi = _cmul(pr, pi, pr, pi)
    assert nt == 16
    pairs = LANES // (2 * S5_GROUP)
    for part, val in enumerate((pr, pi)):
        for a in range(pairs):
            r0 = 2 * a * S5_GROUP
            piece = jnp.where(first_half[:1], val[r0:r0 + 1], val[r0 + S5_GROUP:r0 + S5_GROUP + 1])
            c0 = part * (n_state // 2) + a * LANES
            ab_ref[:, c0:c0 + LANES] = piece

    _, nbr, nbi = a_bar(np_ref[0], np_ref[1], np_ref[2])
    cr, ci = np_ref[3], np_ref[4]
    lane_n = lax.broadcasted_iota(jnp.int32, (half, LANES), 1)
    parity = (lane_n // S5_GROUP) % 2
    first_n = lane_n < half
    pr, pi = nbr, nbi
    outs = []
    for t in range(nt):
        wr, wi = _cmul(cr, ci, pr, pi)
        outs.append((wr, -wi))
        if t + 1 < nt:
            pr, pi = _cmul(pr, pi, nbr, nbi)
    for t2 in range(ns):
        for part in range(2):
            even, odd = outs[2 * t2][part], outs[2 * t2 + 1][part]
            packed = (jnp.where(first_n, even, pltpu.roll(odd, half, axis=1)),
                      jnp.where(first_n, pltpu.roll(even, half, axis=1), odd))
            for hh in range(2):
                for q in range(2):
                    v_ref[hh, t2, part, q * half:(q + 1) * half, :] = (
                        jnp.where(parity == q, packed[hh], 0.0).astype(BF16))


def _s5_ops(a_re, a_im, b_re, b_im, c_re, c_im, log_dt):
    g, p = a_re.shape
    h = S5_GROUP
    gps = LANES // h
    n_j = g // gps
    assert 2 * p == LANES and g % gps == 0
    n_state = 2 * gps * p
    f = lambda t: t.astype(F32)

    def t_arr(t):
        t = f(t).reshape(n_j, gps, -1, 1, p)
        return jnp.broadcast_to(t, (n_j, gps, h, 2, p)).reshape(n_j, LANES, LANES)

    def n_arr(t):
        t = jnp.swapaxes(f(t).reshape(n_j, gps, -1, p), 1, 3)
        t = jnp.swapaxes(jnp.broadcast_to(t, (n_j, p, h, gps)), 2, 3)
        return t.reshape(n_j, p, LANES)

    ldt = jnp.broadcast_to(f(log_dt)[:, None], (g, p))
    tpar = jnp.stack([t_arr(a_re), t_arr(a_im), t_arr(ldt),
                      t_arr(jnp.swapaxes(b_re, 1, 2)), t_arr(jnp.swapaxes(b_im, 1, 2))], axis=1)
    npar = jnp.stack([n_arr(a_re), n_arr(a_im), n_arr(ldt), n_arr(c_re), n_arr(c_im)], axis=1)
    ns = S5_T // 2
    return pl.pallas_call(
        functools.partial(_s5_ops_kernel, n_state=n_state),
        grid=(n_j,),
        in_specs=[
            pl.BlockSpec((None, 5, LANES, LANES), lambda j: (j, 0, 0, 0)),
            pl.BlockSpec((None, 5, p, LANES), lambda j: (j, 0, 0, 0)),
        ],
        out_specs=[
            pl.BlockSpec((None, 2, ns, LANES, LANES), lambda j: (j, 0, 0, 0, 0)),
            pl.BlockSpec((None, 2, ns, 2, LANES, LANES), lambda j: (j, 0, 0, 0, 0, 0)),
            pl.BlockSpec((None, 2, ns, 2, LANES, LANES), lambda j: (j, 0, 0, 0, 0, 0)),
            pl.BlockSpec((None, 1, n_state), lambda j: (j, 0, 0)),
        ],
        out_shape=[
            jax.ShapeDtypeStruct((n_j, 2, ns, LANES, LANES), BF16),
            jax.ShapeDtypeStruct((n_j, 2, ns, 2, LANES, LANES), BF16),
            jax.ShapeDtypeStruct((n_j, 2, ns, 2, LANES, LANES), BF16),
            jax.ShapeDtypeStruct((n_j, 1, n_state), F32),
        ],
        compiler_params=pltpu.CompilerParams(dimension_semantics=("parallel",)),
        name="s5_ops",
    )(tpar, npar)


def _in_proj_kernel(x_ref, nw_ref, w_hbm, u_ref, p_ref, l_ref, h_ref, s_ref, w_ref, stage_ref, sem,
                    *, layer, d_main, tn):
    tm = x_ref.shape[0]
    n_slabs = tn // LANES
    fold = tm // S5_T

    @pl.when(pl.program_id(0) == 0)
    def _():
        n_rows = w_hbm.shape[1]
        chunk = stage_ref.shape[1]
        n_full = n_rows // chunk
        tail = n_rows - n_full * chunk

        def copy(c, rows):
            return pltpu.make_async_copy(
                w_hbm.at[layer, pl.ds(c * chunk, rows), :],
                stage_ref.at[c % 2, pl.ds(0, rows), :], sem.at[c % 2])

        sizes = [chunk] * n_full + ([tail] if tail else [])
        for c in range(min(2, len(sizes))):
            copy(c, sizes[c]).start()
        w_ref[n_full * chunk:, :] = jnp.zeros((w_ref.shape[0] - n_full * chunk, w_ref.shape[1]), BF16)
        for c, rows in enumerate(sizes):
            copy(c, rows).wait()
            w_ref[c * chunk:c * chunk + rows, :] = stage_ref[c % 2, :rows, :].astype(BF16)
            if c + 2 < len(sizes):
                copy(c + 2, sizes[c + 2]).start()

    xf = x_ref[...]
    h_ref[...] = (xf * nw_ref[...]).astype(BF16)
    rs = lax.rsqrt(jnp.mean(xf * xf, axis=-1, keepdims=True) + EPS)

    for n in range(d_main // tn):
        res = lax.dot_general(h_ref[...], w_ref[n * tn:(n + 1) * tn, :], NT_DIMS,
                              preferred_element_type=F32) * rs
        if n == 0:
            for s in range(n_slabs):
                s_ref[s] = res[:, s * LANES:(s + 1) * LANES]
            low = lax.broadcasted_iota(jnp.int32, (fold, LANES), 1) < LANES // 2
            for s in range(n_slabs):
                for t2 in range(S5_T // 2):
                    even = s_ref[s, pl.ds(2 * t2, fold, stride=S5_T), :]
                    odd = s_ref[s, pl.ds(2 * t2 + 1, fold, stride=S5_T), :]
                    cols = slice(t2 * LANES, (t2 + 1) * LANES)
                    u_ref[2 * s, :, cols] = (
                        jnp.where(low, even, pltpu.roll(odd, LANES // 2, axis=1)).astype(BF16))
                    u_ref[2 * s + 1, :, cols] = (
                        jnp.where(low, pltpu.roll(even, LANES // 2, axis=1), odd).astype(BF16))
        else:
            for s in range(n_slabs):
                p_ref[(n - 1) * n_slabs + s] = res[:, s * LANES:(s + 1) * LANES].astype(BF16)
    low = lax.dot_general(h_ref[...], w_ref[d_main:d_main + LANES, :], NT_DIMS,
                          preferred_element_type=F32) * rs
    l_ref[...] = low.astype(BF16)


def _in_proj(x2d, norm_w, w_t, layer, *, d_main, tm, tn):
    m, d = x2d.shape
    n_slabs = tn // LANES
    n_proj = (d_main // tn - 1) * n_slabs
    assert d_main < w_t.shape[1] <= d_main + LANES
    return pl.pallas_call(
        functools.partial(_in_proj_kernel, layer=layer, d_main=d_main, tn=tn),
        grid=(m // tm,),
        in_specs=[
            pl.BlockSpec((tm, d), lambda i: (i, 0)),
            pl.BlockSpec((1, d), lambda i: (0, 0)),
            pl.BlockSpec(memory_space=pl.ANY),
        ],
        out_specs=[
            pl.BlockSpec((2 * n_slabs, tm // S5_T, S5_T * LANES // 2), lambda i: (0, i, 0)),
            pl.BlockSpec((n_proj, tm, LANES), lambda i: (0, i, 0)),
            pl.BlockSpec((tm, LANES), lambda i: (i, 0)),
        ],
        out_shape=[
            jax.ShapeDtypeStruct((2 * n_slabs, m // S5_T, S5_T * LANES // 2), BF16),
            jax.ShapeDtypeStruct((n_proj, m, LANES), BF16),
            jax.ShapeDtypeStruct((m, LANES), BF16),
        ],
        scratch_shapes=[
            pltpu.VMEM((tm, d), BF16),
            pltpu.VMEM((n_slabs, tm, LANES), F32),
            pltpu.VMEM((d_main + LANES, d), BF16),
            pltpu.VMEM((2, IN_PROJ_W_CHUNK, d), w_t.dtype),
            pltpu.SemaphoreType.DMA((2,)),
        ],
        compiler_params=pltpu.CompilerParams(
            dimension_semantics=("arbitrary",),
            vmem_limit_bytes=VMEM_LIMIT),
        name="in_proj",
    )(x2d, norm_w, w_t)


def _s5_kernel(x_ref, kb_ref, u_ref, v_ref, a_ref, d_ref, o_ref,
               m_ref, wz_ref, wy_ref, z_ref, sp_ref, y_ref, *, rows_per_seq, n_seq):
    ns = S5_T // 2
    rows = x_ref.shape[1]
    half = a_ref.shape[-1] // 2
    hstate = half // 2
    pairs = hstate // LANES
    sub = LANES // (2 * pairs)
    wide = 2 * LANES

    @pl.when(pl.program_id(0) == 0)
    def _():
        m_ref[:, (ns - 1) * LANES:, :LANES] = jnp.zeros((2, LANES, LANES), BF16)
        wz_ref[...] = jnp.zeros_like(wz_ref)
        y_ref[...] = jnp.zeros_like(y_ref)

    lane = lax.broadcasted_iota(jnp.int32, (LANES, LANES), 1)
    for hh in range(2):
        for s in range(ns):
            for k in range(2):
                lag = ns - 2 + k - s
                if lag >= 0:
                    m_ref[hh, s * LANES:(s + 1) * LANES, k * LANES:(k + 1) * LANES] = kb_ref[hh, lag]
        for s in range(ns):
            for part in range(2):
                for rho in range(2):
                    for a in range(pairs):
                        r0 = rho * (LANES // 2) + a * sub
                        c0 = part * hstate + a * LANES
                        wz_ref[hh, s * LANES + r0:s * LANES + r0 + sub, c0:c0 + LANES] = (
                            u_ref[hh, s, part, r0:r0 + sub, :])
        for t in range(ns):
            for part in range(2):
                blk = v_ref[hh, t, part]
                for a in range(pairs):
                    r0 = part * hstate + a * LANES
                    wy_ref[hh, r0:r0 + LANES, t * LANES:(t + 1) * LANES] = (
                        jnp.where(((lane % (LANES // 2)) // sub) == a, blk, jnp.zeros_like(blk)))
        zh = jnp.dot(x_ref[hh], wz_ref[hh], preferred_element_type=F32)
        z_ref[:, hh * hstate:(hh + 1) * hstate] = zh[:, :hstate]
        z_ref[:, half + hh * hstate:half + (hh + 1) * hstate] = zh[:, hstate:]

    a_re = a_ref[:, :half]
    a_im = a_ref[:, half:]

    def body(c, carry):
        new = []
        for b in range(n_seq):
            s_re, s_im = carry[2 * b], carry[2 * b + 1]
            r = b * rows_per_seq + c
            sp_ref[pl.ds(r, 1), :half] = s_re
            sp_ref[pl.ds(r, 1), half:] = s_im
            z_re = z_ref[pl.ds(r, 1), :half]
            z_im = z_ref[pl.ds(r, 1), half:]
            new.append(a_re * s_re - a_im * s_im + z_re)
            new.append(a_re * s_im + a_im * s_re + z_im)
        return tuple(new)

    zero = jnp.zeros((1, half), F32)
    lax.fori_loop(0, rows_per_seq, body, (zero,) * (2 * n_seq), unroll=8)

    spb = [jnp.concatenate([sp_ref[:, hh * hstate:(hh + 1) * hstate],
                            sp_ref[:, half + hh * hstate:half + (hh + 1) * hstate]],
                           axis=1).astype(BF16) for hh in range(2)]
    pitch = y_ref.shape[0] // rows
    low = lax.broadcasted_iota(jnp.int32, (rows, LANES), 1) < LANES // 2
    for n in range(ns // 2):
        cols = slice(n * wide, (n + 1) * wide)
        kk = (n + 1) * wide
        ys = []
        for hh in range(2):
            acc = jnp.dot(x_ref[hh, :, :kk], m_ref[hh, ns * LANES - kk:, :],
                          preferred_element_type=F32)
            acc = acc + jnp.dot(spb[hh], wy_ref[hh, :, cols], preferred_element_type=F32)
            acc = acc + d_ref[hh, :, cols] * x_ref[hh, :, cols].astype(F32)
            ys.append(jax.nn.gelu(acc))
        for k in range(2):
            y0 = ys[0][:, k * LANES:(k + 1) * LANES]
            y1 = ys[1][:, k * LANES:(k + 1) * LANES]
            t = 2 * (2 * n + k)
            y_ref[pl.ds(t, rows, stride=pitch), :] = (
                jnp.where(low, y0, pltpu.roll(y1, LANES // 2, axis=1)))
            y_ref[pl.ds(t + 1, rows, stride=pitch), :] = (
                jnp.where(low, pltpu.roll(y0, LANES // 2, axis=1), y1))
    tokens = y_ref[...].reshape(rows, pitch, LANES)[:, :S5_T, :].reshape(rows * S5_T, LANES)
    o_ref[...] = tokens.astype(BF16)


def _s5(u2, kb, u_op, v_op, abar, dtile, *, n_seq):
    n_j, _, ns, _, _ = kb.shape
    rows = u2.shape[1]
    width = ns * LANES
    n_state = abar.shape[-1]
    return pl.pallas_call(
        functools.partial(_s5_kernel, rows_per_seq=rows // n_seq, n_seq=n_seq),
        grid=(n_j,),
        in_specs=[
            pl.BlockSpec((2, rows, width), lambda j: (j, 0, 0)),
            pl.BlockSpec((None, 2, ns, LANES, LANES), lambda j: (j, 0, 0, 0, 0)),
            pl.BlockSpec((None, 2, ns, 2, LANES, LANES), lambda j: (j, 0, 0, 0, 0, 0)),
            pl.BlockSpec((None, 2, ns, 2, LANES, LANES), lambda j: (j, 0, 0, 0, 0, 0)),
            pl.BlockSpec((None, 1, n_state), lambda j: (j, 0, 0)),
            pl.BlockSpec((None, 2, 1, width), lambda j: (j, 0, 0, 0)),
        ],
        out_specs=pl.BlockSpec((None, rows * S5_T, LANES), lambda j: (j, 0, 0)),
        out_shape=jax.ShapeDtypeStruct((n_j, rows * S5_T, LANES), BF16),
        scratch_shapes=[
            pltpu.VMEM((2, width, 2 * LANES), BF16),
            pltpu.VMEM((2, width, n_state // 2), BF16),
            pltpu.VMEM((2, n_state // 2, width), BF16),
            pltpu.VMEM((rows, n_state), F32),
            pltpu.VMEM((rows, n_state), F32),
            pltpu.VMEM((rows * S5_UNFOLD_PITCH, LANES), F32),
        ],
        compiler_params=pltpu.CompilerParams(
            dimension_semantics=("arbitrary",),
            vmem_limit_bytes=VMEM_LIMIT),
        name="s5",
    )(u2, kb, u_op, v_op, abar, dtile)


def _log_sigmoid(x):
    return jnp.minimum(x, 0.0) - jnp.log(1.0 + jnp.exp(-jnp.abs(x)))


def _gla_kernel(q_ref, k_ref, v_ref, gz_ref, gl_ref, gu_ref, gb_ref, nw_ref, o_ref, st_ref,
                *, chunk):
    n_heads, rows, dk = q_ref.shape
    n_chunks = rows // chunk
    d_gk = n_heads * dk

    @pl.when(pl.program_id(1) == 0)
    def _():
        st_ref[...] = jnp.zeros_like(st_ref)

    zg = jnp.dot(gl_ref[...], gu_ref[...], preferred_element_type=F32) + gb_ref[...]
    lg = _log_sigmoid(zg) * (1.0 / GLA_TAU)

    hi = lg.astype(BF16)
    lo = (lg - hi.astype(F32)).astype(BF16)
    r_id = lax.broadcasted_iota(jnp.int32, (chunk, chunk), 0)
    c_id = lax.broadcasted_iota(jnp.int32, (chunk, chunk), 1)
    causal = c_id <= r_id
    tri = jnp.where(causal, 1.0, 0.0).astype(BF16)
    tri2 = jnp.concatenate([tri, tri], axis=1)
    b_parts, last_parts = [], []
    for n in range(n_chunks):
        rs = slice(n * chunk, (n + 1) * chunk)
        b_n = jnp.dot(tri2, jnp.concatenate([hi[rs], lo[rs]], axis=0), preferred_element_type=F32)
        b_parts.append(b_n)
        last_parts.append(jnp.broadcast_to(b_n[chunk - 1:chunk], (chunk, d_gk)))
    b = jnp.concatenate(b_parts, axis=0)
    e_pos = jnp.exp(b)
    e_neg = jnp.exp(-b)
    decay = jnp.exp(jnp.concatenate(last_parts, axis=0))

    for h in range(n_heads):
        hs = slice(h * dk, (h + 1) * dk)
        q_e = (q_ref[h].astype(F32) * ((dk ** -0.5) * e_pos[:, hs])).astype(BF16)
        k_f = k_ref[h].astype(F32) * e_neg[:, hs]
        k_e = k_f.astype(BF16)
        k_t = (k_f * decay[:, hs]).astype(BF16)
        v = jnp.concatenate([v_ref[2 * h], v_ref[2 * h + 1]], axis=1)

        chunks = [slice(n * chunk, (n + 1) * chunk) for n in range(n_chunks)]
        attn = [lax.dot_general(q_e[rs], k_e[rs], NT_DIMS, preferred_element_type=F32)
                for rs in chunks]
        kv_t = [lax.dot_general(v[rs], k_t[rs], TN_DIMS, preferred_element_type=F32)
                for rs in chunks]
        attn = [jnp.where(causal, a, 0.0).astype(BF16) for a in attn]
        st = st_ref[h]
        st_in = []
        for n in range(n_chunks):
            st_in.append(st.astype(BF16))
            st = decay[n * chunk:n * chunk + 1, hs] * st + kv_t[n]
        st_ref[h] = st
        outs = [jnp.dot(attn[n], v[rs], preferred_element_type=F32)
                + lax.dot_general(q_e[rs], st_in[n], NT_DIMS, preferred_element_type=F32)
                for n, rs in enumerate(chunks)]

        o = jnp.concatenate(outs, axis=0)
        ms = jnp.mean(o * o, axis=-1, keepdims=True)
        o = o * lax.rsqrt(ms + EPS) * nw_ref[...]
        gz = jnp.concatenate([gz_ref[2 * h], gz_ref[2 * h + 1]], axis=1).astype(F32)
        y = (o * (gz * jax.nn.sigmoid(gz))).astype(BF16)
        o_ref[2 * h] = y[:, :LANES]
        o_ref[2 * h + 1] = y[:, LANES:]


def _gla(proj, g_low, gate_up_pad, gate_bias, norm_w, *, n_seq, seq_len, rows, slab_q, slab_k,
         slab_v, slab_gz):
    m = proj.shape[1]
    nb = seq_len // rows
    nh = GLA_HEADS
    dk = LANES
    dv = 2 * LANES
    tok = lambda b, n: b * nb + n
    return pl.pallas_call(
        functools.partial(_gla_kernel, chunk=GLA_CHUNK),
        grid=(n_seq, nb),
        in_specs=[
            pl.BlockSpec((nh, rows, LANES), lambda b, n: (slab_q // nh, tok(b, n), 0)),
            pl.BlockSpec((nh, rows, LANES), lambda b, n: (slab_k // nh, tok(b, n), 0)),
            pl.BlockSpec((2 * nh, rows, LANES), lambda b, n: (slab_v // (2 * nh), tok(b, n), 0)),
            pl.BlockSpec((2 * nh, rows, LANES), lambda b, n: (slab_gz // (2 * nh), tok(b, n), 0)),
            pl.BlockSpec((rows, LANES), lambda b, n: (tok(b, n), 0)),
            pl.BlockSpec((LANES, nh * dk), lambda b, n: (0, 0)),
            pl.BlockSpec((1, nh * dk), lambda b, n: (0, 0)),
            pl.BlockSpec((1, dv), lambda b, n: (0, 0)),
        ],
        out_specs=pl.BlockSpec((2 * nh, rows, LANES), lambda b, n: (0, tok(b, n), 0)),
        out_shape=jax.ShapeDtypeStruct((2 * nh, m, LANES), BF16),
        scratch_shapes=[pltpu.VMEM((nh, dv, dk), F32)],
        compiler_params=pltpu.CompilerParams(
            dimension_semantics=("parallel", "arbitrary"),
            vmem_limit_bytes=VMEM_LIMIT),
        name="gla",
    )(proj, proj, proj, proj, g_low, gate_up_pad, gate_bias, norm_w)


def _out_proj_kernel(g_ref, z_ref, yg_ref, x_ref, gw_ref, gb_ref, wo_ref, pw_ref, o_ref):
    n_s5 = g_ref.shape[0]
    d_s5 = n_s5 * LANES
    tm = x_ref.shape[0]
    rc = OUT_PROJ_ROW_CHUNK
    for c in range(tm // rc):
        rows = slice(c * rc, (c + 1) * rc)
        g = jnp.concatenate([g_ref[s, rows, :] for s in range(n_s5)], axis=1)
        gate = jax.nn.sigmoid(jnp.dot(g, gw_ref[...], preferred_element_type=F32) + gb_ref[...])
        z = jnp.concatenate([z_ref[s, rows, :] for s in range(n_s5)], axis=1).astype(F32)
        y_s5 = (g.astype(F32) * gate * (z * jax.nn.sigmoid(z))).astype(BF16)
        y_gla = jnp.concatenate([yg_ref[s, rows, :] for s in range(yg_ref.shape[0])], axis=1)
        mixed = (jnp.dot(y_s5, wo_ref[:d_s5, :], preferred_element_type=F32)
                 + jnp.dot(y_gla, wo_ref[d_s5:, :], preferred_element_type=F32))
        ms = jnp.mean(mixed * mixed, axis=-1, keepdims=True)
        o_ref[rows, :] = x_ref[rows, :] + mixed * lax.rsqrt(ms + EPS) * pw_ref[...]


def _out_proj(g_slabs, proj, y_gla, x2d, glu_w, glu_b, w_out, post_w, *, tm, slab_z):
    m, d = x2d.shape
    n_s5 = g_slabs.shape[0]
    n_gla = y_gla.shape[0]
    d_s5 = n_s5 * LANES
    d_mix = w_out.shape[0]
    return pl.pallas_call(
        _out_proj_kernel,
        grid=(m // tm,),
        in_specs=[
            pl.BlockSpec((n_s5, tm, LANES), lambda i: (0, i, 0)),
            pl.BlockSpec((n_s5, tm, LANES), lambda i: (slab_z // n_s5, i, 0)),
            pl.BlockSpec((n_gla, tm, LANES), lambda i: (0, i, 0)),
            pl.BlockSpec((tm, d), lambda i: (i, 0)),
            pl.BlockSpec((d_s5, d_s5), lambda i: (0, 0)),
            pl.BlockSpec((1, d_s5), lambda i: (0, 0)),
            pl.BlockSpec((d_mix, d), lambda i: (0, 0)),
            pl.BlockSpec((1, d), lambda i: (0, 0)),
        ],
        out_specs=pl.BlockSpec((tm, d), lambda i: (i, 0)),
        out_shape=jax.ShapeDtypeStruct((m, d), F32),
        compiler_params=pltpu.CompilerParams(
            dimension_semantics=("parallel",),
            vmem_limit_bytes=VMEM_LIMIT),
        name="out_proj",
    )(g_slabs, proj, y_gla, x2d, glu_w, glu_b, w_out, post_w)


def kernel(x, pre_norm_w, w_in, s5_A_re, s5_A_im, s5_B_re, s5_B_im, s5_C_re, s5_C_im, s5_D,
           s5_log_dt, s5_glu_w, s5_glu_b, gla_gate_up, gla_gate_bias, gla_norm_w, w_out,
           post_norm_w):
    bsz, seq_len, d_model = x.shape
    depth = w_in.shape[0]
    d_in = w_in.shape[2]
    d_s5 = s5_glu_w.shape[1]
    rank, d_gk = gla_gate_up.shape[1:]
    d_gv = GLA_HEADS * gla_norm_w.shape[1]
    m = bsz * seq_len
    d_main = 2 * d_s5 + 2 * d_gk + 2 * d_gv
    tn = d_s5
    assert d_s5 % LANES == 0 and d_gk == GLA_HEADS * LANES and d_gv == GLA_HEADS * 2 * LANES
    assert d_in == d_main + rank and rank <= LANES and d_main % tn == 0 and 2 * d_gk == tn
    assert seq_len % (S5_T * 8) == 0 and w_out.shape[1] == 2 * d_s5

    n_s5 = d_s5 // LANES
    slab_z = 0
    slab_q = n_s5
    slab_k = slab_q + d_gk // LANES
    slab_v = slab_k + d_gk // LANES
    slab_gz = slab_v + d_gv // LANES

    resid = x.astype(F32).reshape(m, d_model)
    for l in range(depth):
        kb, u_op, v_op, abar = _s5_ops(s5_A_re[l], s5_A_im[l], s5_B_re[l], s5_B_im[l],
                                       s5_C_re[l], s5_C_im[l], s5_log_dt[l])
        dtile = jnp.tile(s5_D[l].astype(F32).reshape(n_s5, 2, 1, LANES // 2), (1, 1, 1, S5_T))

        u2, proj, g_low = _in_proj(resid, pre_norm_w[l].astype(F32)[None],
                                   jnp.swapaxes(w_in, 1, 2), l, d_main=d_main, tm=512, tn=tn)

        g_slabs = _s5(u2, kb, u_op, v_op, abar, dtile, n_seq=bsz)

        gate_up_pad = jnp.pad(gla_gate_up[l].astype(BF16), ((0, LANES - rank), (0, 0)))
        y_gla = _gla(proj, g_low, gate_up_pad, gla_gate_bias[l].astype(F32)[None],
                     gla_norm_w[l].astype(F32)[None], n_seq=bsz, seq_len=seq_len, rows=512,
                     slab_q=slab_q, slab_k=slab_k, slab_v=slab_v, slab_gz=slab_gz)

        resid = _out_proj(g_slabs, proj, y_gla, resid, s5_glu_w[l].astype(BF16),
                          s5_glu_b[l].astype(F32)[None], w_out[l].astype(BF16),
                          post_norm_w[l].astype(F32)[None], tm=512, slab_z=slab_z)
    return resid.reshape(bsz, seq_len, d_model).astype(x.dtype)
```

```python
import functools

import jax
import jax.numpy as jnp
from jax import lax
from jax.experimental import pallas as pl
from jax.experimental.pallas import tpu as pltpu

F32 = jnp.float32
BF16 = jnp.bfloat16

S5_GROUP = 16
GLA_HEADS = 4
GLA_TAU = 16.0
GLA_CHUNK = 64
EPS = 1e-6

LANES = 128
S5_T = 16
S5_UNFOLD_PITCH = 24
OUT_PROJ_ROW_CHUNK = 256
IN_PROJ_W_CHUNK = 256
OUT_PROJ_W_CHUNK = 256
VMEM_LIMIT = 56 * 1024 * 1024


NT_DIMS = (((1,), (1,)), ((), ()))
TN_DIMS = (((0,), (0,)), ((), ()))


def _cmul(ar, ai, br, bi):
    return ar * br - ai * bi, ar * bi + ai * br


def _stream_cast_rows(src, dst_ref, stage_ref, sem):
    n_rows, width = src.shape
    chunk = stage_ref.shape[1]
    sizes = [chunk] * (n_rows // chunk) + ([n_rows % chunk] if n_rows % chunk else [])

    def copy(c):
        return pltpu.make_async_copy(
            src.at[pl.ds(c * chunk, sizes[c]), :],
            stage_ref.at[c % 2, pl.ds(0, sizes[c]), pl.ds(0, width)], sem.at[c % 2])

    for c in range(min(2, len(sizes))):
        copy(c).start()
    for c, rows in enumerate(sizes):
        copy(c).wait()
        dst_ref[c * chunk:c * chunk + rows, :width] = (
            stage_ref[c % 2, :rows, :width].astype(dst_ref.dtype))
        if c + 2 < len(sizes):
            copy(c + 2).start()


def _s5_ops_kernel(tp_ref, np_ref, kb_ref, u_ref, v_ref, ab_ref, *, n_state):
    nt = S5_T
    ns = nt // 2
    half = LANES // 2

    def a_bar(are_raw, aim, ldt):
        are = jnp.minimum(are_raw, -1e-4)
        dt = jnp.exp(ldt)
        mag = jnp.exp(are * dt)
        return are, mag * jnp.cos(aim * dt), mag * jnp.sin(aim * dt)

    are, abr, abi = a_bar(tp_ref[0], tp_ref[1], tp_ref[2])
    aim = tp_ref[1]
    den = are * are + aim * aim
    nr = abr - 1.0
    fr = (nr * are + abi * aim) / den
    fi = (abi * are - nr * aim) / den
    xr, xi = _cmul(fr, fi, tp_ref[3], tp_ref[4])

    row = lax.broadcasted_iota(jnp.int32, (LANES, LANES), 0)
    lane = lax.broadcasted_iota(jnp.int32, (LANES, LANES), 1)
    own_half = ((row // S5_GROUP) % 2) == (lane // half)
    same_group = ((row % half) // S5_GROUP) == ((lane % half) // S5_GROUP)
    first_half = lane < half

    def split(t):
        hi = t.astype(BF16)
        return hi, (t - hi.astype(F32)).astype(BF16)

    c_stack = jnp.concatenate([np_ref[3], np_ref[4]], axis=0)
    c_swap = pltpu.roll(c_stack, half, axis=1)
    c_hi, c_lo = split(jnp.concatenate([c_stack, c_swap], axis=1))
    c_terms = jnp.concatenate([c_hi, c_lo, c_hi], axis=0)
    kers, kers_swap, xs = [], [], []
    for tau in range(nt):
        l_hi, l_lo = split(jnp.where(first_half, xr, -xi))
        ker = jnp.dot(jnp.concatenate([l_hi, l_hi, l_lo], axis=1), c_terms,
                      preferred_element_type=F32)
        kers.append(ker[:, :LANES])
        kers_swap.append(ker[:, LANES:])
        xs.append((jnp.where(own_half, xr, 0.0).astype(BF16),
                   jnp.where(own_half, xi, 0.0).astype(BF16)))
        if tau + 1 < nt:
            xr, xi = _cmul(xr, xi, abr, abi)

    zero = jnp.zeros((half, LANES), F32)
    first_half_rows = lax.broadcasted_iota(jnp.int32, (half, LANES), 1) < half
    for hh in range(2):
        rows = slice(hh * half, (hh + 1) * half)
        low, high = (kers, kers_swap) if hh == 0 else (kers_swap, kers)
        pick = lambda src, tau: src[tau][rows, :] if tau >= 0 else zero
        for d in range(ns):
            top = jnp.where(first_half_rows, pick(low, 2 * d), pick(high, 2 * d + 1))
            bot = jnp.where(first_half_rows, pick(low, 2 * d - 1), pick(high, 2 * d))
            blk = jnp.concatenate([top, bot], axis=0)
            kb_ref[hh, d] = jnp.where(same_group, blk, 0.0).astype(BF16)
        for s2 in range(ns):
            for part in range(2):
                u_ref[hh, s2, part] = jnp.concatenate(
                    [xs[nt - 1 - 2 * s2][part][rows, :], xs[nt - 2 - 2 * s2][part][rows, :]], axis=0)

    pr, pi = abr, abi
    for _ in range(4):
        pr, pi = _cmul(pr, pi, pr, pi)
    assert nt == 16
    pairs = LANES // (2 * S5_GROUP)
    for part, val in enumerate((pr, pi)):
        for a in range(pairs):
            r0 = 2 * a * S5_GROUP
            piece = jnp.where(first_half[:1], val[r0:r0 + 1], val[r0 + S5_GROUP:r0 + S5_GROUP + 1])
            c0 = part * (n_state // 2) + a * LANES
            ab_ref[:, c0:c0 + LANES] = piece

    _, nbr, nbi = a_bar(np_ref[0], np_ref[1], np_ref[2])
    cr, ci = np_ref[3], np_ref[4]
    lane_n = lax.broadcasted_iota(jnp.int32, (half, LANES), 1)
    parity = (lane_n // S5_GROUP) % 2
    first_n = lane_n < half
    pr, pi = nbr, nbi
    outs = []
    for t in range(nt):
        wr, wi = _cmul(cr, ci, pr, pi)
        outs.append((wr, -wi))
        if t + 1 < nt:
            pr, pi = _cmul(pr, pi, nbr, nbi)
    for t2 in range(ns):
        for part in range(2):
            even, odd = outs[2 * t2][part], outs[2 * t2 + 1][part]
            packed = (jnp.where(first_n, even, pltpu.roll(odd, half, axis=1)),
                      jnp.where(first_n, pltpu.roll(even, half, axis=1), odd))
            for hh in range(2):
                for q in range(2):
                    v_ref[hh, t2, part, q * half:(q + 1) * half, :] = (
                        jnp.where(parity == q, packed[hh], 0.0).astype(BF16))


def _s5_ops(a_re, a_im, b_re, b_im, c_re, c_im, log_dt):
    g, p = a_re.shape
    h = S5_GROUP
    gps = LANES // h
    n_j = g // gps
    assert 2 * p == LANES and g % gps == 0
    n_state = 2 * gps * p
    f = lambda t: t.astype(F32)

    def t_arr(t):
        t = f(t).reshape(n_j, gps, -1, 1, p)
        return jnp.broadcast_to(t, (n_j, gps, h, 2, p)).reshape(n_j, LANES, LANES)

    def n_arr(t):
        t = jnp.swapaxes(f(t).reshape(n_j, gps, -1, p), 1, 3)
        t = jnp.swapaxes(jnp.broadcast_to(t, (n_j, p, h, gps)), 2, 3)
        return t.reshape(n_j, p, LANES)

    ldt = jnp.broadcast_to(f(log_dt)[:, None], (g, p))
    tpar = jnp.stack([t_arr(a_re), t_arr(a_im), t_arr(ldt),
                      t_arr(jnp.swapaxes(b_re, 1, 2)), t_arr(jnp.swapaxes(b_im, 1, 2))], axis=1)
    npar = jnp.stack([n_arr(a_re), n_arr(a_im), n_arr(ldt), n_arr(c_re), n_arr(c_im)], axis=1)
    ns = S5_T // 2
    return pl.pallas_call(
        functools.partial(_s5_ops_kernel, n_state=n_state),
        grid=(n_j,),
        in_specs=[
            pl.BlockSpec((None, 5, LANES, LANES), lambda j: (j, 0, 0, 0)),
            pl.BlockSpec((None, 5, p, LANES), lambda j: (j, 0, 0, 0)),
        ],
        out_specs=[
            pl.BlockSpec((None, 2, ns, LANES, LANES), lambda j: (j, 0, 0, 0, 0)),
            pl.BlockSpec((None, 2, ns, 2, LANES, LANES), lambda j: (j, 0, 0, 0, 0, 0)),
            pl.BlockSpec((None, 2, ns, 2, LANES, LANES), lambda j: (j, 0, 0, 0, 0, 0)),
            pl.BlockSpec((None, 1, n_state), lambda j: (j, 0, 0)),
        ],
        out_shape=[
            jax.ShapeDtypeStruct((n_j, 2, ns, LANES, LANES), BF16),
            jax.ShapeDtypeStruct((n_j, 2, ns, 2, LANES, LANES), BF16),
            jax.ShapeDtypeStruct((n_j, 2, ns, 2, LANES, LANES), BF16),
            jax.ShapeDtypeStruct((n_j, 1, n_state), F32),
        ],
        compiler_params=pltpu.CompilerParams(dimension_semantics=("parallel",)),
        name="s5_ops",
    )(tpar, npar)


def _in_proj_kernel(x_ref, nw_ref, w_hbm, u_ref, p_ref, l_ref, h_ref, s_ref, w_ref, stage_ref, sem,
                    *, layer, d_main, tn):
    tm = x_ref.shape[0]
    n_slabs = tn // LANES
    fold = tm // S5_T

    @pl.when(pl.program_id(0) == 0)
    def _():
        n_rows = w_hbm.shape[1]
        w_ref[n_rows:, :] = jnp.zeros((w_ref.shape[0] - n_rows, w_ref.shape[1]), BF16)
        _stream_cast_rows(w_hbm.at[layer], w_ref, stage_ref, sem)

    xf = x_ref[...]
    h_ref[...] = (xf * nw_ref[...]).astype(BF16)
    rs = lax.rsqrt(jnp.mean(xf * xf, axis=-1, keepdims=True) + EPS)

    for n in range(d_main // tn):
        res = lax.dot_general(h_ref[...], w_ref[n * tn:(n + 1) * tn, :], NT_DIMS,
                              preferred_element_type=F32) * rs
        if n == 0:
            for s in range(n_slabs):
                s_ref[s] = res[:, s * LANES:(s + 1) * LANES]
            low = lax.broadcasted_iota(jnp.int32, (fold, LANES), 1) < LANES // 2
            for s in range(n_slabs):
                for t2 in range(S5_T // 2):
                    even = s_ref[s, pl.ds(2 * t2, fold, stride=S5_T), :]
                    odd = s_ref[s, pl.ds(2 * t2 + 1, fold, stride=S5_T), :]
                    cols = slice(t2 * LANES, (t2 + 1) * LANES)
                    u_ref[2 * s, :, cols] = (
                        jnp.where(low, even, pltpu.roll(odd, LANES // 2, axis=1)).astype(BF16))
                    u_ref[2 * s + 1, :, cols] = (
                        jnp.where(low, pltpu.roll(even, LANES // 2, axis=1), odd).astype(BF16))
        else:
            for s in range(n_slabs):
                p_ref[(n - 1) * n_slabs + s] = res[:, s * LANES:(s + 1) * LANES].astype(BF16)
    low = lax.dot_general(h_ref[...], w_ref[d_main:d_main + LANES, :], NT_DIMS,
                          preferred_element_type=F32) * rs
    l_ref[...] = low.astype(BF16)


def _in_proj(x2d, norm_w, w_t, layer, *, d_main, tm, tn):
    m, d = x2d.shape
    n_slabs = tn // LANES
    n_proj = (d_main // tn - 1) * n_slabs
    assert d_main < w_t.shape[1] <= d_main + LANES
    return pl.pallas_call(
        functools.partial(_in_proj_kernel, layer=layer, d_main=d_main, tn=tn),
        grid=(m // tm,),
        in_specs=[
            pl.BlockSpec((tm, d), lambda i: (i, 0)),
            pl.BlockSpec((1, d), lambda i: (0, 0)),
            pl.BlockSpec(memory_space=pl.ANY),
        ],
        out_specs=[
            pl.BlockSpec((2 * n_slabs, tm // S5_T, S5_T * LANES // 2), lambda i: (0, i, 0)),
            pl.BlockSpec((n_proj, tm, LANES), lambda i: (0, i, 0)),
            pl.BlockSpec((tm, LANES), lambda i: (i, 0)),
        ],
        out_shape=[
            jax.ShapeDtypeStruct((2 * n_slabs, m // S5_T, S5_T * LANES // 2), BF16),
            jax.ShapeDtypeStruct((n_proj, m, LANES), BF16),
            jax.ShapeDtypeStruct((m, LANES), BF16),
        ],
        scratch_shapes=[
            pltpu.VMEM((tm, d), BF16),
            pltpu.VMEM((n_slabs, tm, LANES), F32),
            pltpu.VMEM((d_main + LANES, d), BF16),
            pltpu.VMEM((2, IN_PROJ_W_CHUNK, d), w_t.dtype),
            pltpu.SemaphoreType.DMA((2,)),
        ],
        compiler_params=pltpu.CompilerParams(
            dimension_semantics=("arbitrary",),
            vmem_limit_bytes=VMEM_LIMIT),
        name="in_proj",
    )(x2d, norm_w, w_t)


def _s5_kernel(x_ref, kb_ref, u_ref, v_ref, a_ref, d_ref, o_ref,
               m_ref, wz_ref, wy_ref, z_ref, sp_ref, y_ref, *, rows_per_seq, n_seq):
    ns = S5_T // 2
    rows = x_ref.shape[1]
    half = a_ref.shape[-1] // 2
    hstate = half // 2
    pairs = hstate // LANES
    sub = LANES // (2 * pairs)
    wide = 2 * LANES

    @pl.when(pl.program_id(0) == 0)
    def _():
        m_ref[:, (ns - 1) * LANES:, :LANES] = jnp.zeros((2, LANES, LANES), BF16)
        wz_ref[...] = jnp.zeros_like(wz_ref)
        y_ref[...] = jnp.zeros_like(y_ref)

    lane = lax.broadcasted_iota(jnp.int32, (LANES, LANES), 1)
    for hh in range(2):
        for s in range(ns):
            for k in range(2):
                lag = ns - 2 + k - s
                if lag >= 0:
                    m_ref[hh, s * LANES:(s + 1) * LANES, k * LANES:(k + 1) * LANES] = kb_ref[hh, lag]
        for s in range(ns):
            for part in range(2):
                for rho in range(2):
                    for a in range(pairs):
                        r0 = rho * (LANES // 2) + a * sub
                        c0 = part * hstate + a * LANES
                        wz_ref[hh, s * LANES + r0:s * LANES + r0 + sub, c0:c0 + LANES] = (
                            u_ref[hh, s, part, r0:r0 + sub, :])
        for t in range(ns):
            for part in range(2):
                blk = v_ref[hh, t, part]
                for a in range(pairs):
                    r0 = part * hstate + a * LANES
                    wy_ref[hh, r0:r0 + LANES, t * LANES:(t + 1) * LANES] = (
                        jnp.where(((lane % (LANES // 2)) // sub) == a, blk, jnp.zeros_like(blk)))
        zh = jnp.dot(x_ref[hh], wz_ref[hh], preferred_element_type=F32)
        z_ref[:, hh * hstate:(hh + 1) * hstate] = zh[:, :hstate]
        z_ref[:, half + hh * hstate:half + (hh + 1) * hstate] = zh[:, hstate:]

    a_re = a_ref[:, :half]
    a_im = a_ref[:, half:]

    def body(c, carry):
        new = []
        for b in range(n_seq):
            s_re, s_im = carry[2 * b], carry[2 * b + 1]
            r = b * rows_per_seq + c
            sp_ref[pl.ds(r, 1), :half] = s_re
            sp_ref[pl.ds(r, 1), half:] = s_im
            z_re = z_ref[pl.ds(r, 1), :half]
            z_im = z_ref[pl.ds(r, 1), half:]
            new.append(a_re * s_re - a_im * s_im + z_re)
            new.append(a_re * s_im + a_im * s_re + z_im)
        return tuple(new)

    zero = jnp.zeros((1, half), F32)
    lax.fori_loop(0, rows_per_seq, body, (zero,) * (2 * n_seq), unroll=8)

    spb = [jnp.concatenate([sp_ref[:, hh * hstate:(hh + 1) * hstate],
                            sp_ref[:, half + hh * hstate:half + (hh + 1) * hstate]],
                           axis=1).astype(BF16) for hh in range(2)]
    pitch = y_ref.shape[0] // rows
    low = lax.broadcasted_iota(jnp.int32, (rows, LANES), 1) < LANES // 2
    for n in range(ns // 2):
        cols = slice(n * wide, (n + 1) * wide)
        kk = (n + 1) * wide
        ys = []
        for hh in range(2):
            acc = jnp.dot(x_ref[hh, :, :kk], m_ref[hh, ns * LANES - kk:, :],
                          preferred_element_type=F32)
            acc = acc + jnp.dot(spb[hh], wy_ref[hh, :, cols], preferred_element_type=F32)
            acc = acc + d_ref[hh, :, cols] * x_ref[hh, :, cols].astype(F32)
            ys.append(jax.nn.gelu(acc))
        for k in range(2):
            y0 = ys[0][:, k * LANES:(k + 1) * LANES]
            y1 = ys[1][:, k * LANES:(k + 1) * LANES]
            t = 2 * (2 * n + k)
            y_ref[pl.ds(t, rows, stride=pitch), :] = (
                jnp.where(low, y0, pltpu.roll(y1, LANES // 2, axis=1)))
            y_ref[pl.ds(t + 1, rows, stride=pitch), :] = (
                jnp.where(low, pltpu.roll(y0, LANES // 2, axis=1), y1))
    tokens = y_ref[...].reshape(rows, pitch, LANES)[:, :S5_T, :].reshape(rows * S5_T, LANES)
    o_ref[...] = tokens.astype(BF16)


def _s5(u2, kb, u_op, v_op, abar, dtile, *, n_seq):
    n_j, _, ns, _, _ = kb.shape
    rows = u2.shape[1]
    width = ns * LANES
    n_state = abar.shape[-1]
    return pl.pallas_call(
        functools.partial(_s5_kernel, rows_per_seq=rows // n_seq, n_seq=n_seq),
        grid=(n_j,),
        in_specs=[
            pl.BlockSpec((2, rows, width), lambda j: (j, 0, 0)),
            pl.BlockSpec((None, 2, ns, LANES, LANES), lambda j: (j, 0, 0, 0, 0)),
            pl.BlockSpec((None, 2, ns, 2, LANES, LANES), lambda j: (j, 0, 0, 0, 0, 0)),
            pl.BlockSpec((None, 2, ns, 2, LANES, LANES), lambda j: (j, 0, 0, 0, 0, 0)),
            pl.BlockSpec((None, 1, n_state), lambda j: (j, 0, 0)),
            pl.BlockSpec((None, 2, 1, width), lambda j: (j, 0, 0, 0)),
        ],
        out_specs=pl.BlockSpec((None, rows * S5_T, LANES), lambda j: (j, 0, 0)),
        out_shape=jax.ShapeDtypeStruct((n_j, rows * S5_T, LANES), BF16),
        scratch_shapes=[
            pltpu.VMEM((2, width, 2 * LANES), BF16),
            pltpu.VMEM((2, width, n_state // 2), BF16),
            pltpu.VMEM((2, n_state // 2, width), BF16),
            pltpu.VMEM((rows, n_state), F32),
            pltpu.VMEM((rows, n_state), F32),
            pltpu.VMEM((rows * S5_UNFOLD_PITCH, LANES), F32),
        ],
        compiler_params=pltpu.CompilerParams(
            dimension_semantics=("arbitrary",),
            vmem_limit_bytes=VMEM_LIMIT),
        name="s5",
    )(u2, kb, u_op, v_op, abar, dtile)


def _log_sigmoid(x):
    return jnp.minimum(x, 0.0) - jnp.log(1.0 + jnp.exp(-jnp.abs(x)))


def _gla_kernel(q_ref, k_ref, v_ref, gz_ref, gl_ref, gu_ref, gb_ref, nw_ref, o_ref, st_ref,
                *, chunk):
    n_heads, rows, dk = q_ref.shape
    n_chunks = rows // chunk
    d_gk = n_heads * dk

    @pl.when(pl.program_id(1) == 0)
    def _():
        st_ref[...] = jnp.zeros_like(st_ref)

    zg = jnp.dot(gl_ref[...], gu_ref[...], preferred_element_type=F32) + gb_ref[...]
    lg = _log_sigmoid(zg) * (1.0 / GLA_TAU)

    hi = lg.astype(BF16)
    lo = (lg - hi.astype(F32)).astype(BF16)
    r_id = lax.broadcasted_iota(jnp.int32, (chunk, chunk), 0)
    c_id = lax.broadcasted_iota(jnp.int32, (chunk, chunk), 1)
    causal = c_id <= r_id
    tri = jnp.where(causal, 1.0, 0.0).astype(BF16)
    tri2 = jnp.concatenate([tri, tri], axis=1)
    b_parts, last_parts = [], []
    for n in range(n_chunks):
        rs = slice(n * chunk, (n + 1) * chunk)
        b_n = jnp.dot(tri2, jnp.concatenate([hi[rs], lo[rs]], axis=0), preferred_element_type=F32)
        b_parts.append(b_n)
        last_parts.append(jnp.broadcast_to(b_n[chunk - 1:chunk], (chunk, d_gk)))
    b = jnp.concatenate(b_parts, axis=0)
    e_pos = jnp.exp(b)
    e_neg = jnp.exp(-b)
    decay = jnp.exp(jnp.concatenate(last_parts, axis=0))

    for h in range(n_heads):
        hs = slice(h * dk, (h + 1) * dk)
        q_e = (q_ref[h].astype(F32) * ((dk ** -0.5) * e_pos[:, hs])).astype(BF16)
        k_f = k_ref[h].astype(F32) * e_neg[:, hs]
        k_e = k_f.astype(BF16)
        k_t = (k_f * decay[:, hs]).astype(BF16)
        v = jnp.concatenate([v_ref[2 * h], v_ref[2 * h + 1]], axis=1)

        chunks = [slice(n * chunk, (n + 1) * chunk) for n in range(n_chunks)]
        attn = [lax.dot_general(q_e[rs], k_e[rs], NT_DIMS, preferred_element_type=F32)
                for rs in chunks]
        kv_t = [lax.dot_general(v[rs], k_t[rs], TN_DIMS, preferred_element_type=F32)
                for rs in chunks]
        attn = [jnp.where(causal, a, 0.0).astype(BF16) for a in attn]
        st = st_ref[h]
        st_in = []
        for n in range(n_chunks):
            st_in.append(st.astype(BF16))
            st = decay[n * chunk:n * chunk + 1, hs] * st + kv_t[n]
        st_ref[h] = st
        outs = [jnp.dot(attn[n], v[rs], preferred_element_type=F32)
                + lax.dot_general(q_e[rs], st_in[n], NT_DIMS, preferred_element_type=F32)
                for n, rs in enumerate(chunks)]

        o = jnp.concatenate(outs, axis=0)
        ms = jnp.mean(o * o, axis=-1, keepdims=True)
        o = o * lax.rsqrt(ms + EPS) * nw_ref[...]
        gz = jnp.concatenate([gz_ref[2 * h], gz_ref[2 * h + 1]], axis=1).astype(F32)
        y = (o * (gz * jax.nn.sigmoid(gz))).astype(BF16)
        o_ref[2 * h] = y[:, :LANES]
        o_ref[2 * h + 1] = y[:, LANES:]


def _gla(proj, g_low, gate_up_pad, gate_bias, norm_w, *, n_seq, seq_len, rows, slab_q, slab_k,
         slab_v, slab_gz):
    m = proj.shape[1]
    nb = seq_len // rows
    nh = GLA_HEADS
    dk = LANES
    dv = 2 * LANES
    tok = lambda b, n: b * nb + n
    return pl.pallas_call(
        functools.partial(_gla_kernel, chunk=GLA_CHUNK),
        grid=(n_seq, nb),
        in_specs=[
            pl.BlockSpec((nh, rows, LANES), lambda b, n: (slab_q // nh, tok(b, n), 0)),
            pl.BlockSpec((nh, rows, LANES), lambda b, n: (slab_k // nh, tok(b, n), 0)),
            pl.BlockSpec((2 * nh, rows, LANES), lambda b, n: (slab_v // (2 * nh), tok(b, n), 0)),
            pl.BlockSpec((2 * nh, rows, LANES), lambda b, n: (slab_gz // (2 * nh), tok(b, n), 0)),
            pl.BlockSpec((rows, LANES), lambda b, n: (tok(b, n), 0)),
            pl.BlockSpec((LANES, nh * dk), lambda b, n: (0, 0)),
            pl.BlockSpec((1, nh * dk), lambda b, n: (0, 0)),
            pl.BlockSpec((1, dv), lambda b, n: (0, 0)),
        ],
        out_specs=pl.BlockSpec((2 * nh, rows, LANES), lambda b, n: (0, tok(b, n), 0)),
        out_shape=jax.ShapeDtypeStruct((2 * nh, m, LANES), BF16),
        scratch_shapes=[pltpu.VMEM((nh, dv, dk), F32)],
        compiler_params=pltpu.CompilerParams(
            dimension_semantics=("parallel", "arbitrary"),
            vmem_limit_bytes=VMEM_LIMIT),
        name="gla",
    )(proj, proj, proj, proj, g_low, gate_up_pad, gate_bias, norm_w)


def _out_proj_kernel(g_ref, z_ref, yg_ref, x_ref, gw_hbm, gb_ref, wo_hbm, pw_ref,
                     o_ref, gw_ref, wo_ref, stage_ref, sem, *, layer):
    n_s5 = g_ref.shape[0]
    d_s5 = n_s5 * LANES
    tm = x_ref.shape[0]
    rc = OUT_PROJ_ROW_CHUNK

    @pl.when(pl.program_id(0) == 0)
    def _():
        _stream_cast_rows(wo_hbm.at[layer], wo_ref, stage_ref, sem)
        _stream_cast_rows(gw_hbm.at[layer], gw_ref, stage_ref, sem)

    for c in range(tm // rc):
        rows = slice(c * rc, (c + 1) * rc)
        g = jnp.concatenate([g_ref[s, rows, :] for s in range(n_s5)], axis=1)
        gate = jax.nn.sigmoid(jnp.dot(g, gw_ref[...], preferred_element_type=F32) + gb_ref[...])
        z = jnp.concatenate([z_ref[s, rows, :] for s in range(n_s5)], axis=1).astype(F32)
        y_s5 = (g.astype(F32) * gate * (z * jax.nn.sigmoid(z))).astype(BF16)
        y_gla = jnp.concatenate([yg_ref[s, rows, :] for s in range(yg_ref.shape[0])], axis=1)
        mixed = (jnp.dot(y_s5, wo_ref[:d_s5, :], preferred_element_type=F32)
                 + jnp.dot(y_gla, wo_ref[d_s5:, :], preferred_element_type=F32))
        ms = jnp.mean(mixed * mixed, axis=-1, keepdims=True)
        o_ref[rows, :] = x_ref[rows, :] + mixed * lax.rsqrt(ms + EPS) * pw_ref[...]


def _out_proj(g_slabs, proj, y_gla, x2d, glu_w, glu_b, w_out, post_w, layer, *, tm, slab_z):
    m, d = x2d.shape
    n_s5 = g_slabs.shape[0]
    n_gla = y_gla.shape[0]
    d_s5 = n_s5 * LANES
    d_mix = w_out.shape[1]
    assert glu_w.dtype == w_out.dtype and glu_w.shape[2] <= d
    return pl.pallas_call(
        functools.partial(_out_proj_kernel, layer=layer),
        grid=(m // tm,),
        in_specs=[
            pl.BlockSpec((n_s5, tm, LANES), lambda i: (0, i, 0)),
            pl.BlockSpec((n_s5, tm, LANES), lambda i: (slab_z // n_s5, i, 0)),
            pl.BlockSpec((n_gla, tm, LANES), lambda i: (0, i, 0)),
            pl.BlockSpec((tm, d), lambda i: (i, 0)),
            pl.BlockSpec(memory_space=pl.ANY),
            pl.BlockSpec((1, d_s5), lambda i: (0, 0)),
            pl.BlockSpec(memory_space=pl.ANY),
            pl.BlockSpec((1, d), lambda i: (0, 0)),
        ],
        out_specs=pl.BlockSpec((tm, d), lambda i: (i, 0)),
        out_shape=jax.ShapeDtypeStruct((m, d), F32),
        scratch_shapes=[
            pltpu.VMEM((d_s5, d_s5), BF16),
            pltpu.VMEM((d_mix, d), BF16),
            pltpu.VMEM((2, OUT_PROJ_W_CHUNK, d), w_out.dtype),
            pltpu.SemaphoreType.DMA((2,)),
        ],
        compiler_params=pltpu.CompilerParams(
            dimension_semantics=("arbitrary",),
            vmem_limit_bytes=VMEM_LIMIT),
        name="out_proj",
    )(g_slabs, proj, y_gla, x2d, glu_w, glu_b, w_out, post_w)


def kernel(x, pre_norm_w, w_in, s5_A_re, s5_A_im, s5_B_re, s5_B_im, s5_C_re, s5_C_im, s5_D,
           s5_log_dt, s5_glu_w, s5_glu_b, gla_gate_up, gla_gate_bias, gla_norm_w, w_out,
           post_norm_w):
    bsz, seq_len, d_model = x.shape
    depth = w_in.shape[0]
    d_in = w_in.shape[2]
    d_s5 = s5_glu_w.shape[1]
    rank, d_gk = gla_gate_up.shape[1:]
    d_gv = GLA_HEADS * gla_norm_w.shape[1]
    m = bsz * seq_len
    d_main = 2 * d_s5 + 2 * d_gk + 2 * d_gv
    tn = d_s5
    assert d_s5 % LANES == 0 and d_gk == GLA_HEADS * LANES and d_gv == GLA_HEADS * 2 * LANES
    assert d_in == d_main + rank and rank <= LANES and d_main % tn == 0 and 2 * d_gk == tn
    assert seq_len % (S5_T * 8) == 0 and w_out.shape[1] == 2 * d_s5

    n_s5 = d_s5 // LANES
    slab_z = 0
    slab_q = n_s5
    slab_k = slab_q + d_gk // LANES
    slab_v = slab_k + d_gk // LANES
    slab_gz = slab_v + d_gv // LANES

    resid = x.astype(F32).reshape(m, d_model)
    for l in range(depth):
        kb, u_op, v_op, abar = _s5_ops(s5_A_re[l], s5_A_im[l], s5_B_re[l], s5_B_im[l],
                                       s5_C_re[l], s5_C_im[l], s5_log_dt[l])
        dtile = jnp.tile(s5_D[l].astype(F32).reshape(n_s5, 2, 1, LANES // 2), (1, 1, 1, S5_T))

        u2, proj, g_low = _in_proj(resid, pre_norm_w[l].astype(F32)[None],
                                   jnp.swapaxes(w_in, 1, 2), l, d_main=d_main, tm=512, tn=tn)

        g_slabs = _s5(u2, kb, u_op, v_op, abar, dtile, n_seq=bsz)

        gate_up_pad = jnp.pad(gla_gate_up[l].astype(BF16), ((0, LANES - rank), (0, 0)))
        y_gla = _gla(proj, g_low, gate_up_pad, gla_gate_bias[l].astype(F32)[None],
                     gla_norm_w[l].astype(F32)[None], n_seq=bsz, seq_len=seq_len, rows=512,
                     slab_q=slab_q, slab_k=slab_k, slab_v=slab_v, slab_gz=slab_gz)

        resid = _out_proj(g_slabs, proj, y_gla, resid, s5_glu_w, s5_glu_b[l].astype(F32)[None],
                          w_out, post_norm_w[l].astype(F32)[None], l, tm=512, slab_z=slab_z)
    return resid.reshape(bsz, seq_len, d_model).astype(x.dtype)
```

```python
import functools

import jax
import jax.numpy as jnp
from jax import lax
from jax.experimental import pallas as pl
from jax.experimental.pallas import tpu as pltpu

F32 = jnp.float32
BF16 = jnp.bfloat16

S5_GROUP = 16
GLA_HEADS = 4
GLA_TAU = 16.0
GLA_CHUNK = 64
EPS = 1e-6

LANES = 128
S5_T = 16
S5_UNFOLD_PITCH = 24
OUT_PROJ_ROW_CHUNK = 256
IN_PROJ_W_CHUNK = 256
OUT_PROJ_W_CHUNK = 256
VMEM_LIMIT = 56 * 1024 * 1024


NT_DIMS = (((1,), (1,)), ((), ()))
TN_DIMS = (((0,), (0,)), ((), ()))


def _cmul(ar, ai, br, bi):
    return ar * br - ai * bi, ar * bi + ai * br


def _stream_cast_rows(src, dst_ref, stage_ref, sem):
    n_rows, width = src.shape
    chunk = stage_ref.shape[1]
    sizes = [chunk] * (n_rows // chunk) + ([n_rows % chunk] if n_rows % chunk else [])

    def copy(c):
        return pltpu.make_async_copy(
            src.at[pl.ds(c * chunk, sizes[c]), :],
            stage_ref.at[c % 2, pl.ds(0, sizes[c]), pl.ds(0, width)], sem.at[c % 2])

    for c in range(min(2, len(sizes))):
        copy(c).start()
    for c, rows in enumerate(sizes):
        copy(c).wait()
        dst_ref[c * chunk:c * chunk + rows, :width] = (
            stage_ref[c % 2, :rows, :width].astype(dst_ref.dtype))
        if c + 2 < len(sizes):
            copy(c + 2).start()


def _s5_ops_kernel(*refs, n_state):
    tin = [r[...] for r in refs[:5]]
    nin = [r[...] for r in refs[5:10]]
    kb_ref, u_ref, v_ref, ab_ref = refs[10:]
    nt = S5_T
    ns = nt // 2
    half = LANES // 2

    def a_bar(are_raw, aim, ldt):
        are = jnp.minimum(are_raw, -1e-4)
        dt = jnp.exp(ldt)
        mag = jnp.exp(are * dt)
        return are, mag * jnp.cos(aim * dt), mag * jnp.sin(aim * dt)

    are, abr, abi = a_bar(tin[0], tin[1], tin[2])
    aim = tin[1]
    den = are * are + aim * aim
    nr = abr - 1.0
    fr = (nr * are + abi * aim) / den
    fi = (abi * are - nr * aim) / den
    xr, xi = _cmul(fr, fi, tin[3], tin[4])

    row = lax.broadcasted_iota(jnp.int32, (LANES, LANES), 0)
    lane = lax.broadcasted_iota(jnp.int32, (LANES, LANES), 1)
    own_half = ((row // S5_GROUP) % 2) == (lane // half)
    same_group = ((row % half) // S5_GROUP) == ((lane % half) // S5_GROUP)
    first_half = lane < half

    def split(t):
        hi = t.astype(BF16)
        return hi, (t - hi.astype(F32)).astype(BF16)

    c_stack = jnp.concatenate([nin[3], nin[4]], axis=0)
    c_swap = pltpu.roll(c_stack, half, axis=1)
    c_hi, c_lo = split(jnp.concatenate([c_stack, c_swap], axis=1))
    c_terms = jnp.concatenate([c_hi, c_lo, c_hi], axis=0)
    kers, kers_swap, xs = [], [], []
    for tau in range(nt):
        l_hi, l_lo = split(jnp.where(first_half, xr, -xi))
        ker = jnp.dot(jnp.concatenate([l_hi, l_hi, l_lo], axis=1), c_terms,
                      preferred_element_type=F32)
        kers.append(ker[:, :LANES])
        kers_swap.append(ker[:, LANES:])
        xs.append((jnp.where(own_half, xr, 0.0).astype(BF16),
                   jnp.where(own_half, xi, 0.0).astype(BF16)))
        if tau + 1 < nt:
            xr, xi = _cmul(xr, xi, abr, abi)

    zero = jnp.zeros((half, LANES), F32)
    first_half_rows = lax.broadcasted_iota(jnp.int32, (half, LANES), 1) < half
    for hh in range(2):
        rows = slice(hh * half, (hh + 1) * half)
        low, high = (kers, kers_swap) if hh == 0 else (kers_swap, kers)
        pick = lambda src, tau: src[tau][rows, :] if tau >= 0 else zero
        for d in range(ns):
            top = jnp.where(first_half_rows, pick(low, 2 * d), pick(high, 2 * d + 1))
            bot = jnp.where(first_half_rows, pick(low, 2 * d - 1), pick(high, 2 * d))
            blk = jnp.concatenate([top, bot], axis=0)
            kb_ref[hh, d] = jnp.where(same_group, blk, 0.0).astype(BF16)
        for s2 in range(ns):
            for part in range(2):
                u_ref[hh, s2, part] = jnp.concatenate(
                    [xs[nt - 1 - 2 * s2][part][rows, :], xs[nt - 2 - 2 * s2][part][rows, :]], axis=0)

    pr, pi = abr, abi
    for _ in range(4):
        pr, pi = _cmul(pr, pi, pr, pi)
    assert nt == 16
    pairs = LANES // (2 * S5_GROUP)
    for part, val in enumerate((pr, pi)):
        for a in range(pairs):
            r0 = 2 * a * S5_GROUP
            piece = jnp.where(first_half[:1], val[r0:r0 + 1], val[r0 + S5_GROUP:r0 + S5_GROUP + 1])
            c0 = part * (n_state // 2) + a * LANES
            ab_ref[:, c0:c0 + LANES] = piece

    _, nbr, nbi = a_bar(nin[0], nin[1], nin[2])
    cr, ci = nin[3], nin[4]
    lane_n = lax.broadcasted_iota(jnp.int32, (half, LANES), 1)
    parity = (lane_n // S5_GROUP) % 2
    first_n = lane_n < half
    pr, pi = nbr, nbi
    outs = []
    for t in range(nt):
        wr, wi = _cmul(cr, ci, pr, pi)
        outs.append((wr, -wi))
        if t + 1 < nt:
            pr, pi = _cmul(pr, pi, nbr, nbi)
    for t2 in range(ns):
        for part in range(2):
            even, odd = outs[2 * t2][part], outs[2 * t2 + 1][part]
            packed = (jnp.where(first_n, even, pltpu.roll(odd, half, axis=1)),
                      jnp.where(first_n, pltpu.roll(even, half, axis=1), odd))
            for hh in range(2):
                for q in range(2):
                    v_ref[hh, t2, part, q * half:(q + 1) * half, :] = (
                        jnp.where(parity == q, packed[hh], 0.0).astype(BF16))


def _s5_ops(a_re, a_im, b_re, b_im, c_re, c_im, log_dt):
    g, p = a_re.shape
    h = S5_GROUP
    gps = LANES // h
    n_j = g // gps
    assert 2 * p == LANES and g % gps == 0
    n_state = 2 * gps * p
    f = lambda t: t.astype(F32)

    def t_arr(t):
        t = f(t).reshape(n_j, gps, -1, 1, p)
        return jnp.broadcast_to(t, (n_j, gps, h, 2, p)).reshape(n_j, LANES, LANES)

    def n_arr(t):
        t = jnp.swapaxes(f(t).reshape(n_j, gps, -1, p), 1, 3)
        t = jnp.swapaxes(jnp.broadcast_to(t, (n_j, p, h, gps)), 2, 3)
        return t.reshape(n_j, p, LANES)

    ldt = jnp.broadcast_to(f(log_dt)[:, None], (g, p))
    tpar = [t_arr(a_re), t_arr(a_im), t_arr(ldt),
            t_arr(jnp.swapaxes(b_re, 1, 2)), t_arr(jnp.swapaxes(b_im, 1, 2))]
    npar = [n_arr(a_re), n_arr(a_im), n_arr(ldt), n_arr(c_re), n_arr(c_im)]
    ns = S5_T // 2
    return pl.pallas_call(
        functools.partial(_s5_ops_kernel, n_state=n_state),
        grid=(n_j,),
        in_specs=([pl.BlockSpec((None, LANES, LANES), lambda j: (j, 0, 0))] * 5
                  + [pl.BlockSpec((None, p, LANES), lambda j: (j, 0, 0))] * 5),
        out_specs=[
            pl.BlockSpec((None, 2, ns, LANES, LANES), lambda j: (j, 0, 0, 0, 0)),
            pl.BlockSpec((None, 2, ns, 2, LANES, LANES), lambda j: (j, 0, 0, 0, 0, 0)),
            pl.BlockSpec((None, 2, ns, 2, LANES, LANES), lambda j: (j, 0, 0, 0, 0, 0)),
            pl.BlockSpec((None, 1, n_state), lambda j: (j, 0, 0)),
        ],
        out_shape=[
            jax.ShapeDtypeStruct((n_j, 2, ns, LANES, LANES), BF16),
            jax.ShapeDtypeStruct((n_j, 2, ns, 2, LANES, LANES), BF16),
            jax.ShapeDtypeStruct((n_j, 2, ns, 2, LANES, LANES), BF16),
            jax.ShapeDtypeStruct((n_j, 1, n_state), F32),
        ],
        compiler_params=pltpu.CompilerParams(dimension_semantics=("parallel",)),
        name="s5_ops",
    )(*tpar, *npar)


def _in_proj_kernel(x_ref, nw_ref, w_hbm, u_ref, p_ref, l_ref, h_ref, s_ref, w_ref, stage_ref, sem,
                    *, layer, d_main, tn):
    tm = x_ref.shape[0]
    n_slabs = tn // LANES
    fold = tm // S5_T

    @pl.when(pl.program_id(0) == 0)
    def _():
        n_rows = w_hbm.shape[1]
        w_ref[n_rows:, :] = jnp.zeros((w_ref.shape[0] - n_rows, w_ref.shape[1]), BF16)
        _stream_cast_rows(w_hbm.at[layer], w_ref, stage_ref, sem)

    xf = x_ref[...]
    h_ref[...] = (xf * nw_ref[...]).astype(BF16)
    rs = lax.rsqrt(jnp.mean(xf * xf, axis=-1, keepdims=True) + EPS)

    for n in range(d_main // tn):
        res = lax.dot_general(h_ref[...], w_ref[n * tn:(n + 1) * tn, :], NT_DIMS,
                              preferred_element_type=F32) * rs
        if n == 0:
            for s in range(n_slabs):
                s_ref[s] = res[:, s * LANES:(s + 1) * LANES]
            low = lax.broadcasted_iota(jnp.int32, (fold, LANES), 1) < LANES // 2
            for s in range(n_slabs):
                for t2 in range(S5_T // 2):
                    even = s_ref[s, pl.ds(2 * t2, fold, stride=S5_T), :]
                    odd = s_ref[s, pl.ds(2 * t2 + 1, fold, stride=S5_T), :]
                    cols = slice(t2 * LANES, (t2 + 1) * LANES)
                    u_ref[2 * s, :, cols] = (
                        jnp.where(low, even, pltpu.roll(odd, LANES // 2, axis=1)).astype(BF16))
                    u_ref[2 * s + 1, :, cols] = (
                        jnp.where(low, pltpu.roll(even, LANES // 2, axis=1), odd).astype(BF16))
        else:
            for s in range(n_slabs):
                p_ref[(n - 1) * n_slabs + s] = res[:, s * LANES:(s + 1) * LANES].astype(BF16)
    low = lax.dot_general(h_ref[...], w_ref[d_main:d_main + LANES, :], NT_DIMS,
                          preferred_element_type=F32) * rs
    l_ref[...] = low.astype(BF16)


def _in_proj(x2d, norm_w, w_t, layer, *, d_main, tm, tn):
    m, d = x2d.shape
    n_slabs = tn // LANES
    n_proj = (d_main // tn - 1) * n_slabs
    assert d_main < w_t.shape[1] <= d_main + LANES
    return pl.pallas_call(
        functools.partial(_in_proj_kernel, layer=layer, d_main=d_main, tn=tn),
        grid=(m // tm,),
        in_specs=[
            pl.BlockSpec((tm, d), lambda i: (i, 0)),
            pl.BlockSpec((1, d), lambda i: (0, 0)),
            pl.BlockSpec(memory_space=pl.ANY),
        ],
        out_specs=[
            pl.BlockSpec((2 * n_slabs, tm // S5_T, S5_T * LANES // 2), lambda i: (0, i, 0)),
            pl.BlockSpec((n_proj, tm, LANES), lambda i: (0, i, 0)),
            pl.BlockSpec((tm, LANES), lambda i: (i, 0)),
        ],
        out_shape=[
            jax.ShapeDtypeStruct((2 * n_slabs, m // S5_T, S5_T * LANES // 2), BF16),
            jax.ShapeDtypeStruct((n_proj, m, LANES), BF16),
            jax.ShapeDtypeStruct((m, LANES), BF16),
        ],
        scratch_shapes=[
            pltpu.VMEM((tm, d), BF16),
            pltpu.VMEM((n_slabs, tm, LANES), F32),
            pltpu.VMEM((d_main + LANES, d), BF16),
            pltpu.VMEM((2, IN_PROJ_W_CHUNK, d), w_t.dtype),
            pltpu.SemaphoreType.DMA((2,)),
        ],
        compiler_params=pltpu.CompilerParams(
            dimension_semantics=("arbitrary",),
            vmem_limit_bytes=VMEM_LIMIT),
        name="in_proj",
    )(x2d, norm_w, w_t)


def _s5_kernel(x_ref, kb_ref, u_ref, v_ref, a_ref, d_ref, o_ref,
               m_ref, wz_ref, wy_ref, z_ref, sp_ref, y_ref, *, rows_per_seq, n_seq):
    ns = S5_T // 2
    rows = x_ref.shape[1]
    half = a_ref.shape[-1] // 2
    hstate = half // 2
    pairs = hstate // LANES
    sub = LANES // (2 * pairs)
    wide = 2 * LANES

    @pl.when(pl.program_id(0) == 0)
    def _():
        m_ref[:, (ns - 1) * LANES:, :LANES] = jnp.zeros((2, LANES, LANES), BF16)
        wz_ref[...] = jnp.zeros_like(wz_ref)
        y_ref[...] = jnp.zeros_like(y_ref)

    lane = lax.broadcasted_iota(jnp.int32, (LANES, LANES), 1)
    for hh in range(2):
        for s in range(ns):
            for k in range(2):
                lag = ns - 2 + k - s
                if lag >= 0:
                    m_ref[hh, s * LANES:(s + 1) * LANES, k * LANES:(k + 1) * LANES] = kb_ref[hh, lag]
        for s in range(ns):
            for part in range(2):
                for rho in range(2):
                    for a in range(pairs):
                        r0 = rho * (LANES // 2) + a * sub
                        c0 = part * hstate + a * LANES
                        wz_ref[hh, s * LANES + r0:s * LANES + r0 + sub, c0:c0 + LANES] = (
                            u_ref[hh, s, part, r0:r0 + sub, :])
        for t in range(ns):
            for part in range(2):
                blk = v_ref[hh, t, part]
                for a in range(pairs):
                    r0 = part * hstate + a * LANES
                    wy_ref[hh, r0:r0 + LANES, t * LANES:(t + 1) * LANES] = (
                        jnp.where(((lane % (LANES // 2)) // sub) == a, blk, jnp.zeros_like(blk)))
        zh = jnp.dot(x_ref[hh], wz_ref[hh], preferred_element_type=F32)
        z_ref[:, hh * hstate:(hh + 1) * hstate] = zh[:, :hstate]
        z_ref[:, half + hh * hstate:half + (hh + 1) * hstate] = zh[:, hstate:]

    a_re = a_ref[:, :half]
    a_im = a_ref[:, half:]

    state = [(jnp.zeros((1, half), F32), jnp.zeros((1, half), F32)) for _ in range(n_seq)]
    for c in range(rows_per_seq):
        for b in range(n_seq):
            s_re, s_im = state[b]
            r = b * rows_per_seq + c
            sp_ref[r:r + 1, :half] = s_re
            sp_ref[r:r + 1, half:] = s_im
            z_re = z_ref[r:r + 1, :half]
            z_im = z_ref[r:r + 1, half:]
            state[b] = (a_re * s_re - a_im * s_im + z_re, a_re * s_im + a_im * s_re + z_im)

    spb = [jnp.concatenate([sp_ref[:, hh * hstate:(hh + 1) * hstate],
                            sp_ref[:, half + hh * hstate:half + (hh + 1) * hstate]],
                           axis=1).astype(BF16) for hh in range(2)]
    pitch = y_ref.shape[0] // rows
    low = lax.broadcasted_iota(jnp.int32, (rows, LANES), 1) < LANES // 2
    for n in range(ns // 2):
        cols = slice(n * wide, (n + 1) * wide)
        kk = (n + 1) * wide
        ys = []
        for hh in range(2):
            acc = jnp.dot(x_ref[hh, :, :kk], m_ref[hh, ns * LANES - kk:, :],
                          preferred_element_type=F32)
            acc = acc + jnp.dot(spb[hh], wy_ref[hh, :, cols], preferred_element_type=F32)
            acc = acc + d_ref[hh, :, cols] * x_ref[hh, :, cols].astype(F32)
            ys.append(jax.nn.gelu(acc))
        for k in range(2):
            y0 = ys[0][:, k * LANES:(k + 1) * LANES]
            y1 = ys[1][:, k * LANES:(k + 1) * LANES]
            t = 2 * (2 * n + k)
            y_ref[pl.ds(t, rows, stride=pitch), :] = (
                jnp.where(low, y0, pltpu.roll(y1, LANES // 2, axis=1)))
            y_ref[pl.ds(t + 1, rows, stride=pitch), :] = (
                jnp.where(low, pltpu.roll(y0, LANES // 2, axis=1), y1))
    tokens = y_ref[...].reshape(rows, pitch, LANES)[:, :S5_T, :].reshape(rows * S5_T, LANES)
    o_ref[...] = tokens.astype(BF16)


def _s5(u2, kb, u_op, v_op, abar, dtile, *, n_seq):
    n_j, _, ns, _, _ = kb.shape
    rows = u2.shape[1]
    width = ns * LANES
    n_state = abar.shape[-1]
    return pl.pallas_call(
        functools.partial(_s5_kernel, rows_per_seq=rows // n_seq, n_seq=n_seq),
        grid=(n_j,),
        in_specs=[
            pl.BlockSpec((2, rows, width), lambda j: (j, 0, 0)),
            pl.BlockSpec((None, 2, ns, LANES, LANES), lambda j: (j, 0, 0, 0, 0)),
            pl.BlockSpec((None, 2, ns, 2, LANES, LANES), lambda j: (j, 0, 0, 0, 0, 0)),
            pl.BlockSpec((None, 2, ns, 2, LANES, LANES), lambda j: (j, 0, 0, 0, 0, 0)),
            pl.BlockSpec((None, 1, n_state), lambda j: (j, 0, 0)),
            pl.BlockSpec((None, 2, 1, width), lambda j: (j, 0, 0, 0)),
        ],
        out_specs=pl.BlockSpec((None, rows * S5_T, LANES), lambda j: (j, 0, 0)),
        out_shape=jax.ShapeDtypeStruct((n_j, rows * S5_T, LANES), BF16),
        scratch_shapes=[
            pltpu.VMEM((2, width, 2 * LANES), BF16),
            pltpu.VMEM((2, width, n_state // 2), BF16),
            pltpu.VMEM((2, n_state // 2, width), BF16),
            pltpu.VMEM((rows, n_state), F32),
            pltpu.VMEM((rows, n_state), F32),
            pltpu.VMEM((rows * S5_UNFOLD_PITCH, LANES), F32),
        ],
        compiler_params=pltpu.CompilerParams(
            dimension_semantics=("arbitrary",),
            vmem_limit_bytes=VMEM_LIMIT),
        name="s5",
    )(u2, kb, u_op, v_op, abar, dtile)


def _log_sigmoid(x):
    return jnp.minimum(x, 0.0) - jnp.log(1.0 + jnp.exp(-jnp.abs(x)))


def _gla_kernel(q_ref, k_ref, v_ref, gz_ref, gl_ref, gu_ref, gb_ref, nw_ref, o_ref, st_ref,
                *, chunk):
    n_heads, rows, dk = q_ref.shape
    n_chunks = rows // chunk
    d_gk = n_heads * dk

    @pl.when(pl.program_id(1) == 0)
    def _():
        st_ref[...] = jnp.zeros_like(st_ref)

    zg = jnp.dot(gl_ref[...], gu_ref[...], preferred_element_type=F32) + gb_ref[...]
    lg = _log_sigmoid(zg) * (1.0 / GLA_TAU)

    hi = lg.astype(BF16)
    lo = (lg - hi.astype(F32)).astype(BF16)
    r_id = lax.broadcasted_iota(jnp.int32, (chunk, chunk), 0)
    c_id = lax.broadcasted_iota(jnp.int32, (chunk, chunk), 1)
    causal = c_id <= r_id
    tri = jnp.where(causal, 1.0, 0.0).astype(BF16)
    tri2 = jnp.concatenate([tri, tri], axis=1)
    b_parts, last_parts = [], []
    for n in range(n_chunks):
        rs = slice(n * chunk, (n + 1) * chunk)
        b_n = jnp.dot(tri2, jnp.concatenate([hi[rs], lo[rs]], axis=0), preferred_element_type=F32)
        b_parts.append(b_n)
        last_parts.append(jnp.broadcast_to(b_n[chunk - 1:chunk], (chunk, d_gk)))
    b = jnp.concatenate(b_parts, axis=0)
    e_pos = jnp.exp(b)
    e_neg = jnp.exp(-b)
    decay = jnp.exp(jnp.concatenate(last_parts, axis=0))

    for h in range(n_heads):
        hs = slice(h * dk, (h + 1) * dk)
        q_e = (q_ref[h].astype(F32) * ((dk ** -0.5) * e_pos[:, hs])).astype(BF16)
        k_f = k_ref[h].astype(F32) * e_neg[:, hs]
        k_e = k_f.astype(BF16)
        k_t = (k_f * decay[:, hs]).astype(BF16)
        v = jnp.concatenate([v_ref[2 * h], v_ref[2 * h + 1]], axis=1)

        chunks = [slice(n * chunk, (n + 1) * chunk) for n in range(n_chunks)]
        attn = [lax.dot_general(q_e[rs], k_e[rs], NT_DIMS, preferred_element_type=F32)
                for rs in chunks]
        kv_t = [lax.dot_general(v[rs], k_t[rs], TN_DIMS, preferred_element_type=F32)
                for rs in chunks]
        attn = [jnp.where(causal, a, 0.0).astype(BF16) for a in attn]
        st = st_ref[h]
        st_in = []
        for n in range(n_chunks):
            st_in.append(st.astype(BF16))
            st = decay[n * chunk:n * chunk + 1, hs] * st + kv_t[n]
        st_ref[h] = st
        outs = [jnp.dot(attn[n], v[rs], preferred_element_type=F32)
                + lax.dot_general(q_e[rs], st_in[n], NT_DIMS, preferred_element_type=F32)
                for n, rs in enumerate(chunks)]

        o = jnp.concatenate(outs, axis=0)
        ms = jnp.mean(o * o, axis=-1, keepdims=True)
        o = o * lax.rsqrt(ms + EPS) * nw_ref[...]
        gz = jnp.concatenate([gz_ref[2 * h], gz_ref[2 * h + 1]], axis=1).astype(F32)
        y = (o * (gz * jax.nn.sigmoid(gz))).astype(BF16)
        o_ref[2 * h] = y[:, :LANES]
        o_ref[2 * h + 1] = y[:, LANES:]


def _gla(proj, g_low, gate_up_pad, gate_bias, norm_w, *, n_seq, seq_len, rows, slab_q, slab_k,
         slab_v, slab_gz):
    m = proj.shape[1]
    nb = seq_len // rows
    nh = GLA_HEADS
    dk = LANES
    dv = 2 * LANES
    tok = lambda b, n: b * nb + n
    return pl.pallas_call(
        functools.partial(_gla_kernel, chunk=GLA_CHUNK),
        grid=(n_seq, nb),
        in_specs=[
            pl.BlockSpec((nh, rows, LANES), lambda b, n: (slab_q // nh, tok(b, n), 0)),
            pl.BlockSpec((nh, rows, LANES), lambda b, n: (slab_k // nh, tok(b, n), 0)),
            pl.BlockSpec((2 * nh, rows, LANES), lambda b, n: (slab_v // (2 * nh), tok(b, n), 0)),
            pl.BlockSpec((2 * nh, rows, LANES), lambda b, n: (slab_gz // (2 * nh), tok(b, n), 0)),
            pl.BlockSpec((rows, LANES), lambda b, n: (tok(b, n), 0)),
            pl.BlockSpec((LANES, nh * dk), lambda b, n: (0, 0)),
            pl.BlockSpec((1, nh * dk), lambda b, n: (0, 0)),
            pl.BlockSpec((1, dv), lambda b, n: (0, 0)),
        ],
        out_specs=pl.BlockSpec((2 * nh, rows, LANES), lambda b, n: (0, tok(b, n), 0)),
        out_shape=jax.ShapeDtypeStruct((2 * nh, m, LANES), BF16),
        scratch_shapes=[pltpu.VMEM((nh, dv, dk), F32)],
        compiler_params=pltpu.CompilerParams(
            dimension_semantics=("parallel", "arbitrary"),
            vmem_limit_bytes=VMEM_LIMIT),
        name="gla",
    )(proj, proj, proj, proj, g_low, gate_up_pad, gate_bias, norm_w)


def _out_proj_kernel(g_ref, z_ref, yg_ref, x_ref, gw_hbm, gb_ref, wo_hbm, pw_ref,
                     o_ref, gw_ref, wo_ref, stage_ref, sem, *, layer):
    n_s5 = g_ref.shape[0]
    d_s5 = n_s5 * LANES
    tm = x_ref.shape[0]
    rc = OUT_PROJ_ROW_CHUNK

    @pl.when(pl.program_id(0) == 0)
    def _():
        _stream_cast_rows(wo_hbm.at[layer], wo_ref, stage_ref, sem)
        _stream_cast_rows(gw_hbm.at[layer], gw_ref, stage_ref, sem)

    for c in range(tm // rc):
        rows = slice(c * rc, (c + 1) * rc)
        g = jnp.concatenate([g_ref[s, rows, :] for s in range(n_s5)], axis=1)
        gate = jax.nn.sigmoid(jnp.dot(g, gw_ref[...], preferred_element_type=F32) + gb_ref[...])
        z = jnp.concatenate([z_ref[s, rows, :] for s in range(n_s5)], axis=1).astype(F32)
        y_s5 = (g.astype(F32) * gate * (z * jax.nn.sigmoid(z))).astype(BF16)
        y_gla = jnp.concatenate([yg_ref[s, rows, :] for s in range(yg_ref.shape[0])], axis=1)
        mixed = (jnp.dot(y_s5, wo_ref[:d_s5, :], preferred_element_type=F32)
                 + jnp.dot(y_gla, wo_ref[d_s5:, :], preferred_element_type=F32))
        ms = jnp.mean(mixed * mixed, axis=-1, keepdims=True)
        o_ref[rows, :] = x_ref[rows, :] + mixed * lax.rsqrt(ms + EPS) * pw_ref[...]


def _out_proj(g_slabs, proj, y_gla, x2d, glu_w, glu_b, w_out, post_w, layer, *, tm, slab_z):
    m, d = x2d.shape
    n_s5 = g_slabs.shape[0]
    n_gla = y_gla.shape[0]
    d_s5 = n_s5 * LANES
    d_mix = w_out.shape[1]
    assert glu_w.dtype == w_out.dtype and glu_w.shape[2] <= d
    return pl.pallas_call(
        functools.partial(_out_proj_kernel, layer=layer),
        grid=(m // tm,),
        in_specs=[
            pl.BlockSpec((n_s5, tm, LANES), lambda i: (0, i, 0)),
            pl.BlockSpec((n_s5, tm, LANES), lambda i: (slab_z // n_s5, i, 0)),
            pl.BlockSpec((n_gla, tm, LANES), lambda i: (0, i, 0)),
            pl.BlockSpec((tm, d), lambda i: (i, 0)),
            pl.BlockSpec(memory_space=pl.ANY),
            pl.BlockSpec((1, d_s5), lambda i: (0, 0)),
            pl.BlockSpec(memory_space=pl.ANY),
            pl.BlockSpec((1, d), lambda i: (0, 0)),
        ],
        out_specs=pl.BlockSpec((tm, d), lambda i: (i, 0)),
        out_shape=jax.ShapeDtypeStruct((m, d), F32),
        scratch_shapes=[
            pltpu.VMEM((d_s5, d_s5), BF16),
            pltpu.VMEM((d_mix, d), BF16),
            pltpu.VMEM((2, OUT_PROJ_W_CHUNK, d), w_out.dtype),
            pltpu.SemaphoreType.DMA((2,)),
        ],
        compiler_params=pltpu.CompilerParams(
            dimension_semantics=("arbitrary",),
            vmem_limit_bytes=VMEM_LIMIT),
        name="out_proj",
    )(g_slabs, proj, y_gla, x2d, glu_w, glu_b, w_out, post_w)


def kernel(x, pre_norm_w, w_in, s5_A_re, s5_A_im, s5_B_re, s5_B_im, s5_C_re, s5_C_im, s5_D,
           s5_log_dt, s5_glu_w, s5_glu_b, gla_gate_up, gla_gate_bias, gla_norm_w, w_out,
           post_norm_w):
    bsz, seq_len, d_model = x.shape
    depth = w_in.shape[0]
    d_in = w_in.shape[2]
    d_s5 = s5_glu_w.shape[1]
    rank, d_gk = gla_gate_up.shape[1:]
    d_gv = GLA_HEADS * gla_norm_w.shape[1]
    m = bsz * seq_len
    d_main = 2 * d_s5 + 2 * d_gk + 2 * d_gv
    tn = d_s5
    assert d_s5 % LANES == 0 and d_gk == GLA_HEADS * LANES and d_gv == GLA_HEADS * 2 * LANES
    assert d_in == d_main + rank and rank <= LANES and d_main % tn == 0 and 2 * d_gk == tn
    assert seq_len % (S5_T * 8) == 0 and w_out.shape[1] == 2 * d_s5

    n_s5 = d_s5 // LANES
    slab_z = 0
    slab_q = n_s5
    slab_k = slab_q + d_gk // LANES
    slab_v = slab_k + d_gk // LANES
    slab_gz = slab_v + d_gv // LANES

    resid = x.astype(F32).reshape(m, d_model)
    for l in range(depth):
        kb, u_op, v_op, abar = _s5_ops(s5_A_re[l], s5_A_im[l], s5_B_re[l], s5_B_im[l],
                                       s5_C_re[l], s5_C_im[l], s5_log_dt[l])
        dtile = jnp.tile(s5_D[l].astype(F32).reshape(n_s5, 2, 1, LANES // 2), (1, 1, 1, S5_T))

        u2, proj, g_low = _in_proj(resid, pre_norm_w[l].astype(F32)[None],
                                   jnp.swapaxes(w_in, 1, 2), l, d_main=d_main, tm=512, tn=tn)

        g_slabs = _s5(u2, kb, u_op, v_op, abar, dtile, n_seq=bsz)

        gate_up_pad = jnp.pad(gla_gate_up[l].astype(BF16), ((0, LANES - rank), (0, 0)))
        y_gla = _gla(proj, g_low, gate_up_pad, gla_gate_bias[l].astype(F32)[None],
                     gla_norm_w[l].astype(F32)[None], n_seq=bsz, seq_len=seq_len, rows=512,
                     slab_q=slab_q, slab_k=slab_k, slab_v=slab_v, slab_gz=slab_gz)

        resid = _out_proj(g_slabs, proj, y_gla, resid, s5_glu_w, s5_glu_b[l].astype(F32)[None],
                          w_out, post_norm_w[l].astype(F32)[None], l, tm=512, slab_z=slab_z)
    return resid.reshape(bsz, seq_len, d_model).astype(x.dtype)
```

```python
import functools

import jax
import jax.numpy as jnp
from jax import lax
from jax.experimental import pallas as pl
from jax.experimental.pallas import tpu as pltpu

F32 = jnp.float32
BF16 = jnp.bfloat16

S5_GROUP = 16
GLA_HEADS = 4
GLA_TAU = 16.0
GLA_CHUNK = 64
EPS = 1e-6

LANES = 128
S5_T = 16
S5_UNFOLD_PITCH = 24
OUT_PROJ_ROW_CHUNK = 256
IN_PROJ_W_CHUNK = 256
OUT_PROJ_W_CHUNK = 256
VMEM_LIMIT = 56 * 1024 * 1024


NT_DIMS = (((1,), (1,)), ((), ()))
TN_DIMS = (((0,), (0,)), ((), ()))


def _cmul(ar, ai, br, bi):
    return ar * br - ai * bi, ar * bi + ai * br


def _stream_cast_rows(src, dst_ref, stage_ref, sem):
    n_rows, width = src.shape
    chunk = stage_ref.shape[1]
    sizes = [chunk] * (n_rows // chunk) + ([n_rows % chunk] if n_rows % chunk else [])

    def copy(c):
        return pltpu.make_async_copy(
            src.at[pl.ds(c * chunk, sizes[c]), :],
            stage_ref.at[c % 2, pl.ds(0, sizes[c]), pl.ds(0, width)], sem.at[c % 2])

    for c in range(min(2, len(sizes))):
        copy(c).start()
    for c, rows in enumerate(sizes):
        copy(c).wait()
        dst_ref[c * chunk:c * chunk + rows, :width] = (
            stage_ref[c % 2, :rows, :width].astype(dst_ref.dtype))
        if c + 2 < len(sizes):
            copy(c + 2).start()


def _s5_ops_kernel(*refs, n_state):
    (a_re_ref, a_im_ref, ldt_ref, b_re_ref, b_im_ref, c_re_ref, c_im_ref,
     kb_ref, u_ref, v_ref, ab_ref) = refs
    nt = S5_T
    ns = nt // 2
    half = LANES // 2
    gps, p = a_re_ref.shape

    def terms(t):
        hi = t.astype(BF16).astype(F32)
        mid = (t - hi).astype(BF16).astype(F32)
        return hi, mid, t - hi - mid

    def select(t, onehot):
        return sum(lax.dot_general(x, onehot, TN_DIMS, preferred_element_type=F32)
                   for x in terms(t))

    def spread_rows(t):
        return sum(lax.dot_general(spread, x, TN_DIMS, preferred_element_type=F32)
                   for x in terms(t))

    group_of_lane = lax.broadcasted_iota(jnp.int32, (gps, LANES), 1) // S5_GROUP
    spread = jnp.where(group_of_lane == lax.broadcasted_iota(jnp.int32, (gps, LANES), 0), 1.0, 0.0)
    eye = jnp.where(lax.broadcasted_iota(jnp.int32, (LANES, LANES), 0)
                    == lax.broadcasted_iota(jnp.int32, (LANES, LANES), 1), 1.0, 0.0)
    dup = lambda t: jnp.concatenate([t, t], axis=1)
    ldt = jnp.broadcast_to(ldt_ref[...], (gps, p))
    per_group = (a_re_ref[...], a_im_ref[...], ldt)
    tin = [spread_rows(dup(t)) for t in per_group]
    tin += [dup(r[...].reshape(gps * S5_GROUP, p)) for r in (b_re_ref, b_im_ref)]
    nin = [select(t, spread) for t in per_group]
    nin += [select(r[...].reshape(gps * S5_GROUP, p), eye) for r in (c_re_ref, c_im_ref)]

    def a_bar(are_raw, aim, ldt):
        are = jnp.minimum(are_raw, -1e-4)
        dt = jnp.exp(ldt)
        mag = jnp.exp(are * dt)
        return are, mag * jnp.cos(aim * dt), mag * jnp.sin(aim * dt)

    are, abr, abi = a_bar(tin[0], tin[1], tin[2])
    aim = tin[1]
    den = are * are + aim * aim
    nr = abr - 1.0
    fr = (nr * are + abi * aim) / den
    fi = (abi * are - nr * aim) / den
    xr, xi = _cmul(fr, fi, tin[3], tin[4])

    row = lax.broadcasted_iota(jnp.int32, (LANES, LANES), 0)
    lane = lax.broadcasted_iota(jnp.int32, (LANES, LANES), 1)
    own_half = ((row // S5_GROUP) % 2) == (lane // half)
    same_group = ((row % half) // S5_GROUP) == ((lane % half) // S5_GROUP)
    first_half = lane < half

    def split(t):
        hi = t.astype(BF16)
        return hi, (t - hi.astype(F32)).astype(BF16)

    c_stack = jnp.concatenate([nin[3], nin[4]], axis=0)
    c_swap = pltpu.roll(c_stack, half, axis=1)
    c_hi, c_lo = split(jnp.concatenate([c_stack, c_swap], axis=1))
    c_terms = jnp.concatenate([c_hi, c_lo, c_hi], axis=0)
    kers, kers_swap, xs = [], [], []
    for tau in range(nt):
        l_hi, l_lo = split(jnp.where(first_half, xr, -xi))
        ker = jnp.dot(jnp.concatenate([l_hi, l_hi, l_lo], axis=1), c_terms,
                      preferred_element_type=F32)
        kers.append(ker[:, :LANES])
        kers_swap.append(ker[:, LANES:])
        xs.append((jnp.where(own_half, xr, 0.0).astype(BF16),
                   jnp.where(own_half, xi, 0.0).astype(BF16)))
        if tau + 1 < nt:
            xr, xi = _cmul(xr, xi, abr, abi)

    zero = jnp.zeros((half, LANES), F32)
    first_half_rows = lax.broadcasted_iota(jnp.int32, (half, LANES), 1) < half
    for hh in range(2):
        rows = slice(hh * half, (hh + 1) * half)
        low, high = (kers, kers_swap) if hh == 0 else (kers_swap, kers)
        pick = lambda src, tau: src[tau][rows, :] if tau >= 0 else zero
        for d in range(ns):
            top = jnp.where(first_half_rows, pick(low, 2 * d), pick(high, 2 * d + 1))
            bot = jnp.where(first_half_rows, pick(low, 2 * d - 1), pick(high, 2 * d))
            blk = jnp.concatenate([top, bot], axis=0)
            kb_ref[hh, d] = jnp.where(same_group, blk, 0.0).astype(BF16)
        for s2 in range(ns):
            for part in range(2):
                u_ref[hh, s2, part] = jnp.concatenate(
                    [xs[nt - 1 - 2 * s2][part][rows, :], xs[nt - 2 - 2 * s2][part][rows, :]], axis=0)

    pr, pi = abr, abi
    for _ in range(4):
        pr, pi = _cmul(pr, pi, pr, pi)
    assert nt == 16
    pairs = LANES // (2 * S5_GROUP)
    for part, val in enumerate((pr, pi)):
        for a in range(pairs):
            r0 = 2 * a * S5_GROUP
            piece = jnp.where(first_half[:1], val[r0:r0 + 1], val[r0 + S5_GROUP:r0 + S5_GROUP + 1])
            c0 = part * (n_state // 2) + a * LANES
            ab_ref[:, c0:c0 + LANES] = piece

    _, nbr, nbi = a_bar(nin[0], nin[1], nin[2])
    cr, ci = nin[3], nin[4]
    lane_n = lax.broadcasted_iota(jnp.int32, (half, LANES), 1)
    parity = (lane_n // S5_GROUP) % 2
    first_n = lane_n < half
    pr, pi = nbr, nbi
    outs = []
    for t in range(nt):
        wr, wi = _cmul(cr, ci, pr, pi)
        outs.append((wr, -wi))
        if t + 1 < nt:
            pr, pi = _cmul(pr, pi, nbr, nbi)
    for t2 in range(ns):
        for part in range(2):
            even, odd = outs[2 * t2][part], outs[2 * t2 + 1][part]
            packed = (jnp.where(first_n, even, pltpu.roll(odd, half, axis=1)),
                      jnp.where(first_n, pltpu.roll(even, half, axis=1), odd))
            for hh in range(2):
                for q in range(2):
                    v_ref[hh, t2, part, q * half:(q + 1) * half, :] = (
                        jnp.where(parity == q, packed[hh], 0.0).astype(BF16))


def _s5_ops(a_re, a_im, b_re, b_im, c_re, c_im, log_dt):
    g, p = a_re.shape
    h = S5_GROUP
    gps = LANES // h
    n_j = g // gps
    assert 2 * p == LANES and g % gps == 0
    n_state = 2 * gps * p
    f = lambda t: t.astype(F32)
    per_group = [f(t).reshape(n_j, gps, -1) for t in (a_re, a_im, log_dt)]
    per_chan = [f(t).reshape(n_j, gps, h, p)
                for t in (jnp.swapaxes(b_re, 1, 2), jnp.swapaxes(b_im, 1, 2), c_re, c_im)]
    ns = S5_T // 2
    return pl.pallas_call(
        functools.partial(_s5_ops_kernel, n_state=n_state),
        grid=(n_j,),
        in_specs=([pl.BlockSpec((None, gps, t.shape[-1]), lambda j: (j, 0, 0)) for t in per_group]
                  + [pl.BlockSpec((None, gps, h, p), lambda j: (j, 0, 0, 0))] * 4),
        out_specs=[
            pl.BlockSpec((None, 2, ns, LANES, LANES), lambda j: (j, 0, 0, 0, 0)),
            pl.BlockSpec((None, 2, ns, 2, LANES, LANES), lambda j: (j, 0, 0, 0, 0, 0)),
            pl.BlockSpec((None, 2, ns, 2, LANES, LANES), lambda j: (j, 0, 0, 0, 0, 0)),
            pl.BlockSpec((None, 1, n_state), lambda j: (j, 0, 0)),
        ],
        out_shape=[
            jax.ShapeDtypeStruct((n_j, 2, ns, LANES, LANES), BF16),
            jax.ShapeDtypeStruct((n_j, 2, ns, 2, LANES, LANES), BF16),
            jax.ShapeDtypeStruct((n_j, 2, ns, 2, LANES, LANES), BF16),
            jax.ShapeDtypeStruct((n_j, 1, n_state), F32),
        ],
        compiler_params=pltpu.CompilerParams(dimension_semantics=("parallel",)),
        name="s5_ops",
    )(*per_group, *per_chan)


def _in_proj_kernel(x_ref, nw_ref, w_hbm, u_ref, p_ref, l_ref, h_ref, s_ref, w_ref, stage_ref, sem,
                    *, layer, d_main, tn):
    tm = x_ref.shape[0]
    n_slabs = tn // LANES
    fold = tm // S5_T

    @pl.when(pl.program_id(0) == 0)
    def _():
        n_rows = w_hbm.shape[1]
        w_ref[n_rows:, :] = jnp.zeros((w_ref.shape[0] - n_rows, w_ref.shape[1]), BF16)
        _stream_cast_rows(w_hbm.at[layer], w_ref, stage_ref, sem)

    xf = x_ref[...]
    h_ref[...] = (xf * nw_ref[...]).astype(BF16)
    rs = lax.rsqrt(jnp.mean(xf * xf, axis=-1, keepdims=True) + EPS)

    for n in range(d_main // tn):
        res = lax.dot_general(h_ref[...], w_ref[n * tn:(n + 1) * tn, :], NT_DIMS,
                              preferred_element_type=F32) * rs
        if n == 0:
            for s in range(n_slabs):
                s_ref[s] = res[:, s * LANES:(s + 1) * LANES]
            low = lax.broadcasted_iota(jnp.int32, (fold, LANES), 1) < LANES // 2
            for s in range(n_slabs):
                for t2 in range(S5_T // 2):
                    even = s_ref[s, pl.ds(2 * t2, fold, stride=S5_T), :]
                    odd = s_ref[s, pl.ds(2 * t2 + 1, fold, stride=S5_T), :]
                    cols = slice(t2 * LANES, (t2 + 1) * LANES)
                    u_ref[2 * s, :, cols] = (
                        jnp.where(low, even, pltpu.roll(odd, LANES // 2, axis=1)).astype(BF16))
                    u_ref[2 * s + 1, :, cols] = (
                        jnp.where(low, pltpu.roll(even, LANES // 2, axis=1), odd).astype(BF16))
        else:
            for s in range(n_slabs):
                p_ref[(n - 1) * n_slabs + s] = res[:, s * LANES:(s + 1) * LANES].astype(BF16)
    low = lax.dot_general(h_ref[...], w_ref[d_main:d_main + LANES, :], NT_DIMS,
                          preferred_element_type=F32) * rs
    l_ref[...] = low.astype(BF16)


def _in_proj(x2d, norm_w, w_t, layer, *, d_main, tm, tn):
    m, d = x2d.shape
    n_slabs = tn // LANES
    n_proj = (d_main // tn - 1) * n_slabs
    assert d_main < w_t.shape[1] <= d_main + LANES
    return pl.pallas_call(
        functools.partial(_in_proj_kernel, layer=layer, d_main=d_main, tn=tn),
        grid=(m // tm,),
        in_specs=[
            pl.BlockSpec((tm, d), lambda i: (i, 0)),
            pl.BlockSpec((1, d), lambda i: (0, 0)),
            pl.BlockSpec(memory_space=pl.ANY),
        ],
        out_specs=[
            pl.BlockSpec((2 * n_slabs, tm // S5_T, S5_T * LANES // 2), lambda i: (0, i, 0)),
            pl.BlockSpec((n_proj, tm, LANES), lambda i: (0, i, 0)),
            pl.BlockSpec((tm, LANES), lambda i: (i, 0)),
        ],
        out_shape=[
            jax.ShapeDtypeStruct((2 * n_slabs, m // S5_T, S5_T * LANES // 2), BF16),
            jax.ShapeDtypeStruct((n_proj, m, LANES), BF16),
            jax.ShapeDtypeStruct((m, LANES), BF16),
        ],
        scratch_shapes=[
            pltpu.VMEM((tm, d), BF16),
            pltpu.VMEM((n_slabs, tm, LANES), F32),
            pltpu.VMEM((d_main + LANES, d), BF16),
            pltpu.VMEM((2, IN_PROJ_W_CHUNK, d), w_t.dtype),
            pltpu.SemaphoreType.DMA((2,)),
        ],
        compiler_params=pltpu.CompilerParams(
            dimension_semantics=("arbitrary",),
            vmem_limit_bytes=VMEM_LIMIT),
        name="in_proj",
    )(x2d, norm_w, w_t)


def _s5_kernel(x_ref, kb_ref, u_ref, v_ref, a_ref, d_ref, o_ref,
               m_ref, wz_ref, wy_ref, z_ref, sp_ref, y_ref, *, rows_per_seq, n_seq):
    ns = S5_T // 2
    rows = x_ref.shape[1]
    half = a_ref.shape[-1] // 2
    hstate = half // 2
    pairs = hstate // LANES
    sub = LANES // (2 * pairs)
    wide = 2 * LANES

    @pl.when(pl.program_id(0) == 0)
    def _():
        m_ref[:, (ns - 1) * LANES:, :LANES] = jnp.zeros((2, LANES, LANES), BF16)
        wz_ref[...] = jnp.zeros_like(wz_ref)
        y_ref[...] = jnp.zeros_like(y_ref)

    lane = lax.broadcasted_iota(jnp.int32, (LANES, LANES), 1)
    for hh in range(2):
        for s in range(ns):
            for k in range(2):
                lag = ns - 2 + k - s
                if lag >= 0:
                    m_ref[hh, s * LANES:(s + 1) * LANES, k * LANES:(k + 1) * LANES] = kb_ref[hh, lag]
        for s in range(ns):
            for part in range(2):
                for rho in range(2):
                    for a in range(pairs):
                        r0 = rho * (LANES // 2) + a * sub
                        c0 = part * hstate + a * LANES
                        wz_ref[hh, s * LANES + r0:s * LANES + r0 + sub, c0:c0 + LANES] = (
                            u_ref[hh, s, part, r0:r0 + sub, :])
        for t in range(ns):
            for part in range(2):
                blk = v_ref[hh, t, part]
                for a in range(pairs):
                    r0 = part * hstate + a * LANES
                    wy_ref[hh, r0:r0 + LANES, t * LANES:(t + 1) * LANES] = (
                        jnp.where(((lane % (LANES // 2)) // sub) == a, blk, jnp.zeros_like(blk)))
        zh = jnp.dot(x_ref[hh], wz_ref[hh], preferred_element_type=F32)
        z_ref[:, hh * hstate:(hh + 1) * hstate] = zh[:, :hstate]
        z_ref[:, half + hh * hstate:half + (hh + 1) * hstate] = zh[:, hstate:]

    a_re = a_ref[:, :half]
    a_im = a_ref[:, half:]

    state = [(jnp.zeros((1, half), F32), jnp.zeros((1, half), F32)) for _ in range(n_seq)]
    for c in range(rows_per_seq):
        for b in range(n_seq):
            s_re, s_im = state[b]
            r = b * rows_per_seq + c
            sp_ref[r:r + 1, :half] = s_re
            sp_ref[r:r + 1, half:] = s_im
            z_re = z_ref[r:r + 1, :half]
            z_im = z_ref[r:r + 1, half:]
            state[b] = (a_re * s_re - a_im * s_im + z_re, a_re * s_im + a_im * s_re + z_im)

    spb = [jnp.concatenate([sp_ref[:, hh * hstate:(hh + 1) * hstate],
                            sp_ref[:, half + hh * hstate:half + (hh + 1) * hstate]],
                           axis=1).astype(BF16) for hh in range(2)]
    pitch = y_ref.shape[0] // rows
    low = lax.broadcasted_iota(jnp.int32, (rows, LANES), 1) < LANES // 2
    for n in range(ns // 2):
        cols = slice(n * wide, (n + 1) * wide)
        kk = (n + 1) * wide
        ys = []
        for hh in range(2):
            acc = jnp.dot(x_ref[hh, :, :kk], m_ref[hh, ns * LANES - kk:, :],
                          preferred_element_type=F32)
            acc = acc + jnp.dot(spb[hh], wy_ref[hh, :, cols], preferred_element_type=F32)
            acc = acc + d_ref[hh, :, cols] * x_ref[hh, :, cols].astype(F32)
            ys.append(jax.nn.gelu(acc))
        for k in range(2):
            y0 = ys[0][:, k * LANES:(k + 1) * LANES]
            y1 = ys[1][:, k * LANES:(k + 1) * LANES]
            t = 2 * (2 * n + k)
            y_ref[pl.ds(t, rows, stride=pitch), :] = (
                jnp.where(low, y0, pltpu.roll(y1, LANES // 2, axis=1)))
            y_ref[pl.ds(t + 1, rows, stride=pitch), :] = (
                jnp.where(low, pltpu.roll(y0, LANES // 2, axis=1), y1))
    tokens = y_ref[...].reshape(rows, pitch, LANES)[:, :S5_T, :].reshape(rows * S5_T, LANES)
    o_ref[...] = tokens.astype(BF16)


def _s5(u2, kb, u_op, v_op, abar, dtile, *, n_seq):
    n_j, _, ns, _, _ = kb.shape
    rows = u2.shape[1]
    width = ns * LANES
    n_state = abar.shape[-1]
    return pl.pallas_call(
        functools.partial(_s5_kernel, rows_per_seq=rows // n_seq, n_seq=n_seq),
        grid=(n_j,),
        in_specs=[
            pl.BlockSpec((2, rows, width), lambda j: (j, 0, 0)),
            pl.BlockSpec((None, 2, ns, LANES, LANES), lambda j: (j, 0, 0, 0, 0)),
            pl.BlockSpec((None, 2, ns, 2, LANES, LANES), lambda j: (j, 0, 0, 0, 0, 0)),
            pl.BlockSpec((None, 2, ns, 2, LANES, LANES), lambda j: (j, 0, 0, 0, 0, 0)),
            pl.BlockSpec((None, 1, n_state), lambda j: (j, 0, 0)),
            pl.BlockSpec((None, 2, 1, width), lambda j: (j, 0, 0, 0)),
        ],
        out_specs=pl.BlockSpec((None, rows * S5_T, LANES), lambda j: (j, 0, 0)),
        out_shape=jax.ShapeDtypeStruct((n_j, rows * S5_T, LANES), BF16),
        scratch_shapes=[
            pltpu.VMEM((2, width, 2 * LANES), BF16),
            pltpu.VMEM((2, width, n_state // 2), BF16),
            pltpu.VMEM((2, n_state // 2, width), BF16),
            pltpu.VMEM((rows, n_state), F32),
            pltpu.VMEM((rows, n_state), F32),
            pltpu.VMEM((rows * S5_UNFOLD_PITCH, LANES), F32),
        ],
        compiler_params=pltpu.CompilerParams(
            dimension_semantics=("arbitrary",),
            vmem_limit_bytes=VMEM_LIMIT),
        name="s5",
    )(u2, kb, u_op, v_op, abar, dtile)


def _log_sigmoid(x):
    return jnp.minimum(x, 0.0) - jnp.log(1.0 + jnp.exp(-jnp.abs(x)))


def _gla_kernel(q_ref, k_ref, v_ref, gz_ref, gl_ref, gu_ref, gb_ref, nw_ref, o_ref, st_ref,
                *, chunk):
    n_heads, rows, dk = q_ref.shape
    n_chunks = rows // chunk
    d_gk = n_heads * dk

    @pl.when(pl.program_id(1) == 0)
    def _():
        st_ref[...] = jnp.zeros_like(st_ref)

    zg = jnp.dot(gl_ref[...], gu_ref[...], preferred_element_type=F32) + gb_ref[...]
    lg = _log_sigmoid(zg) * (1.0 / GLA_TAU)

    hi = lg.astype(BF16)
    lo = (lg - hi.astype(F32)).astype(BF16)
    r_id = lax.broadcasted_iota(jnp.int32, (chunk, chunk), 0)
    c_id = lax.broadcasted_iota(jnp.int32, (chunk, chunk), 1)
    causal = c_id <= r_id
    tri = jnp.where(causal, 1.0, 0.0).astype(BF16)
    tri2 = jnp.concatenate([tri, tri], axis=1)
    b_parts, last_parts = [], []
    for n in range(n_chunks):
        rs = slice(n * chunk, (n + 1) * chunk)
        b_n = jnp.dot(tri2, jnp.concatenate([hi[rs], lo[rs]], axis=0), preferred_element_type=F32)
        b_parts.append(b_n)
        last_parts.append(jnp.broadcast_to(b_n[chunk - 1:chunk], (chunk, d_gk)))
    b = jnp.concatenate(b_parts, axis=0)
    e_pos = jnp.exp(b)
    e_neg = jnp.exp(-b)
    decay = jnp.exp(jnp.concatenate(last_parts, axis=0))

    for h in range(n_heads):
        hs = slice(h * dk, (h + 1) * dk)
        q_e = (q_ref[h].astype(F32) * ((dk ** -0.5) * e_pos[:, hs])).astype(BF16)
        k_f = k_ref[h].astype(F32) * e_neg[:, hs]
        k_e = k_f.astype(BF16)
        k_t = (k_f * decay[:, hs]).astype(BF16)
        v = jnp.concatenate([v_ref[2 * h], v_ref[2 * h + 1]], axis=1)

        chunks = [slice(n * chunk, (n + 1) * chunk) for n in range(n_chunks)]
        attn = [lax.dot_general(q_e[rs], k_e[rs], NT_DIMS, preferred_element_type=F32)
                for rs in chunks]
        kv_t = [lax.dot_general(v[rs], k_t[rs], TN_DIMS, preferred_element_type=F32)
                for rs in chunks]
        attn = [jnp.where(causal, a, 0.0).astype(BF16) for a in attn]
        st = st_ref[h]
        st_in = []
        for n in range(n_chunks):
            st_in.append(st.astype(BF16))
            st = decay[n * chunk:n * chunk + 1, hs] * st + kv_t[n]
        st_ref[h] = st
        outs = [jnp.dot(attn[n], v[rs], preferred_element_type=F32)
                + lax.dot_general(q_e[rs], st_in[n], NT_DIMS, preferred_element_type=F32)
                for n, rs in enumerate(chunks)]

        o = jnp.concatenate(outs, axis=0)
        ms = jnp.mean(o * o, axis=-1, keepdims=True)
        o = o * lax.rsqrt(ms + EPS) * nw_ref[...]
        gz = jnp.concatenate([gz_ref[2 * h], gz_ref[2 * h + 1]], axis=1).astype(F32)
        y = (o * (gz * jax.nn.sigmoid(gz))).astype(BF16)
        o_ref[2 * h] = y[:, :LANES]
        o_ref[2 * h + 1] = y[:, LANES:]


def _gla(proj, g_low, gate_up_pad, gate_bias, norm_w, *, n_seq, seq_len, rows, slab_q, slab_k,
         slab_v, slab_gz):
    m = proj.shape[1]
    nb = seq_len // rows
    nh = GLA_HEADS
    dk = LANES
    dv = 2 * LANES
    tok = lambda b, n: b * nb + n
    return pl.pallas_call(
        functools.partial(_gla_kernel, chunk=GLA_CHUNK),
        grid=(n_seq, nb),
        in_specs=[
            pl.BlockSpec((nh, rows, LANES), lambda b, n: (slab_q // nh, tok(b, n), 0)),
            pl.BlockSpec((nh, rows, LANES), lambda b, n: (slab_k // nh, tok(b, n), 0)),
            pl.BlockSpec((2 * nh, rows, LANES), lambda b, n: (slab_v // (2 * nh), tok(b, n), 0)),
            pl.BlockSpec((2 * nh, rows, LANES), lambda b, n: (slab_gz // (2 * nh), tok(b, n), 0)),
            pl.BlockSpec((rows, LANES), lambda b, n: (tok(b, n), 0)),
            pl.BlockSpec((LANES, nh * dk), lambda b, n: (0, 0)),
            pl.BlockSpec((1, nh * dk), lambda b, n: (0, 0)),
            pl.BlockSpec((1, dv), lambda b, n: (0, 0)),
        ],
        out_specs=pl.BlockSpec((2 * nh, rows, LANES), lambda b, n: (0, tok(b, n), 0)),
        out_shape=jax.ShapeDtypeStruct((2 * nh, m, LANES), BF16),
        scratch_shapes=[pltpu.VMEM((nh, dv, dk), F32)],
        compiler_params=pltpu.CompilerParams(
            dimension_semantics=("parallel", "arbitrary"),
            vmem_limit_bytes=VMEM_LIMIT),
        name="gla",
    )(proj, proj, proj, proj, g_low, gate_up_pad, gate_bias, norm_w)


def _out_proj_kernel(g_ref, z_ref, yg_ref, x_ref, gw_hbm, gb_ref, wo_hbm, pw_ref,
                     o_ref, gw_ref, wo_ref, stage_ref, sem, *, layer):
    n_s5 = g_ref.shape[0]
    d_s5 = n_s5 * LANES
    tm = x_ref.shape[0]
    rc = OUT_PROJ_ROW_CHUNK

    @pl.when(pl.program_id(0) == 0)
    def _():
        _stream_cast_rows(wo_hbm.at[layer], wo_ref, stage_ref, sem)
        _stream_cast_rows(gw_hbm.at[layer], gw_ref, stage_ref, sem)

    for c in range(tm // rc):
        rows = slice(c * rc, (c + 1) * rc)
        g = jnp.concatenate([g_ref[s, rows, :] for s in range(n_s5)], axis=1)
        gate = jax.nn.sigmoid(jnp.dot(g, gw_ref[...], preferred_element_type=F32) + gb_ref[...])
        z = jnp.concatenate([z_ref[s, rows, :] for s in range(n_s5)], axis=1).astype(F32)
        y_s5 = (g.astype(F32) * gate * (z * jax.nn.sigmoid(z))).astype(BF16)
        y_gla = jnp.concatenate([yg_ref[s, rows, :] for s in range(yg_ref.shape[0])], axis=1)
        mixed = (jnp.dot(y_s5, wo_ref[:d_s5, :], preferred_element_type=F32)
                 + jnp.dot(y_gla, wo_ref[d_s5:, :], preferred_element_type=F32))
        ms = jnp.mean(mixed * mixed, axis=-1, keepdims=True)
        o_ref[rows, :] = x_ref[rows, :] + mixed * lax.rsqrt(ms + EPS) * pw_ref[...]


def _out_proj(g_slabs, proj, y_gla, x2d, glu_w, glu_b, w_out, post_w, layer, *, tm, slab_z):
    m, d = x2d.shape
    n_s5 = g_slabs.shape[0]
    n_gla = y_gla.shape[0]
    d_s5 = n_s5 * LANES
    d_mix = w_out.shape[1]
    assert glu_w.dtype == w_out.dtype and glu_w.shape[2] <= d
    return pl.pallas_call(
        functools.partial(_out_proj_kernel, layer=layer),
        grid=(m // tm,),
        in_specs=[
            pl.BlockSpec((n_s5, tm, LANES), lambda i: (0, i, 0)),
            pl.BlockSpec((n_s5, tm, LANES), lambda i: (slab_z // n_s5, i, 0)),
            pl.BlockSpec((n_gla, tm, LANES), lambda i: (0, i, 0)),
            pl.BlockSpec((tm, d), lambda i: (i, 0)),
            pl.BlockSpec(memory_space=pl.ANY),
            pl.BlockSpec((1, d_s5), lambda i: (0, 0)),
            pl.BlockSpec(memory_space=pl.ANY),
            pl.BlockSpec((1, d), lambda i: (0, 0)),
        ],
        out_specs=pl.BlockSpec((tm, d), lambda i: (i, 0)),
        out_shape=jax.ShapeDtypeStruct((m, d), F32),
        scratch_shapes=[
            pltpu.VMEM((d_s5, d_s5), BF16),
            pltpu.VMEM((d_mix, d), BF16),
            pltpu.VMEM((2, OUT_PROJ_W_CHUNK, d), w_out.dtype),
            pltpu.SemaphoreType.DMA((2,)),
        ],
        compiler_params=pltpu.CompilerParams(
            dimension_semantics=("arbitrary",),
            vmem_limit_bytes=VMEM_LIMIT),
        name="out_proj",
    )(g_slabs, proj, y_gla, x2d, glu_w, glu_b, w_out, post_w)


def kernel(x, pre_norm_w, w_in, s5_A_re, s5_A_im, s5_B_re, s5_B_im, s5_C_re, s5_C_im, s5_D,
           s5_log_dt, s5_glu_w, s5_glu_b, gla_gate_up, gla_gate_bias, gla_norm_w, w_out,
           post_norm_w):
    bsz, seq_len, d_model = x.shape
    depth = w_in.shape[0]
    d_in = w_in.shape[2]
    d_s5 = s5_glu_w.shape[1]
    rank, d_gk = gla_gate_up.shape[1:]
    d_gv = GLA_HEADS * gla_norm_w.shape[1]
    m = bsz * seq_len
    d_main = 2 * d_s5 + 2 * d_gk + 2 * d_gv
    tn = d_s5
    assert d_s5 % LANES == 0 and d_gk == GLA_HEADS * LANES and d_gv == GLA_HEADS * 2 * LANES
    assert d_in == d_main + rank and rank <= LANES and d_main % tn == 0 and 2 * d_gk == tn
    assert seq_len % (S5_T * 8) == 0 and w_out.shape[1] == 2 * d_s5

    n_s5 = d_s5 // LANES
    slab_z = 0
    slab_q = n_s5
    slab_k = slab_q + d_gk // LANES
    slab_v = slab_k + d_gk // LANES
    slab_gz = slab_v + d_gv // LANES

    resid = x.astype(F32).reshape(m, d_model)
    for l in range(depth):
        kb, u_op, v_op, abar = _s5_ops(s5_A_re[l], s5_A_im[l], s5_B_re[l], s5_B_im[l],
                                       s5_C_re[l], s5_C_im[l], s5_log_dt[l])
        dtile = jnp.tile(s5_D[l].astype(F32).reshape(n_s5, 2, 1, LANES // 2), (1, 1, 1, S5_T))

        u2, proj, g_low = _in_proj(resid, pre_norm_w[l].astype(F32)[None],
                                   jnp.swapaxes(w_in, 1, 2), l, d_main=d_main, tm=512, tn=tn)

        g_slabs = _s5(u2, kb, u_op, v_op, abar, dtile, n_seq=bsz)

        gate_up_pad = jnp.pad(gla_gate_up[l].astype(BF16), ((0, LANES - rank), (0, 0)))
        y_gla = _gla(proj, g_low, gate_up_pad, gla_gate_bias[l].astype(F32)[None],
                     gla_norm_w[l].astype(F32)[None], n_seq=bsz, seq_len=seq_len, rows=512,
                     slab_q=slab_q, slab_k=slab_k, slab_v=slab_v, slab_gz=slab_gz)

        resid = _out_proj(g_slabs, proj, y_gla, resid, s5_glu_w, s5_glu_b[l].astype(F32)[None],
                          w_out, post_norm_w[l].astype(F32)[None], l, tm=512, slab_z=slab_z)
    return resid.reshape(bsz, seq_len, d_model).astype(x.dtype)
```

```python
import functools

import jax
import jax.numpy as jnp
from jax import lax
from jax.experimental import pallas as pl
from jax.experimental.pallas import tpu as pltpu

F32 = jnp.float32
BF16 = jnp.bfloat16

S5_GROUP = 16
GLA_HEADS = 4
GLA_TAU = 16.0
GLA_CHUNK = 64
EPS = 1e-6

LANES = 128
S5_T = 16
S5_UNFOLD_PITCH = 24
OUT_PROJ_ROW_CHUNK = 256
IN_PROJ_W_CHUNK = 256
OUT_PROJ_W_CHUNK = 256
VMEM_LIMIT = 56 * 1024 * 1024


NT_DIMS = (((1,), (1,)), ((), ()))
TN_DIMS = (((0,), (0,)), ((), ()))


def _cmul(ar, ai, br, bi):
    return ar * br - ai * bi, ar * bi + ai * br


def _stream_cast_rows(src, dst_ref, stage_ref, sem):
    n_rows, width = src.shape
    chunk = stage_ref.shape[1]
    sizes = [chunk] * (n_rows // chunk) + ([n_rows % chunk] if n_rows % chunk else [])

    def copy(c):
        return pltpu.make_async_copy(
            src.at[pl.ds(c * chunk, sizes[c]), :],
            stage_ref.at[c % 2, pl.ds(0, sizes[c]), pl.ds(0, width)], sem.at[c % 2])

    for c in range(min(2, len(sizes))):
        copy(c).start()
    for c, rows in enumerate(sizes):
        copy(c).wait()
        dst_ref[c * chunk:c * chunk + rows, :width] = (
            stage_ref[c % 2, :rows, :width].astype(dst_ref.dtype))
        if c + 2 < len(sizes):
            copy(c + 2).start()


def _s5_ops_kernel(*refs, n_state):
    (a_re_ref, a_im_ref, ldt_ref, b_re_ref, b_im_ref, c_re_ref, c_im_ref,
     kb_ref, u_ref, v_ref, ab_ref) = refs
    nt = S5_T
    ns = nt // 2
    half = LANES // 2
    gps, p = a_re_ref.shape

    def terms(t):
        hi = t.astype(BF16).astype(F32)
        mid = (t - hi).astype(BF16).astype(F32)
        return hi, mid, t - hi - mid

    def select(t, onehot):
        return sum(lax.dot_general(x, onehot, TN_DIMS, preferred_element_type=F32)
                   for x in terms(t))

    def spread_rows(t):
        return sum(lax.dot_general(spread, x, TN_DIMS, preferred_element_type=F32)
                   for x in terms(t))

    group_of_lane = lax.broadcasted_iota(jnp.int32, (gps, LANES), 1) // S5_GROUP
    spread = jnp.where(group_of_lane == lax.broadcasted_iota(jnp.int32, (gps, LANES), 0), 1.0, 0.0)
    eye = jnp.where(lax.broadcasted_iota(jnp.int32, (LANES, LANES), 0)
                    == lax.broadcasted_iota(jnp.int32, (LANES, LANES), 1), 1.0, 0.0)
    dup = lambda t: jnp.concatenate([t, t], axis=1)
    are_c = jnp.minimum(a_re_ref[...], -1e-4)
    aim_c = a_im_ref[...]
    dt = jnp.exp(ldt_ref[...])
    mag = jnp.exp(are_c * dt)
    abr_c = mag * jnp.cos(aim_c * dt)
    abi_c = mag * jnp.sin(aim_c * dt)
    are, aim, abr, abi = [spread_rows(dup(t)) for t in (are_c, aim_c, abr_c, abi_c)]
    b_re, b_im = [dup(r[...].reshape(gps * S5_GROUP, p)) for r in (b_re_ref, b_im_ref)]
    nbr, nbi = [select(t, spread) for t in (abr_c, abi_c)]
    cr, ci = [select(r[...].reshape(gps * S5_GROUP, p), eye) for r in (c_re_ref, c_im_ref)]

    den = are * are + aim * aim
    nr = abr - 1.0
    fr = (nr * are + abi * aim) / den
    fi = (abi * are - nr * aim) / den
    xr, xi = _cmul(fr, fi, b_re, b_im)

    row = lax.broadcasted_iota(jnp.int32, (LANES, LANES), 0)
    lane = lax.broadcasted_iota(jnp.int32, (LANES, LANES), 1)
    own_half = ((row // S5_GROUP) % 2) == (lane // half)
    same_group = ((row % half) // S5_GROUP) == ((lane % half) // S5_GROUP)
    first_half = lane < half

    def split(t):
        hi = t.astype(BF16)
        return hi, (t - hi.astype(F32)).astype(BF16)

    c_stack = jnp.concatenate([cr, ci], axis=0)
    c_swap = pltpu.roll(c_stack, half, axis=1)
    c_hi, c_lo = split(jnp.concatenate([c_stack, c_swap], axis=1))
    c_terms = jnp.concatenate([c_hi, c_lo, c_hi], axis=0)
    kers, kers_swap, xs = [], [], []
    for tau in range(nt):
        l_hi, l_lo = split(jnp.where(first_half, xr, -xi))
        ker = jnp.dot(jnp.concatenate([l_hi, l_hi, l_lo], axis=1), c_terms,
                      preferred_element_type=F32)
        kers.append(ker[:, :LANES])
        kers_swap.append(ker[:, LANES:])
        xs.append((jnp.where(own_half, xr, 0.0).astype(BF16),
                   jnp.where(own_half, xi, 0.0).astype(BF16)))
        if tau + 1 < nt:
            xr, xi = _cmul(xr, xi, abr, abi)

    zero = jnp.zeros((half, LANES), F32)
    first_half_rows = lax.broadcasted_iota(jnp.int32, (half, LANES), 1) < half
    for hh in range(2):
        rows = slice(hh * half, (hh + 1) * half)
        low, high = (kers, kers_swap) if hh == 0 else (kers_swap, kers)
        pick = lambda src, tau: src[tau][rows, :] if tau >= 0 else zero
        for d in range(ns):
            top = jnp.where(first_half_rows, pick(low, 2 * d), pick(high, 2 * d + 1))
            bot = jnp.where(first_half_rows, pick(low, 2 * d - 1), pick(high, 2 * d))
            blk = jnp.concatenate([top, bot], axis=0)
            kb_ref[hh, d] = jnp.where(same_group, blk, 0.0).astype(BF16)
        for s2 in range(ns):
            for part in range(2):
                u_ref[hh, s2, part] = jnp.concatenate(
                    [xs[nt - 1 - 2 * s2][part][rows, :], xs[nt - 2 - 2 * s2][part][rows, :]], axis=0)

    pr, pi = abr, abi
    for _ in range(4):
        pr, pi = _cmul(pr, pi, pr, pi)
    assert nt == 16
    pairs = LANES // (2 * S5_GROUP)
    for part, val in enumerate((pr, pi)):
        for a in range(pairs):
            r0 = 2 * a * S5_GROUP
            piece = jnp.where(first_half[:1], val[r0:r0 + 1], val[r0 + S5_GROUP:r0 + S5_GROUP + 1])
            c0 = part * (n_state // 2) + a * LANES
            ab_ref[:, c0:c0 + LANES] = piece

    lane_n = lax.broadcasted_iota(jnp.int32, (half, LANES), 1)
    parity = (lane_n // S5_GROUP) % 2
    first_n = lane_n < half
    pr, pi = nbr, nbi
    outs = []
    for t in range(nt):
        wr, wi = _cmul(cr, ci, pr, pi)
        outs.append((wr, -wi))
        if t + 1 < nt:
            pr, pi = _cmul(pr, pi, nbr, nbi)
    for t2 in range(ns):
        for part in range(2):
            even, odd = outs[2 * t2][part], outs[2 * t2 + 1][part]
            packed = (jnp.where(first_n, even, pltpu.roll(odd, half, axis=1)),
                      jnp.where(first_n, pltpu.roll(even, half, axis=1), odd))
            for hh in range(2):
                for q in range(2):
                    v_ref[hh, t2, part, q * half:(q + 1) * half, :] = (
                        jnp.where(parity == q, packed[hh], 0.0).astype(BF16))


def _s5_ops(a_re, a_im, b_re, b_im, c_re, c_im, log_dt):
    g, p = a_re.shape
    h = S5_GROUP
    gps = LANES // h
    n_j = g // gps
    assert 2 * p == LANES and g % gps == 0
    n_state = 2 * gps * p
    f = lambda t: t.astype(F32)
    per_group = [f(t).reshape(n_j, gps, -1) for t in (a_re, a_im, log_dt)]
    per_chan = [f(t).reshape(n_j, gps, h, p)
                for t in (jnp.swapaxes(b_re, 1, 2), jnp.swapaxes(b_im, 1, 2), c_re, c_im)]
    ns = S5_T // 2
    return pl.pallas_call(
        functools.partial(_s5_ops_kernel, n_state=n_state),
        grid=(n_j,),
        in_specs=([pl.BlockSpec((None, gps, t.shape[-1]), lambda j: (j, 0, 0)) for t in per_group]
                  + [pl.BlockSpec((None, gps, h, p), lambda j: (j, 0, 0, 0))] * 4),
        out_specs=[
            pl.BlockSpec((None, 2, ns, LANES, LANES), lambda j: (j, 0, 0, 0, 0)),
            pl.BlockSpec((None, 2, ns, 2, LANES, LANES), lambda j: (j, 0, 0, 0, 0, 0)),
            pl.BlockSpec((None, 2, ns, 2, LANES, LANES), lambda j: (j, 0, 0, 0, 0, 0)),
            pl.BlockSpec((None, 1, n_state), lambda j: (j, 0, 0)),
        ],
        out_shape=[
            jax.ShapeDtypeStruct((n_j, 2, ns, LANES, LANES), BF16),
            jax.ShapeDtypeStruct((n_j, 2, ns, 2, LANES, LANES), BF16),
            jax.ShapeDtypeStruct((n_j, 2, ns, 2, LANES, LANES), BF16),
            jax.ShapeDtypeStruct((n_j, 1, n_state), F32),
        ],
        compiler_params=pltpu.CompilerParams(dimension_semantics=("parallel",)),
        name="s5_ops",
    )(*per_group, *per_chan)


def _in_proj_kernel(x_ref, nw_ref, w_hbm, u_ref, p_ref, l_ref, h_ref, s_ref, w_ref, stage_ref, sem,
                    *, layer, d_main, tn):
    tm = x_ref.shape[0]
    n_slabs = tn // LANES
    fold = tm // S5_T

    @pl.when(pl.program_id(0) == 0)
    def _():
        n_rows = w_hbm.shape[1]
        w_ref[n_rows:, :] = jnp.zeros((w_ref.shape[0] - n_rows, w_ref.shape[1]), BF16)
        _stream_cast_rows(w_hbm.at[layer], w_ref, stage_ref, sem)

    xf = x_ref[...]
    h_ref[...] = (xf * nw_ref[...]).astype(BF16)
    rs = lax.rsqrt(jnp.mean(xf * xf, axis=-1, keepdims=True) + EPS)

    for n in range(d_main // tn):
        res = lax.dot_general(h_ref[...], w_ref[n * tn:(n + 1) * tn, :], NT_DIMS,
                              preferred_element_type=F32) * rs
        if n == 0:
            for s in range(n_slabs):
                s_ref[s] = res[:, s * LANES:(s + 1) * LANES]
            low = lax.broadcasted_iota(jnp.int32, (fold, LANES), 1) < LANES // 2
            for s in range(n_slabs):
                for t2 in range(S5_T // 2):
                    even = s_ref[s, pl.ds(2 * t2, fold, stride=S5_T), :]
                    odd = s_ref[s, pl.ds(2 * t2 + 1, fold, stride=S5_T), :]
                    cols = slice(t2 * LANES, (t2 + 1) * LANES)
                    u_ref[2 * s, :, cols] = (
                        jnp.where(low, even, pltpu.roll(odd, LANES // 2, axis=1)).astype(BF16))
                    u_ref[2 * s + 1, :, cols] = (
                        jnp.where(low, pltpu.roll(even, LANES // 2, axis=1), odd).astype(BF16))
        else:
            for s in range(n_slabs):
                p_ref[(n - 1) * n_slabs + s] = res[:, s * LANES:(s + 1) * LANES].astype(BF16)
    low = lax.dot_general(h_ref[...], w_ref[d_main:d_main + LANES, :], NT_DIMS,
                          preferred_element_type=F32) * rs
    l_ref[...] = low.astype(BF16)


def _in_proj(x2d, norm_w, w_t, layer, *, d_main, tm, tn):
    m, d = x2d.shape
    n_slabs = tn // LANES
    n_proj = (d_main // tn - 1) * n_slabs
    assert d_main < w_t.shape[1] <= d_main + LANES
    return pl.pallas_call(
        functools.partial(_in_proj_kernel, layer=layer, d_main=d_main, tn=tn),
        grid=(m // tm,),
        in_specs=[
            pl.BlockSpec((tm, d), lambda i: (i, 0)),
            pl.BlockSpec((1, d), lambda i: (0, 0)),
            pl.BlockSpec(memory_space=pl.ANY),
        ],
        out_specs=[
            pl.BlockSpec((2 * n_slabs, tm // S5_T, S5_T * LANES // 2), lambda i: (0, i, 0)),
            pl.BlockSpec((n_proj, tm, LANES), lambda i: (0, i, 0)),
            pl.BlockSpec((tm, LANES), lambda i: (i, 0)),
        ],
        out_shape=[
            jax.ShapeDtypeStruct((2 * n_slabs, m // S5_T, S5_T * LANES // 2), BF16),
            jax.ShapeDtypeStruct((n_proj, m, LANES), BF16),
            jax.ShapeDtypeStruct((m, LANES), BF16),
        ],
        scratch_shapes=[
            pltpu.VMEM((tm, d), BF16),
            pltpu.VMEM((n_slabs, tm, LANES), F32),
            pltpu.VMEM((d_main + LANES, d), BF16),
            pltpu.VMEM((2, IN_PROJ_W_CHUNK, d), w_t.dtype),
            pltpu.SemaphoreType.DMA((2,)),
        ],
        compiler_params=pltpu.CompilerParams(
            dimension_semantics=("arbitrary",),
            vmem_limit_bytes=VMEM_LIMIT),
        name="in_proj",
    )(x2d, norm_w, w_t)


def _s5_kernel(x_ref, kb_ref, u_ref, v_ref, a_ref, d_ref, o_ref,
               m_ref, wz_ref, wy_ref, z_ref, sp_ref, y_ref, *, rows_per_seq, n_seq):
    ns = S5_T // 2
    rows = x_ref.shape[1]
    half = a_ref.shape[-1] // 2
    hstate = half // 2
    pairs = hstate // LANES
    sub = LANES // (2 * pairs)
    wide = 2 * LANES

    @pl.when(pl.program_id(0) == 0)
    def _():
        m_ref[:, (ns - 1) * LANES:, :LANES] = jnp.zeros((2, LANES, LANES), BF16)
        wz_ref[...] = jnp.zeros_like(wz_ref)
        y_ref[...] = jnp.zeros_like(y_ref)

    lane = lax.broadcasted_iota(jnp.int32, (LANES, LANES), 1)
    for hh in range(2):
        for s in range(ns):
            for k in range(2):
                lag = ns - 2 + k - s
                if lag >= 0:
                    m_ref[hh, s * LANES:(s + 1) * LANES, k * LANES:(k + 1) * LANES] = kb_ref[hh, lag]
        for s in range(ns):
            for part in range(2):
                for rho in range(2):
                    for a in range(pairs):
                        r0 = rho * (LANES // 2) + a * sub
                        c0 = part * hstate + a * LANES
                        wz_ref[hh, s * LANES + r0:s * LANES + r0 + sub, c0:c0 + LANES] = (
                            u_ref[hh, s, part, r0:r0 + sub, :])
        for t in range(ns):
            for part in range(2):
                blk = v_ref[hh, t, part]
                for a in range(pairs):
                    r0 = part * hstate + a * LANES
                    wy_ref[hh, r0:r0 + LANES, t * LANES:(t + 1) * LANES] = (
                        jnp.where(((lane % (LANES // 2)) // sub) == a, blk, jnp.zeros_like(blk)))
        zh = jnp.dot(x_ref[hh], wz_ref[hh], preferred_element_type=F32)
        z_ref[:, hh * hstate:(hh + 1) * hstate] = zh[:, :hstate]
        z_ref[:, half + hh * hstate:half + (hh + 1) * hstate] = zh[:, hstate:]

    a_re = a_ref[:, :half]
    a_im = a_ref[:, half:]

    state = [(jnp.zeros((1, half), F32), jnp.zeros((1, half), F32)) for _ in range(n_seq)]
    for c in range(rows_per_seq):
        for b in range(n_seq):
            s_re, s_im = state[b]
            r = b * rows_per_seq + c
            sp_ref[r:r + 1, :half] = s_re
            sp_ref[r:r + 1, half:] = s_im
            z_re = z_ref[r:r + 1, :half]
            z_im = z_ref[r:r + 1, half:]
            state[b] = (a_re * s_re - a_im * s_im + z_re, a_re * s_im + a_im * s_re + z_im)

    spb = [jnp.concatenate([sp_ref[:, hh * hstate:(hh + 1) * hstate],
                            sp_ref[:, half + hh * hstate:half + (hh + 1) * hstate]],
                           axis=1).astype(BF16) for hh in range(2)]
    pitch = y_ref.shape[0] // rows
    low = lax.broadcasted_iota(jnp.int32, (rows, LANES), 1) < LANES // 2
    for n in range(ns // 2):
        cols = slice(n * wide, (n + 1) * wide)
        kk = (n + 1) * wide
        ys = []
        for hh in range(2):
            acc = jnp.dot(x_ref[hh, :, :kk], m_ref[hh, ns * LANES - kk:, :],
                          preferred_element_type=F32)
            acc = acc + jnp.dot(spb[hh], wy_ref[hh, :, cols], preferred_element_type=F32)
            acc = acc + d_ref[hh, :, cols] * x_ref[hh, :, cols].astype(F32)
            ys.append(jax.nn.gelu(acc))
        for k in range(2):
            y0 = ys[0][:, k * LANES:(k + 1) * LANES]
            y1 = ys[1][:, k * LANES:(k + 1) * LANES]
            t = 2 * (2 * n + k)
            y_ref[pl.ds(t, rows, stride=pitch), :] = (
                jnp.where(low, y0, pltpu.roll(y1, LANES // 2, axis=1)))
            y_ref[pl.ds(t + 1, rows, stride=pitch), :] = (
                jnp.where(low, pltpu.roll(y0, LANES // 2, axis=1), y1))
    tokens = y_ref[...].reshape(rows, pitch, LANES)[:, :S5_T, :].reshape(rows * S5_T, LANES)
    o_ref[...] = tokens.astype(BF16)


def _s5(u2, kb, u_op, v_op, abar, dtile, *, n_seq):
    n_j, _, ns, _, _ = kb.shape
    rows = u2.shape[1]
    width = ns * LANES
    n_state = abar.shape[-1]
    return pl.pallas_call(
        functools.partial(_s5_kernel, rows_per_seq=rows // n_seq, n_seq=n_seq),
        grid=(n_j,),
        in_specs=[
            pl.BlockSpec((2, rows, width), lambda j: (j, 0, 0)),
            pl.BlockSpec((None, 2, ns, LANES, LANES), lambda j: (j, 0, 0, 0, 0)),
            pl.BlockSpec((None, 2, ns, 2, LANES, LANES), lambda j: (j, 0, 0, 0, 0, 0)),
            pl.BlockSpec((None, 2, ns, 2, LANES, LANES), lambda j: (j, 0, 0, 0, 0, 0)),
            pl.BlockSpec((None, 1, n_state), lambda j: (j, 0, 0)),
            pl.BlockSpec((None, 2, 1, width), lambda j: (j, 0, 0, 0)),
        ],
        out_specs=pl.BlockSpec((None, rows * S5_T, LANES), lambda j: (j, 0, 0)),
        out_shape=jax.ShapeDtypeStruct((n_j, rows * S5_T, LANES), BF16),
        scratch_shapes=[
            pltpu.VMEM((2, width, 2 * LANES), BF16),
            pltpu.VMEM((2, width, n_state // 2), BF16),
            pltpu.VMEM((2, n_state // 2, width), BF16),
            pltpu.VMEM((rows, n_state), F32),
            pltpu.VMEM((rows, n_state), F32),
            pltpu.VMEM((rows * S5_UNFOLD_PITCH, LANES), F32),
        ],
        compiler_params=pltpu.CompilerParams(
            dimension_semantics=("arbitrary",),
            vmem_limit_bytes=VMEM_LIMIT),
        name="s5",
    )(u2, kb, u_op, v_op, abar, dtile)


def _log_sigmoid(x):
    return jnp.minimum(x, 0.0) - jnp.log(1.0 + jnp.exp(-jnp.abs(x)))


def _gla_kernel(q_ref, k_ref, v_ref, gz_ref, gl_ref, gu_ref, gb_ref, nw_ref, o_ref, st_ref,
                *, chunk):
    n_heads, rows, dk = q_ref.shape
    n_chunks = rows // chunk
    d_gk = n_heads * dk

    @pl.when(pl.program_id(1) == 0)
    def _():
        st_ref[...] = jnp.zeros_like(st_ref)

    zg = jnp.dot(gl_ref[...], gu_ref[...], preferred_element_type=F32) + gb_ref[...]
    lg = _log_sigmoid(zg) * (1.0 / GLA_TAU)

    hi = lg.astype(BF16)
    lo = (lg - hi.astype(F32)).astype(BF16)
    r_id = lax.broadcasted_iota(jnp.int32, (chunk, chunk), 0)
    c_id = lax.broadcasted_iota(jnp.int32, (chunk, chunk), 1)
    causal = c_id <= r_id
    tri = jnp.where(causal, 1.0, 0.0).astype(BF16)
    tri2 = jnp.concatenate([tri, tri], axis=1)
    b_parts, last_parts = [], []
    for n in range(n_chunks):
        rs = slice(n * chunk, (n + 1) * chunk)
        b_n = jnp.dot(tri2, jnp.concatenate([hi[rs], lo[rs]], axis=0), preferred_element_type=F32)
        b_parts.append(b_n)
        last_parts.append(jnp.broadcast_to(b_n[chunk - 1:chunk], (chunk, d_gk)))
    b = jnp.concatenate(b_parts, axis=0)
    e_pos = jnp.exp(b)
    e_neg = jnp.exp(-b)
    decay = jnp.exp(jnp.concatenate(last_parts, axis=0))

    for h in range(n_heads):
        hs = slice(h * dk, (h + 1) * dk)
        q_e = (q_ref[h].astype(F32) * ((dk ** -0.5) * e_pos[:, hs])).astype(BF16)
        k_f = k_ref[h].astype(F32) * e_neg[:, hs]
        k_e = k_f.astype(BF16)
        k_t = (k_f * decay[:, hs]).astype(BF16)
        v = jnp.concatenate([v_ref[2 * h], v_ref[2 * h + 1]], axis=1)

        chunks = [slice(n * chunk, (n + 1) * chunk) for n in range(n_chunks)]
        attn = [lax.dot_general(q_e[rs], k_e[rs], NT_DIMS, preferred_element_type=F32)
                for rs in chunks]
        kv_t = [lax.dot_general(v[rs], k_t[rs], TN_DIMS, preferred_element_type=F32)
                for rs in chunks]
        attn = [jnp.where(causal, a, 0.0).astype(BF16) for a in attn]
        st = st_ref[h]
        st_in = []
        for n in range(n_chunks):
            st_in.append(st.astype(BF16))
            st = decay[n * chunk:n * chunk + 1, hs] * st + kv_t[n]
        st_ref[h] = st
        outs = [jnp.dot(attn[n], v[rs], preferred_element_type=F32)
                + lax.dot_general(q_e[rs], st_in[n], NT_DIMS, preferred_element_type=F32)
                for n, rs in enumerate(chunks)]

        o = jnp.concatenate(outs, axis=0)
        ms = jnp.mean(o * o, axis=-1, keepdims=True)
        o = o * lax.rsqrt(ms + EPS) * nw_ref[...]
        gz = jnp.concatenate([gz_ref[2 * h], gz_ref[2 * h + 1]], axis=1).astype(F32)
        y = (o * (gz * jax.nn.sigmoid(gz))).astype(BF16)
        o_ref[2 * h] = y[:, :LANES]
        o_ref[2 * h + 1] = y[:, LANES:]


def _gla(proj, g_low, gate_up_pad, gate_bias, norm_w, *, n_seq, seq_len, rows, slab_q, slab_k,
         slab_v, slab_gz):
    m = proj.shape[1]
    nb = seq_len // rows
    nh = GLA_HEADS
    dk = LANES
    dv = 2 * LANES
    tok = lambda b, n: b * nb + n
    return pl.pallas_call(
        functools.partial(_gla_kernel, chunk=GLA_CHUNK),
        grid=(n_seq, nb),
        in_specs=[
            pl.BlockSpec((nh, rows, LANES), lambda b, n: (slab_q // nh, tok(b, n), 0)),
            pl.BlockSpec((nh, rows, LANES), lambda b, n: (slab_k // nh, tok(b, n), 0)),
            pl.BlockSpec((2 * nh, rows, LANES), lambda b, n: (slab_v // (2 * nh), tok(b, n), 0)),
            pl.BlockSpec((2 * nh, rows, LANES), lambda b, n: (slab_gz // (2 * nh), tok(b, n), 0)),
            pl.BlockSpec((rows, LANES), lambda b, n: (tok(b, n), 0)),
            pl.BlockSpec((LANES, nh * dk), lambda b, n: (0, 0)),
            pl.BlockSpec((1, nh * dk), lambda b, n: (0, 0)),
            pl.BlockSpec((1, dv), lambda b, n: (0, 0)),
        ],
        out_specs=pl.BlockSpec((2 * nh, rows, LANES), lambda b, n: (0, tok(b, n), 0)),
        out_shape=jax.ShapeDtypeStruct((2 * nh, m, LANES), BF16),
        scratch_shapes=[pltpu.VMEM((nh, dv, dk), F32)],
        compiler_params=pltpu.CompilerParams(
            dimension_semantics=("parallel", "arbitrary"),
            vmem_limit_bytes=VMEM_LIMIT),
        name="gla",
    )(proj, proj, proj, proj, g_low, gate_up_pad, gate_bias, norm_w)


def _out_proj_kernel(g_ref, z_ref, yg_ref, x_ref, gw_hbm, gb_ref, wo_hbm, pw_ref,
                     o_ref, gw_ref, wo_ref, stage_ref, sem, *, layer):
    n_s5 = g_ref.shape[0]
    d_s5 = n_s5 * LANES
    tm = x_ref.shape[0]
    rc = OUT_PROJ_ROW_CHUNK

    @pl.when(pl.program_id(0) == 0)
    def _():
        _stream_cast_rows(wo_hbm.at[layer], wo_ref, stage_ref, sem)
        _stream_cast_rows(gw_hbm.at[layer], gw_ref, stage_ref, sem)

    for c in range(tm // rc):
        rows = slice(c * rc, (c + 1) * rc)
        g = jnp.concatenate([g_ref[s, rows, :] for s in range(n_s5)], axis=1)
        gate = jax.nn.sigmoid(jnp.dot(g, gw_ref[...], preferred_element_type=F32) + gb_ref[...])
        z = jnp.concatenate([z_ref[s, rows, :] for s in range(n_s5)], axis=1).astype(F32)
        y_s5 = (g.astype(F32) * gate * (z * jax.nn.sigmoid(z))).astype(BF16)
        y_gla = jnp.concatenate([yg_ref[s, rows, :] for s in range(yg_ref.shape[0])], axis=1)
        mixed = (jnp.dot(y_s5, wo_ref[:d_s5, :], preferred_element_type=F32)
                 + jnp.dot(y_gla, wo_ref[d_s5:, :], preferred_element_type=F32))
        ms = jnp.mean(mixed * mixed, axis=-1, keepdims=True)
        o_ref[rows, :] = x_ref[rows, :] + mixed * lax.rsqrt(ms + EPS) * pw_ref[...]


def _out_proj(g_slabs, proj, y_gla, x2d, glu_w, glu_b, w_out, post_w, layer, *, tm, slab_z):
    m, d = x2d.shape
    n_s5 = g_slabs.shape[0]
    n_gla = y_gla.shape[0]
    d_s5 = n_s5 * LANES
    d_mix = w_out.shape[1]
    assert glu_w.dtype == w_out.dtype and glu_w.shape[2] <= d
    return pl.pallas_call(
        functools.partial(_out_proj_kernel, layer=layer),
        grid=(m // tm,),
        in_specs=[
            pl.BlockSpec((n_s5, tm, LANES), lambda i: (0, i, 0)),
            pl.BlockSpec((n_s5, tm, LANES), lambda i: (slab_z // n_s5, i, 0)),
            pl.BlockSpec((n_gla, tm, LANES), lambda i: (0, i, 0)),
            pl.BlockSpec((tm, d), lambda i: (i, 0)),
            pl.BlockSpec(memory_space=pl.ANY),
            pl.BlockSpec((1, d_s5), lambda i: (0, 0)),
            pl.BlockSpec(memory_space=pl.ANY),
            pl.BlockSpec((1, d), lambda i: (0, 0)),
        ],
        out_specs=pl.BlockSpec((tm, d), lambda i: (i, 0)),
        out_shape=jax.ShapeDtypeStruct((m, d), F32),
        scratch_shapes=[
            pltpu.VMEM((d_s5, d_s5), BF16),
            pltpu.VMEM((d_mix, d), BF16),
            pltpu.VMEM((2, OUT_PROJ_W_CHUNK, d), w_out.dtype),
            pltpu.SemaphoreType.DMA((2,)),
        ],
        compiler_params=pltpu.CompilerParams(
            dimension_semantics=("arbitrary",),
            vmem_limit_bytes=VMEM_LIMIT),
        name="out_proj",
    )(g_slabs, proj, y_gla, x2d, glu_w, glu_b, w_out, post_w)


def kernel(x, pre_norm_w, w_in, s5_A_re, s5_A_im, s5_B_re, s5_B_im, s5_C_re, s5_C_im, s5_D,
           s5_log_dt, s5_glu_w, s5_glu_b, gla_gate_up, gla_gate_bias, gla_norm_w, w_out,
           post_norm_w):
    bsz, seq_len, d_model = x.shape
    depth = w_in.shape[0]
    d_in = w_in.shape[2]
    d_s5 = s5_glu_w.shape[1]
    rank, d_gk = gla_gate_up.shape[1:]
    d_gv = GLA_HEADS * gla_norm_w.shape[1]
    m = bsz * seq_len
    d_main = 2 * d_s5 + 2 * d_gk + 2 * d_gv
    tn = d_s5
    assert d_s5 % LANES == 0 and d_gk == GLA_HEADS * LANES and d_gv == GLA_HEADS * 2 * LANES
    assert d_in == d_main + rank and rank <= LANES and d_main % tn == 0 and 2 * d_gk == tn
    assert seq_len % (S5_T * 8) == 0 and w_out.shape[1] == 2 * d_s5

    n_s5 = d_s5 // LANES
    slab_z = 0
    slab_q = n_s5
    slab_k = slab_q + d_gk // LANES
    slab_v = slab_k + d_gk // LANES
    slab_gz = slab_v + d_gv // LANES

    resid = x.astype(F32).reshape(m, d_model)
    for l in range(depth):
        kb, u_op, v_op, abar = _s5_ops(s5_A_re[l], s5_A_im[l], s5_B_re[l], s5_B_im[l],
                                       s5_C_re[l], s5_C_im[l], s5_log_dt[l])
        dtile = jnp.tile(s5_D[l].astype(F32).reshape(n_s5, 2, 1, LANES // 2), (1, 1, 1, S5_T))

        u2, proj, g_low = _in_proj(resid, pre_norm_w[l].astype(F32)[None],
                                   jnp.swapaxes(w_in, 1, 2), l, d_main=d_main, tm=512, tn=tn)

        g_slabs = _s5(u2, kb, u_op, v_op, abar, dtile, n_seq=bsz)

        gate_up_pad = jnp.pad(gla_gate_up[l].astype(BF16), ((0, LANES - rank), (0, 0)))
        y_gla = _gla(proj, g_low, gate_up_pad, gla_gate_bias[l].astype(F32)[None],
                     gla_norm_w[l].astype(F32)[None], n_seq=bsz, seq_len=seq_len, rows=512,
                     slab_q=slab_q, slab_k=slab_k, slab_v=slab_v, slab_gz=slab_gz)

        resid = _out_proj(g_slabs, proj, y_gla, resid, s5_glu_w, s5_glu_b[l].astype(F32)[None],
                          w_out, post_norm_w[l].astype(F32)[None], l, tm=512, slab_z=slab_z)
    return resid.reshape(bsz, seq_len, d_model).astype(x.dtype)
```

```python
import functools

import jax
import jax.numpy as jnp
from jax import lax
from jax.experimental import pallas as pl
from jax.experimental.pallas import tpu as pltpu

F32 = jnp.float32
BF16 = jnp.bfloat16

S5_GROUP = 16
GLA_HEADS = 4
GLA_TAU = 16.0
GLA_CHUNK = 64
EPS = 1e-6

LANES = 128
S5_T = 16
S5_UNFOLD_PITCH = 24
OUT_PROJ_ROW_CHUNK = 256
IN_PROJ_W_CHUNK = 256
IN_PROJ_W_SLOTS = 4
OUT_PROJ_W_CHUNK = 256
VMEM_LIMIT = 56 * 1024 * 1024


NT_DIMS = (((1,), (1,)), ((), ()))
TN_DIMS = (((0,), (0,)), ((), ()))


def _cmul(ar, ai, br, bi):
    return ar * br - ai * bi, ar * bi + ai * br


def _stream_cast_rows(src, dst_ref, stage_ref, sem):
    n_rows, width = src.shape
    chunk = stage_ref.shape[1]
    sizes = [chunk] * (n_rows // chunk) + ([n_rows % chunk] if n_rows % chunk else [])

    def copy(c):
        return pltpu.make_async_copy(
            src.at[pl.ds(c * chunk, sizes[c]), :],
            stage_ref.at[c % 2, pl.ds(0, sizes[c]), pl.ds(0, width)], sem.at[c % 2])

    for c in range(min(2, len(sizes))):
        copy(c).start()
    for c, rows in enumerate(sizes):
        copy(c).wait()
        dst_ref[c * chunk:c * chunk + rows, :width] = (
            stage_ref[c % 2, :rows, :width].astype(dst_ref.dtype))
        if c + 2 < len(sizes):
            copy(c + 2).start()


def _s5_ops_kernel(*refs, n_state):
    (a_re_ref, a_im_ref, ldt_ref, b_re_ref, b_im_ref, c_re_ref, c_im_ref,
     kb_ref, u_ref, v_ref, ab_ref) = refs
    nt = S5_T
    ns = nt // 2
    half = LANES // 2
    gps, p = a_re_ref.shape

    def terms(t):
        hi = t.astype(BF16).astype(F32)
        mid = (t - hi).astype(BF16).astype(F32)
        return hi, mid, t - hi - mid

    def select(t, onehot):
        return sum(lax.dot_general(x, onehot, TN_DIMS, preferred_element_type=F32)
                   for x in terms(t))

    def spread_rows(t):
        return sum(lax.dot_general(spread, x, TN_DIMS, preferred_element_type=F32)
                   for x in terms(t))

    group_of_lane = lax.broadcasted_iota(jnp.int32, (gps, LANES), 1) // S5_GROUP
    spread = jnp.where(group_of_lane == lax.broadcasted_iota(jnp.int32, (gps, LANES), 0), 1.0, 0.0)
    eye = jnp.where(lax.broadcasted_iota(jnp.int32, (LANES, LANES), 0)
                    == lax.broadcasted_iota(jnp.int32, (LANES, LANES), 1), 1.0, 0.0)
    dup = lambda t: jnp.concatenate([t, t], axis=1)
    are_c = jnp.minimum(a_re_ref[...], -1e-4)
    aim_c = a_im_ref[...]
    dt = jnp.exp(ldt_ref[...])
    mag = jnp.exp(are_c * dt)
    abr_c = mag * jnp.cos(aim_c * dt)
    abi_c = mag * jnp.sin(aim_c * dt)
    are, aim, abr, abi = [spread_rows(dup(t)) for t in (are_c, aim_c, abr_c, abi_c)]
    b_re, b_im = [dup(r[...].reshape(gps * S5_GROUP, p)) for r in (b_re_ref, b_im_ref)]
    nbr, nbi = [select(t, spread) for t in (abr_c, abi_c)]
    cr, ci = [select(r[...].reshape(gps * S5_GROUP, p), eye) for r in (c_re_ref, c_im_ref)]

    den = are * are + aim * aim
    nr = abr - 1.0
    fr = (nr * are + abi * aim) / den
    fi = (abi * are - nr * aim) / den
    xr, xi = _cmul(fr, fi, b_re, b_im)

    row = lax.broadcasted_iota(jnp.int32, (LANES, LANES), 0)
    lane = lax.broadcasted_iota(jnp.int32, (LANES, LANES), 1)
    own_half = ((row // S5_GROUP) % 2) == (lane // half)
    same_group = ((row % half) // S5_GROUP) == ((lane % half) // S5_GROUP)
    first_half = lane < half

    def split(t):
        hi = t.astype(BF16)
        return hi, (t - hi.astype(F32)).astype(BF16)

    c_stack = jnp.concatenate([cr, ci], axis=0)
    c_swap = pltpu.roll(c_stack, half, axis=1)
    c_hi, c_lo = split(jnp.concatenate([c_stack, c_swap], axis=1))
    c_terms = jnp.concatenate([c_hi, c_lo, c_hi], axis=0)
    kers, kers_swap, xs = [], [], []
    for tau in range(nt):
        l_hi, l_lo = split(jnp.where(first_half, xr, -xi))
        ker = jnp.dot(jnp.concatenate([l_hi, l_hi, l_lo], axis=1), c_terms,
                      preferred_element_type=F32)
        kers.append(ker[:, :LANES])
        kers_swap.append(ker[:, LANES:])
        xs.append((jnp.where(own_half, xr, 0.0).astype(BF16),
                   jnp.where(own_half, xi, 0.0).astype(BF16)))
        if tau + 1 < nt:
            xr, xi = _cmul(xr, xi, abr, abi)

    zero = jnp.zeros((half, LANES), F32)
    first_half_rows = lax.broadcasted_iota(jnp.int32, (half, LANES), 1) < half
    for hh in range(2):
        rows = slice(hh * half, (hh + 1) * half)
        low, high = (kers, kers_swap) if hh == 0 else (kers_swap, kers)
        pick = lambda src, tau: src[tau][rows, :] if tau >= 0 else zero
        for d in range(ns):
            top = jnp.where(first_half_rows, pick(low, 2 * d), pick(high, 2 * d + 1))
            bot = jnp.where(first_half_rows, pick(low, 2 * d - 1), pick(high, 2 * d))
            blk = jnp.concatenate([top, bot], axis=0)
            kb_ref[hh, d] = jnp.where(same_group, blk, 0.0).astype(BF16)
        for s2 in range(ns):
            for part in range(2):
                u_ref[hh, s2, part] = jnp.concatenate(
                    [xs[nt - 1 - 2 * s2][part][rows, :], xs[nt - 2 - 2 * s2][part][rows, :]], axis=0)

    pr, pi = abr, abi
    for _ in range(4):
        pr, pi = _cmul(pr, pi, pr, pi)
    assert nt == 16
    pairs = LANES // (2 * S5_GROUP)
    for part, val in enumerate((pr, pi)):
        for a in range(pairs):
            r0 = 2 * a * S5_GROUP
            piece = jnp.where(first_half[:1], val[r0:r0 + 1], val[r0 + S5_GROUP:r0 + S5_GROUP + 1])
            c0 = part * (n_state // 2) + a * LANES
            ab_ref[:, c0:c0 + LANES] = piece

    lane_n = lax.broadcasted_iota(jnp.int32, (half, LANES), 1)
    parity = (lane_n // S5_GROUP) % 2
    first_n = lane_n < half
    pr, pi = nbr, nbi
    outs = []
    for t in range(nt):
        wr, wi = _cmul(cr, ci, pr, pi)
        outs.append((wr, -wi))
        if t + 1 < nt:
            pr, pi = _cmul(pr, pi, nbr, nbi)
    for t2 in range(ns):
        for part in range(2):
            even, odd = outs[2 * t2][part], outs[2 * t2 + 1][part]
            packed = (jnp.where(first_n, even, pltpu.roll(odd, half, axis=1)),
                      jnp.where(first_n, pltpu.roll(even, half, axis=1), odd))
            for hh in range(2):
                for q in range(2):
                    v_ref[hh, t2, part, q * half:(q + 1) * half, :] = (
                        jnp.where(parity == q, packed[hh], 0.0).astype(BF16))


def _s5_ops(a_re, a_im, b_re, b_im, c_re, c_im, log_dt):
    g, p = a_re.shape
    h = S5_GROUP
    gps = LANES // h
    n_j = g // gps
    assert 2 * p == LANES and g % gps == 0
    n_state = 2 * gps * p
    f = lambda t: t.astype(F32)
    per_group = [f(t).reshape(n_j, gps, -1) for t in (a_re, a_im, log_dt)]
    per_chan = [f(t).reshape(n_j, gps, h, p)
                for t in (jnp.swapaxes(b_re, 1, 2), jnp.swapaxes(b_im, 1, 2), c_re, c_im)]
    ns = S5_T // 2
    return pl.pallas_call(
        functools.partial(_s5_ops_kernel, n_state=n_state),
        grid=(n_j,),
        in_specs=([pl.BlockSpec((None, gps, t.shape[-1]), lambda j: (j, 0, 0)) for t in per_group]
                  + [pl.BlockSpec((None, gps, h, p), lambda j: (j, 0, 0, 0))] * 4),
        out_specs=[
            pl.BlockSpec((None, 2, ns, LANES, LANES), lambda j: (j, 0, 0, 0, 0)),
            pl.BlockSpec((None, 2, ns, 2, LANES, LANES), lambda j: (j, 0, 0, 0, 0, 0)),
            pl.BlockSpec((None, 2, ns, 2, LANES, LANES), lambda j: (j, 0, 0, 0, 0, 0)),
            pl.BlockSpec((None, 1, n_state), lambda j: (j, 0, 0)),
        ],
        out_shape=[
            jax.ShapeDtypeStruct((n_j, 2, ns, LANES, LANES), BF16),
            jax.ShapeDtypeStruct((n_j, 2, ns, 2, LANES, LANES), BF16),
            jax.ShapeDtypeStruct((n_j, 2, ns, 2, LANES, LANES), BF16),
            jax.ShapeDtypeStruct((n_j, 1, n_state), F32),
        ],
        compiler_params=pltpu.CompilerParams(dimension_semantics=("parallel",)),
        name="s5_ops",
    )(*per_group, *per_chan)


def _in_proj_kernel(x_ref, nw_ref, w_hbm, u_ref, p_ref, l_ref, h_ref, s_ref, w_ref, stage_ref, sem,
                    *, layer, d_main, tn):
    tm = x_ref.shape[0]
    n_slabs = tn // LANES
    fold = tm // S5_T

    n_rows = w_hbm.shape[1]
    n_slots, chunk = stage_ref.shape[:2]
    sizes = [chunk] * (n_rows // chunk) + ([n_rows % chunk] if n_rows % chunk else [])
    per_dot = tn // chunk
    assert per_dot <= n_slots and tn % chunk == 0

    def copy(c):
        return pltpu.make_async_copy(
            w_hbm.at[layer, pl.ds(c * chunk, sizes[c]), :],
            stage_ref.at[c % n_slots, pl.ds(0, sizes[c]), :], sem.at[c % n_slots])

    def land(c):
        copy(c).wait()
        w_ref[c * chunk:c * chunk + sizes[c], :] = (
            stage_ref[c % n_slots, :sizes[c], :].astype(BF16))
        if c + n_slots < len(sizes):
            copy(c + n_slots).start()

    def project(after_dot):
        xf = x_ref[...]
        h_ref[...] = (xf * nw_ref[...]).astype(BF16)
        rs = lax.rsqrt(jnp.mean(xf * xf, axis=-1, keepdims=True) + EPS)
        for n in range(d_main // tn):
            res = lax.dot_general(h_ref[...], w_ref[n * tn:(n + 1) * tn, :], NT_DIMS,
                                  preferred_element_type=F32) * rs
            after_dot(n)
            if n == 0:
                for s in range(n_slabs):
                    s_ref[s] = res[:, s * LANES:(s + 1) * LANES]
                low = lax.broadcasted_iota(jnp.int32, (fold, LANES), 1) < LANES // 2
                for s in range(n_slabs):
                    for t2 in range(S5_T // 2):
                        even = s_ref[s, pl.ds(2 * t2, fold, stride=S5_T), :]
                        odd = s_ref[s, pl.ds(2 * t2 + 1, fold, stride=S5_T), :]
                        cols = slice(t2 * LANES, (t2 + 1) * LANES)
                        u_ref[2 * s, :, cols] = (
                            jnp.where(low, even, pltpu.roll(odd, LANES // 2, axis=1)).astype(BF16))
                        u_ref[2 * s + 1, :, cols] = (
                            jnp.where(low, pltpu.roll(even, LANES // 2, axis=1), odd).astype(BF16))
            else:
                for s in range(n_slabs):
                    p_ref[(n - 1) * n_slabs + s] = res[:, s * LANES:(s + 1) * LANES].astype(BF16)
        low_rank = lax.dot_general(h_ref[...], w_ref[d_main:d_main + LANES, :], NT_DIMS,
                                   preferred_element_type=F32) * rs
        l_ref[...] = low_rank.astype(BF16)

    @pl.when(pl.program_id(0) == 0)
    def _():
        w_ref[n_rows:, :] = jnp.zeros((w_ref.shape[0] - n_rows, w_ref.shape[1]), BF16)
        for c in range(min(n_slots, len(sizes))):
            copy(c).start()
        for c in range(per_dot):
            land(c)

        def stage_next(n):
            for c in range((n + 1) * per_dot, min((n + 2) * per_dot, len(sizes))):
                land(c)

        project(stage_next)

    @pl.when(pl.program_id(0) > 0)
    def _():
        project(lambda n: None)


def _in_proj(x2d, norm_w, w_t, layer, *, d_main, tm, tn):
    m, d = x2d.shape
    n_slabs = tn // LANES
    n_proj = (d_main // tn - 1) * n_slabs
    assert d_main < w_t.shape[1] <= d_main + LANES
    return pl.pallas_call(
        functools.partial(_in_proj_kernel, layer=layer, d_main=d_main, tn=tn),
        grid=(m // tm,),
        in_specs=[
            pl.BlockSpec((tm, d), lambda i: (i, 0)),
            pl.BlockSpec((1, d), lambda i: (0, 0)),
            pl.BlockSpec(memory_space=pl.ANY),
        ],
        out_specs=[
            pl.BlockSpec((2 * n_slabs, tm // S5_T, S5_T * LANES // 2), lambda i: (0, i, 0)),
            pl.BlockSpec((n_proj, tm, LANES), lambda i: (0, i, 0)),
            pl.BlockSpec((tm, LANES), lambda i: (i, 0)),
        ],
        out_shape=[
            jax.ShapeDtypeStruct((2 * n_slabs, m // S5_T, S5_T * LANES // 2), BF16),
            jax.ShapeDtypeStruct((n_proj, m, LANES), BF16),
            jax.ShapeDtypeStruct((m, LANES), BF16),
        ],
        scratch_shapes=[
            pltpu.VMEM((tm, d), BF16),
            pltpu.VMEM((n_slabs, tm, LANES), F32),
            pltpu.VMEM((d_main + LANES, d), BF16),
            pltpu.VMEM((IN_PROJ_W_SLOTS, IN_PROJ_W_CHUNK, d), w_t.dtype),
            pltpu.SemaphoreType.DMA((IN_PROJ_W_SLOTS,)),
        ],
        compiler_params=pltpu.CompilerParams(
            dimension_semantics=("arbitrary",),
            vmem_limit_bytes=VMEM_LIMIT),
        name="in_proj",
    )(x2d, norm_w, w_t)


def _s5_kernel(x_ref, kb_ref, u_ref, v_ref, a_ref, d_ref, o_ref,
               m_ref, wz_ref, wy_ref, z_ref, sp_ref, y_ref, *, rows_per_seq, n_seq):
    ns = S5_T // 2
    rows = x_ref.shape[1]
    half = a_ref.shape[-1] // 2
    hstate = half // 2
    pairs = hstate // LANES
    sub = LANES // (2 * pairs)
    wide = 2 * LANES

    @pl.when(pl.program_id(0) == 0)
    def _():
        m_ref[:, (ns - 1) * LANES:, :LANES] = jnp.zeros((2, LANES, LANES), BF16)
        wz_ref[...] = jnp.zeros_like(wz_ref)
        y_ref[...] = jnp.zeros_like(y_ref)

    lane = lax.broadcasted_iota(jnp.int32, (LANES, LANES), 1)
    for hh in range(2):
        for s in range(ns):
            for k in range(2):
                lag = ns - 2 + k - s
                if lag >= 0:
                    m_ref[hh, s * LANES:(s + 1) * LANES, k * LANES:(k + 1) * LANES] = kb_ref[hh, lag]
        for s in range(ns):
            for part in range(2):
                for rho in range(2):
                    for a in range(pairs):
                        r0 = rho * (LANES // 2) + a * sub
                        c0 = part * hstate + a * LANES
                        wz_ref[hh, s * LANES + r0:s * LANES + r0 + sub, c0:c0 + LANES] = (
                            u_ref[hh, s, part, r0:r0 + sub, :])
        for t in range(ns):
            for part in range(2):
                blk = v_ref[hh, t, part]
                for a in range(pairs):
                    r0 = part * hstate + a * LANES
                    wy_ref[hh, r0:r0 + LANES, t * LANES:(t + 1) * LANES] = (
                        jnp.where(((lane % (LANES // 2)) // sub) == a, blk, jnp.zeros_like(blk)))
        zh = jnp.dot(x_ref[hh], wz_ref[hh], preferred_element_type=F32)
        z_ref[:, hh * hstate:(hh + 1) * hstate] = zh[:, :hstate]
        z_ref[:, half + hh * hstate:half + (hh + 1) * hstate] = zh[:, hstate:]

    a_re = a_ref[:, :half]
    a_im = a_ref[:, half:]

    state = [(jnp.zeros((1, half), F32), jnp.zeros((1, half), F32)) for _ in range(n_seq)]
    for c in range(rows_per_seq):
        for b in range(n_seq):
            s_re, s_im = state[b]
            r = b * rows_per_seq + c
            sp_ref[r:r + 1, :half] = s_re
            sp_ref[r:r + 1, half:] = s_im
            z_re = z_ref[r:r + 1, :half]
            z_im = z_ref[r:r + 1, half:]
            state[b] = (a_re * s_re - a_im * s_im + z_re, a_re * s_im + a_im * s_re + z_im)

    spb = [jnp.concatenate([sp_ref[:, hh * hstate:(hh + 1) * hstate],
                            sp_ref[:, half + hh * hstate:half + (hh + 1) * hstate]],
                           axis=1).astype(BF16) for hh in range(2)]
    pitch = y_ref.shape[0] // rows
    low = lax.broadcasted_iota(jnp.int32, (rows, LANES), 1) < LANES // 2
    for n in range(ns // 2):
        cols = slice(n * wide, (n + 1) * wide)
        kk = (n + 1) * wide
        ys = []
        for hh in range(2):
            acc = jnp.dot(x_ref[hh, :, :kk], m_ref[hh, ns * LANES - kk:, :],
                          preferred_element_type=F32)
            acc = acc + jnp.dot(spb[hh], wy_ref[hh, :, cols], preferred_element_type=F32)
            acc = acc + d_ref[hh, :, cols] * x_ref[hh, :, cols].astype(F32)
            ys.append(jax.nn.gelu(acc))
        for k in range(2):
            y0 = ys[0][:, k * LANES:(k + 1) * LANES]
            y1 = ys[1][:, k * LANES:(k + 1) * LANES]
            t = 2 * (2 * n + k)
            y_ref[pl.ds(t, rows, stride=pitch), :] = (
                jnp.where(low, y0, pltpu.roll(y1, LANES // 2, axis=1)))
            y_ref[pl.ds(t + 1, rows, stride=pitch), :] = (
                jnp.where(low, pltpu.roll(y0, LANES // 2, axis=1), y1))
    tokens = y_ref[...].reshape(rows, pitch, LANES)[:, :S5_T, :].reshape(rows * S5_T, LANES)
    o_ref[...] = tokens.astype(BF16)


def _s5(u2, kb, u_op, v_op, abar, dtile, *, n_seq):
    n_j, _, ns, _, _ = kb.shape
    rows = u2.shape[1]
    width = ns * LANES
    n_state = abar.shape[-1]
    return pl.pallas_call(
        functools.partial(_s5_kernel, rows_per_seq=rows // n_seq, n_seq=n_seq),
        grid=(n_j,),
        in_specs=[
            pl.BlockSpec((2, rows, width), lambda j: (j, 0, 0)),
            pl.BlockSpec((None, 2, ns, LANES, LANES), lambda j: (j, 0, 0, 0, 0)),
            pl.BlockSpec((None, 2, ns, 2, LANES, LANES), lambda j: (j, 0, 0, 0, 0, 0)),
            pl.BlockSpec((None, 2, ns, 2, LANES, LANES), lambda j: (j, 0, 0, 0, 0, 0)),
            pl.BlockSpec((None, 1, n_state), lambda j: (j, 0, 0)),
            pl.BlockSpec((None, 2, 1, width), lambda j: (j, 0, 0, 0)),
        ],
        out_specs=pl.BlockSpec((None, rows * S5_T, LANES), lambda j: (j, 0, 0)),
        out_shape=jax.ShapeDtypeStruct((n_j, rows * S5_T, LANES), BF16),
        scratch_shapes=[
            pltpu.VMEM((2, width, 2 * LANES), BF16),
            pltpu.VMEM((2, width, n_state // 2), BF16),
            pltpu.VMEM((2, n_state // 2, width), BF16),
            pltpu.VMEM((rows, n_state), F32),
            pltpu.VMEM((rows, n_state), F32),
            pltpu.VMEM((rows * S5_UNFOLD_PITCH, LANES), F32),
        ],
        compiler_params=pltpu.CompilerParams(
            dimension_semantics=("arbitrary",),
            vmem_limit_bytes=VMEM_LIMIT),
        name="s5",
    )(u2, kb, u_op, v_op, abar, dtile)


def _log_sigmoid(x):
    return jnp.minimum(x, 0.0) - jnp.log(1.0 + jnp.exp(-jnp.abs(x)))


def _gla_kernel(q_ref, k_ref, v_ref, gz_ref, gl_ref, gu_ref, gb_ref, nw_ref, o_ref, st_ref,
                *, chunk):
    n_heads, rows, dk = q_ref.shape
    n_chunks = rows // chunk
    d_gk = n_heads * dk

    @pl.when(pl.program_id(1) == 0)
    def _():
        st_ref[...] = jnp.zeros_like(st_ref)

    zg = jnp.dot(gl_ref[...], gu_ref[...], preferred_element_type=F32) + gb_ref[...]
    lg = _log_sigmoid(zg) * (1.0 / GLA_TAU)

    hi = lg.astype(BF16)
    lo = (lg - hi.astype(F32)).astype(BF16)
    r_id = lax.broadcasted_iota(jnp.int32, (chunk, chunk), 0)
    c_id = lax.broadcasted_iota(jnp.int32, (chunk, chunk), 1)
    causal = c_id <= r_id
    tri = jnp.where(causal, 1.0, 0.0).astype(BF16)
    tri2 = jnp.concatenate([tri, tri], axis=1)
    b_parts, last_parts = [], []
    for n in range(n_chunks):
        rs = slice(n * chunk, (n + 1) * chunk)
        b_n = jnp.dot(tri2, jnp.concatenate([hi[rs], lo[rs]], axis=0), preferred_element_type=F32)
        b_parts.append(b_n)
        last_parts.append(jnp.broadcast_to(b_n[chunk - 1:chunk], (chunk, d_gk)))
    b = jnp.concatenate(b_parts, axis=0)
    e_pos = jnp.exp(b)
    e_neg = jnp.exp(-b)
    decay = jnp.exp(jnp.concatenate(last_parts, axis=0))

    for h in range(n_heads):
        hs = slice(h * dk, (h + 1) * dk)
        q_e = (q_ref[h].astype(F32) * ((dk ** -0.5) * e_pos[:, hs])).astype(BF16)
        k_f = k_ref[h].astype(F32) * e_neg[:, hs]
        k_e = k_f.astype(BF16)
        k_t = (k_f * decay[:, hs]).astype(BF16)
        v = jnp.concatenate([v_ref[2 * h], v_ref[2 * h + 1]], axis=1)

        chunks = [slice(n * chunk, (n + 1) * chunk) for n in range(n_chunks)]
        attn = [lax.dot_general(q_e[rs], k_e[rs], NT_DIMS, preferred_element_type=F32)
                for rs in chunks]
        kv_t = [lax.dot_general(v[rs], k_t[rs], TN_DIMS, preferred_element_type=F32)
                for rs in chunks]
        attn = [jnp.where(causal, a, 0.0).astype(BF16) for a in attn]
        st = st_ref[h]
        st_in = []
        for n in range(n_chunks):
            st_in.append(st.astype(BF16))
            st = decay[n * chunk:n * chunk + 1, hs] * st + kv_t[n]
        st_ref[h] = st
        outs = [jnp.dot(attn[n], v[rs], preferred_element_type=F32)
                + lax.dot_general(q_e[rs], st_in[n], NT_DIMS, preferred_element_type=F32)
                for n, rs in enumerate(chunks)]

        o = jnp.concatenate(outs, axis=0)
        ms = jnp.mean(o * o, axis=-1, keepdims=True)
        o = o * lax.rsqrt(ms + EPS) * nw_ref[...]
        gz = jnp.concatenate([gz_ref[2 * h], gz_ref[2 * h + 1]], axis=1).astype(F32)
        y = (o * (gz * jax.nn.sigmoid(gz))).astype(BF16)
        o_ref[2 * h] = y[:, :LANES]
        o_ref[2 * h + 1] = y[:, LANES:]


def _gla(proj, g_low, gate_up_pad, gate_bias, norm_w, *, n_seq, seq_len, rows, slab_q, slab_k,
         slab_v, slab_gz):
    m = proj.shape[1]
    nb = seq_len // rows
    nh = GLA_HEADS
    dk = LANES
    dv = 2 * LANES
    tok = lambda b, n: b * nb + n
    return pl.pallas_call(
        functools.partial(_gla_kernel, chunk=GLA_CHUNK),
        grid=(n_seq, nb),
        in_specs=[
            pl.BlockSpec((nh, rows, LANES), lambda b, n: (slab_q // nh, tok(b, n), 0)),
            pl.BlockSpec((nh, rows, LANES), lambda b, n: (slab_k // nh, tok(b, n), 0)),
            pl.BlockSpec((2 * nh, rows, LANES), lambda b, n: (slab_v // (2 * nh), tok(b, n), 0)),
            pl.BlockSpec((2 * nh, rows, LANES), lambda b, n: (slab_gz // (2 * nh), tok(b, n), 0)),
            pl.BlockSpec((rows, LANES), lambda b, n: (tok(b, n), 0)),
            pl.BlockSpec((LANES, nh * dk), lambda b, n: (0, 0)),
            pl.BlockSpec((1, nh * dk), lambda b, n: (0, 0)),
            pl.BlockSpec((1, dv), lambda b, n: (0, 0)),
        ],
        out_specs=pl.BlockSpec((2 * nh, rows, LANES), lambda b, n: (0, tok(b, n), 0)),
        out_shape=jax.ShapeDtypeStruct((2 * nh, m, LANES), BF16),
        scratch_shapes=[pltpu.VMEM((nh, dv, dk), F32)],
        compiler_params=pltpu.CompilerParams(
            dimension_semantics=("parallel", "arbitrary"),
            vmem_limit_bytes=VMEM_LIMIT),
        name="gla",
    )(proj, proj, proj, proj, g_low, gate_up_pad, gate_bias, norm_w)


def _out_proj_kernel(g_ref, z_ref, yg_ref, x_ref, gw_hbm, gb_ref, wo_hbm, pw_ref,
                     o_ref, gw_ref, wo_ref, stage_ref, sem, *, layer):
    n_s5 = g_ref.shape[0]
    d_s5 = n_s5 * LANES
    tm = x_ref.shape[0]
    rc = OUT_PROJ_ROW_CHUNK

    @pl.when(pl.program_id(0) == 0)
    def _():
        _stream_cast_rows(wo_hbm.at[layer], wo_ref, stage_ref, sem)
        _stream_cast_rows(gw_hbm.at[layer], gw_ref, stage_ref, sem)

    for c in range(tm // rc):
        rows = slice(c * rc, (c + 1) * rc)
        g = jnp.concatenate([g_ref[s, rows, :] for s in range(n_s5)], axis=1)
        gate = jax.nn.sigmoid(jnp.dot(g, gw_ref[...], preferred_element_type=F32) + gb_ref[...])
        z = jnp.concatenate([z_ref[s, rows, :] for s in range(n_s5)], axis=1).astype(F32)
        y_s5 = (g.astype(F32) * gate * (z * jax.nn.sigmoid(z))).astype(BF16)
        y_gla = jnp.concatenate([yg_ref[s, rows, :] for s in range(yg_ref.shape[0])], axis=1)
        mixed = (jnp.dot(y_s5, wo_ref[:d_s5, :], preferred_element_type=F32)
                 + jnp.dot(y_gla, wo_ref[d_s5:, :], preferred_element_type=F32))
        ms = jnp.mean(mixed * mixed, axis=-1, keepdims=True)
        o_ref[rows, :] = x_ref[rows, :] + mixed * lax.rsqrt(ms + EPS) * pw_ref[...]


def _out_proj(g_slabs, proj, y_gla, x2d, glu_w, glu_b, w_out, post_w, layer, *, tm, slab_z):
    m, d = x2d.shape
    n_s5 = g_slabs.shape[0]
    n_gla = y_gla.shape[0]
    d_s5 = n_s5 * LANES
    d_mix = w_out.shape[1]
    assert glu_w.dtype == w_out.dtype and glu_w.shape[2] <= d
    return pl.pallas_call(
        functools.partial(_out_proj_kernel, layer=layer),
        grid=(m // tm,),
        in_specs=[
            pl.BlockSpec((n_s5, tm, LANES), lambda i: (0, i, 0)),
            pl.BlockSpec((n_s5, tm, LANES), lambda i: (slab_z // n_s5, i, 0)),
            pl.BlockSpec((n_gla, tm, LANES), lambda i: (0, i, 0)),
            pl.BlockSpec((tm, d), lambda i: (i, 0)),
            pl.BlockSpec(memory_space=pl.ANY),
            pl.BlockSpec((1, d_s5), lambda i: (0, 0)),
            pl.BlockSpec(memory_space=pl.ANY),
            pl.BlockSpec((1, d), lambda i: (0, 0)),
        ],
        out_specs=pl.BlockSpec((tm, d), lambda i: (i, 0)),
        out_shape=jax.ShapeDtypeStruct((m, d), F32),
        scratch_shapes=[
            pltpu.VMEM((d_s5, d_s5), BF16),
            pltpu.VMEM((d_mix, d), BF16),
            pltpu.VMEM((2, OUT_PROJ_W_CHUNK, d), w_out.dtype),
            pltpu.SemaphoreType.DMA((2,)),
        ],
        compiler_params=pltpu.CompilerParams(
            dimension_semantics=("arbitrary",),
            vmem_limit_bytes=VMEM_LIMIT),
        name="out_proj",
    )(g_slabs, proj, y_gla, x2d, glu_w, glu_b, w_out, post_w)


def kernel(x, pre_norm_w, w_in, s5_A_re, s5_A_im, s5_B_re, s5_B_im, s5_C_re, s5_C_im, s5_D,
           s5_log_dt, s5_glu_w, s5_glu_b, gla_gate_up, gla_gate_bias, gla_norm_w, w_out,
           post_norm_w):
    bsz, seq_len, d_model = x.shape
    depth = w_in.shape[0]
    d_in = w_in.shape[2]
    d_s5 = s5_glu_w.shape[1]
    rank, d_gk = gla_gate_up.shape[1:]
    d_gv = GLA_HEADS * gla_norm_w.shape[1]
    m = bsz * seq_len
    d_main = 2 * d_s5 + 2 * d_gk + 2 * d_gv
    tn = d_s5
    assert d_s5 % LANES == 0 and d_gk == GLA_HEADS * LANES and d_gv == GLA_HEADS * 2 * LANES
    assert d_in == d_main + rank and rank <= LANES and d_main % tn == 0 and 2 * d_gk == tn
    assert seq_len % (S5_T * 8) == 0 and w_out.shape[1] == 2 * d_s5

    n_s5 = d_s5 // LANES
    slab_z = 0
    slab_q = n_s5
    slab_k = slab_q + d_gk // LANES
    slab_v = slab_k + d_gk // LANES
    slab_gz = slab_v + d_gv // LANES

    resid = x.astype(F32).reshape(m, d_model)
    for l in range(depth):
        kb, u_op, v_op, abar = _s5_ops(s5_A_re[l], s5_A_im[l], s5_B_re[l], s5_B_im[l],
                                       s5_C_re[l], s5_C_im[l], s5_log_dt[l])
        dtile = jnp.tile(s5_D[l].astype(F32).reshape(n_s5, 2, 1, LANES // 2), (1, 1, 1, S5_T))

        u2, proj, g_low = _in_proj(resid, pre_norm_w[l].astype(F32)[None],
                                   jnp.swapaxes(w_in, 1, 2), l, d_main=d_main, tm=512, tn=tn)

        g_slabs = _s5(u2, kb, u_op, v_op, abar, dtile, n_seq=bsz)

        gate_up_pad = jnp.pad(gla_gate_up[l].astype(BF16), ((0, LANES - rank), (0, 0)))
        y_gla = _gla(proj, g_low, gate_up_pad, gla_gate_bias[l].astype(F32)[None],
                     gla_norm_w[l].astype(F32)[None], n_seq=bsz, seq_len=seq_len, rows=512,
                     slab_q=slab_q, slab_k=slab_k, slab_v=slab_v, slab_gz=slab_gz)

        resid = _out_proj(g_slabs, proj, y_gla, resid, s5_glu_w, s5_glu_b[l].astype(F32)[None],
                          w_out, post_norm_w[l].astype(F32)[None], l, tm=512, slab_z=slab_z)
    return resid.reshape(bsz, seq_len, d_model).astype(x.dtype)
```

```python
import functools

import jax
import jax.numpy as jnp
from jax import lax
from jax.experimental import pallas as pl
from jax.experimental.pallas import tpu as pltpu

F32 = jnp.float32
BF16 = jnp.bfloat16

S5_GROUP = 16
GLA_HEADS = 4
GLA_TAU = 16.0
GLA_CHUNK = 64
EPS = 1e-6

LANES = 128
S5_T = 16
S5_UNFOLD_PITCH = 24
OUT_PROJ_ROW_CHUNK = 256
IN_PROJ_W_CHUNK = 256
IN_PROJ_W_SLOTS = 4
OUT_PROJ_W_CHUNK = 256
VMEM_LIMIT = 56 * 1024 * 1024


NT_DIMS = (((1,), (1,)), ((), ()))
TN_DIMS = (((0,), (0,)), ((), ()))


def _cmul(ar, ai, br, bi):
    return ar * br - ai * bi, ar * bi + ai * br


def _stream_cast_rows(src, dst_ref, stage_ref, sem):
    n_rows, width = src.shape
    chunk = stage_ref.shape[1]
    sizes = [chunk] * (n_rows // chunk) + ([n_rows % chunk] if n_rows % chunk else [])

    def copy(c):
        return pltpu.make_async_copy(
            src.at[pl.ds(c * chunk, sizes[c]), :],
            stage_ref.at[c % 2, pl.ds(0, sizes[c]), pl.ds(0, width)], sem.at[c % 2])

    for c in range(min(2, len(sizes))):
        copy(c).start()
    for c, rows in enumerate(sizes):
        copy(c).wait()
        dst_ref[c * chunk:c * chunk + rows, :width] = (
            stage_ref[c % 2, :rows, :width].astype(dst_ref.dtype))
        if c + 2 < len(sizes):
            copy(c + 2).start()


def _s5_ops_kernel(*refs, n_state):
    (a_re_ref, a_im_ref, ldt_ref, b_re_ref, b_im_ref, c_re_ref, c_im_ref,
     kb_ref, u_ref, v_ref, ab_ref) = refs
    nt = S5_T
    ns = nt // 2
    half = LANES // 2
    gps, p = a_re_ref.shape

    def terms(t):
        hi = t.astype(BF16).astype(F32)
        mid = (t - hi).astype(BF16).astype(F32)
        return hi, mid, t - hi - mid

    def select(t, onehot):
        return sum(lax.dot_general(x, onehot, TN_DIMS, preferred_element_type=F32)
                   for x in terms(t))

    def spread_rows(t):
        return sum(lax.dot_general(spread, x, TN_DIMS, preferred_element_type=F32)
                   for x in terms(t))

    group_of_lane = lax.broadcasted_iota(jnp.int32, (gps, LANES), 1) // S5_GROUP
    spread = jnp.where(group_of_lane == lax.broadcasted_iota(jnp.int32, (gps, LANES), 0), 1.0, 0.0)
    eye = jnp.where(lax.broadcasted_iota(jnp.int32, (LANES, LANES), 0)
                    == lax.broadcasted_iota(jnp.int32, (LANES, LANES), 1), 1.0, 0.0)
    dup = lambda t: jnp.concatenate([t, t], axis=1)
    are_c = jnp.minimum(a_re_ref[...], -1e-4)
    aim_c = a_im_ref[...]
    dt = jnp.exp(ldt_ref[...])
    mag = jnp.exp(are_c * dt)
    abr_c = mag * jnp.cos(aim_c * dt)
    abi_c = mag * jnp.sin(aim_c * dt)
    are, aim, abr, abi = [spread_rows(dup(t)) for t in (are_c, aim_c, abr_c, abi_c)]
    b_re, b_im = [dup(r[...].reshape(gps * S5_GROUP, p)) for r in (b_re_ref, b_im_ref)]
    nbr, nbi = [select(t, spread) for t in (abr_c, abi_c)]
    cr, ci = [select(r[...].reshape(gps * S5_GROUP, p), eye) for r in (c_re_ref, c_im_ref)]

    den = are * are + aim * aim
    nr = abr - 1.0
    fr = (nr * are + abi * aim) / den
    fi = (abi * are - nr * aim) / den
    xr, xi = _cmul(fr, fi, b_re, b_im)

    row = lax.broadcasted_iota(jnp.int32, (LANES, LANES), 0)
    lane = lax.broadcasted_iota(jnp.int32, (LANES, LANES), 1)
    own_half = ((row // S5_GROUP) % 2) == (lane // half)
    same_group = ((row % half) // S5_GROUP) == ((lane % half) // S5_GROUP)
    first_half = lane < half

    def split(t):
        hi = t.astype(BF16)
        return hi, (t - hi.astype(F32)).astype(BF16)

    c_stack = jnp.concatenate([cr, ci], axis=0)
    c_swap = pltpu.roll(c_stack, half, axis=1)
    c_hi, c_lo = split(jnp.concatenate([c_stack, c_swap], axis=1))
    c_terms = jnp.concatenate([c_hi, c_lo, c_hi], axis=0)
    kers, kers_swap, xs = [], [], []
    for tau in range(nt):
        l_hi, l_lo = split(jnp.where(first_half, xr, -xi))
        ker = jnp.dot(jnp.concatenate([l_hi, l_hi, l_lo], axis=1), c_terms,
                      preferred_element_type=F32)
        kers.append(ker[:, :LANES])
        kers_swap.append(ker[:, LANES:])
        xs.append((jnp.where(own_half, xr, 0.0).astype(BF16),
                   jnp.where(own_half, xi, 0.0).astype(BF16)))
        if tau + 1 < nt:
            xr, xi = _cmul(xr, xi, abr, abi)

    zero = jnp.zeros((half, LANES), F32)
    first_half_rows = lax.broadcasted_iota(jnp.int32, (half, LANES), 1) < half
    for hh in range(2):
        rows = slice(hh * half, (hh + 1) * half)
        low, high = (kers, kers_swap) if hh == 0 else (kers_swap, kers)
        pick = lambda src, tau: src[tau][rows, :] if tau >= 0 else zero
        for d in range(ns):
            top = jnp.where(first_half_rows, pick(low, 2 * d), pick(high, 2 * d + 1))
            bot = jnp.where(first_half_rows, pick(low, 2 * d - 1), pick(high, 2 * d))
            blk = jnp.concatenate([top, bot], axis=0)
            kb_ref[hh, d] = jnp.where(same_group, blk, 0.0).astype(BF16)
        for s2 in range(ns):
            for part in range(2):
                u_ref[hh, s2, part] = jnp.concatenate(
                    [xs[nt - 1 - 2 * s2][part][rows, :], xs[nt - 2 - 2 * s2][part][rows, :]], axis=0)

    pr, pi = abr, abi
    for _ in range(4):
        pr, pi = _cmul(pr, pi, pr, pi)
    assert nt == 16
    pairs = LANES // (2 * S5_GROUP)
    for part, val in enumerate((pr, pi)):
        for a in range(pairs):
            r0 = 2 * a * S5_GROUP
            piece = jnp.where(first_half[:1], val[r0:r0 + 1], val[r0 + S5_GROUP:r0 + S5_GROUP + 1])
            c0 = part * (n_state // 2) + a * LANES
            ab_ref[:, c0:c0 + LANES] = piece

    lane_n = lax.broadcasted_iota(jnp.int32, (half, LANES), 1)
    parity = (lane_n // S5_GROUP) % 2
    first_n = lane_n < half
    pr, pi = nbr, nbi
    outs = []
    for t in range(nt):
        wr, wi = _cmul(cr, ci, pr, pi)
        outs.append((wr, -wi))
        if t + 1 < nt:
            pr, pi = _cmul(pr, pi, nbr, nbi)
    for t2 in range(ns):
        for part in range(2):
            even, odd = outs[2 * t2][part], outs[2 * t2 + 1][part]
            packed = (jnp.where(first_n, even, pltpu.roll(odd, half, axis=1)),
                      jnp.where(first_n, pltpu.roll(even, half, axis=1), odd))
            for hh in range(2):
                for q in range(2):
                    v_ref[hh, t2, part, q * half:(q + 1) * half, :] = (
                        jnp.where(parity == q, packed[hh], 0.0).astype(BF16))


def _s5_ops(a_re, a_im, b_re, b_im, c_re, c_im, log_dt):
    g, p = a_re.shape
    h = S5_GROUP
    gps = LANES // h
    n_j = g // gps
    assert 2 * p == LANES and g % gps == 0
    n_state = 2 * gps * p
    f = lambda t: t.astype(F32)
    per_group = [f(t).reshape(n_j, gps, -1) for t in (a_re, a_im, log_dt)]
    per_chan = [f(t).reshape(n_j, gps, h, p)
                for t in (jnp.swapaxes(b_re, 1, 2), jnp.swapaxes(b_im, 1, 2), c_re, c_im)]
    ns = S5_T // 2
    return pl.pallas_call(
        functools.partial(_s5_ops_kernel, n_state=n_state),
        grid=(n_j,),
        in_specs=([pl.BlockSpec((None, gps, t.shape[-1]), lambda j: (j, 0, 0)) for t in per_group]
                  + [pl.BlockSpec((None, gps, h, p), lambda j: (j, 0, 0, 0))] * 4),
        out_specs=[
            pl.BlockSpec((None, 2, ns, LANES, LANES), lambda j: (j, 0, 0, 0, 0)),
            pl.BlockSpec((None, 2, ns, 2, LANES, LANES), lambda j: (j, 0, 0, 0, 0, 0)),
            pl.BlockSpec((None, 2, ns, 2, LANES, LANES), lambda j: (j, 0, 0, 0, 0, 0)),
            pl.BlockSpec((None, 1, n_state), lambda j: (j, 0, 0)),
        ],
        out_shape=[
            jax.ShapeDtypeStruct((n_j, 2, ns, LANES, LANES), BF16),
            jax.ShapeDtypeStruct((n_j, 2, ns, 2, LANES, LANES), BF16),
            jax.ShapeDtypeStruct((n_j, 2, ns, 2, LANES, LANES), BF16),
            jax.ShapeDtypeStruct((n_j, 1, n_state), F32),
        ],
        compiler_params=pltpu.CompilerParams(dimension_semantics=("parallel",)),
        name="s5_ops",
    )(*per_group, *per_chan)


def _in_proj_kernel(x_ref, nw_ref, w_hbm, u_ref, p_ref, l_ref, h_ref, s_ref, w_ref, stage_ref, sem,
                    *, layer, d_main, tn):
    tm = x_ref.shape[0]
    n_slabs = tn // LANES
    fold = tm // S5_T

    n_rows = w_hbm.shape[1]
    n_slots, chunk = stage_ref.shape[:2]
    sizes = [chunk] * (n_rows // chunk) + ([n_rows % chunk] if n_rows % chunk else [])
    per_dot = tn // chunk
    assert per_dot <= n_slots and tn % chunk == 0

    def copy(c):
        return pltpu.make_async_copy(
            w_hbm.at[layer, pl.ds(c * chunk, sizes[c]), :],
            stage_ref.at[c % n_slots, pl.ds(0, sizes[c]), :], sem.at[c % n_slots])

    def land(c):
        copy(c).wait()
        w_ref[c * chunk:c * chunk + sizes[c], :] = (
            stage_ref[c % n_slots, :sizes[c], :].astype(BF16))
        if c + n_slots < len(sizes):
            copy(c + n_slots).start()

    def project(after_dot):
        xf = x_ref[...]
        h_ref[...] = (xf * nw_ref[...]).astype(BF16)
        rs = lax.rsqrt(jnp.mean(xf * xf, axis=-1, keepdims=True) + EPS)
        for n in range(d_main // tn):
            res = lax.dot_general(h_ref[...], w_ref[n * tn:(n + 1) * tn, :], NT_DIMS,
                                  preferred_element_type=F32) * rs
            after_dot(n)
            if n == 0:
                for s in range(n_slabs):
                    s_ref[s] = res[:, s * LANES:(s + 1) * LANES]
                low = lax.broadcasted_iota(jnp.int32, (fold, LANES), 1) < LANES // 2
                for s in range(n_slabs):
                    for t2 in range(S5_T // 2):
                        even = s_ref[s, pl.ds(2 * t2, fold, stride=S5_T), :]
                        odd = s_ref[s, pl.ds(2 * t2 + 1, fold, stride=S5_T), :]
                        cols = slice(t2 * LANES, (t2 + 1) * LANES)
                        u_ref[2 * s, :, cols] = (
                            jnp.where(low, even, pltpu.roll(odd, LANES // 2, axis=1)).astype(BF16))
                        u_ref[2 * s + 1, :, cols] = (
                            jnp.where(low, pltpu.roll(even, LANES // 2, axis=1), odd).astype(BF16))
            else:
                for s in range(n_slabs):
                    p_ref[(n - 1) * n_slabs + s] = res[:, s * LANES:(s + 1) * LANES].astype(BF16)
        low_rank = lax.dot_general(h_ref[...], w_ref[d_main:d_main + LANES, :], NT_DIMS,
                                   preferred_element_type=F32) * rs
        l_ref[...] = low_rank.astype(BF16)

    @pl.when(pl.program_id(0) == 0)
    def _():
        w_ref[n_rows:, :] = jnp.zeros((w_ref.shape[0] - n_rows, w_ref.shape[1]), BF16)
        for c in range(min(n_slots, len(sizes))):
            copy(c).start()
        for c in range(per_dot):
            land(c)

        def stage_next(n):
            for c in range((n + 1) * per_dot, min((n + 2) * per_dot, len(sizes))):
                land(c)

        project(stage_next)

    @pl.when(pl.program_id(0) > 0)
    def _():
        project(lambda n: None)


def _in_proj(x2d, norm_w, w_t, layer, *, d_main, tm, tn):
    m, d = x2d.shape
    n_slabs = tn // LANES
    n_proj = (d_main // tn - 1) * n_slabs
    assert d_main < w_t.shape[1] <= d_main + LANES
    return pl.pallas_call(
        functools.partial(_in_proj_kernel, layer=layer, d_main=d_main, tn=tn),
        grid=(m // tm,),
        in_specs=[
            pl.BlockSpec((tm, d), lambda i: (i, 0)),
            pl.BlockSpec((1, d), lambda i: (0, 0)),
            pl.BlockSpec(memory_space=pl.ANY),
        ],
        out_specs=[
            pl.BlockSpec((2 * n_slabs, tm // S5_T, S5_T * LANES // 2), lambda i: (0, i, 0)),
            pl.BlockSpec((n_proj, tm, LANES), lambda i: (0, i, 0)),
            pl.BlockSpec((tm, LANES), lambda i: (i, 0)),
        ],
        out_shape=[
            jax.ShapeDtypeStruct((2 * n_slabs, m // S5_T, S5_T * LANES // 2), BF16),
            jax.ShapeDtypeStruct((n_proj, m, LANES), BF16),
            jax.ShapeDtypeStruct((m, LANES), BF16),
        ],
        scratch_shapes=[
            pltpu.VMEM((tm, d), BF16),
            pltpu.VMEM((n_slabs, tm, LANES), F32),
            pltpu.VMEM((d_main + LANES, d), BF16),
            pltpu.VMEM((IN_PROJ_W_SLOTS, IN_PROJ_W_CHUNK, d), w_t.dtype),
            pltpu.SemaphoreType.DMA((IN_PROJ_W_SLOTS,)),
        ],
        compiler_params=pltpu.CompilerParams(
            dimension_semantics=("arbitrary",),
            vmem_limit_bytes=VMEM_LIMIT),
        name="in_proj",
    )(x2d, norm_w, w_t)


def _s5_kernel(x_ref, kb_ref, u_ref, v_ref, a_ref, d_ref, o_ref,
               m_ref, wz_ref, wy_ref, z_ref, sp_ref, y_ref, *, rows_per_seq, n_seq):
    ns = S5_T // 2
    rows = x_ref.shape[1]
    half = a_ref.shape[-1] // 2
    hstate = half // 2
    pairs = hstate // LANES
    sub = LANES // (2 * pairs)
    wide = 2 * LANES

    @pl.when(pl.program_id(0) == 0)
    def _():
        m_ref[:, (ns - 1) * LANES:, :LANES] = jnp.zeros((2, LANES, LANES), BF16)
        wz_ref[...] = jnp.zeros_like(wz_ref)
        y_ref[...] = jnp.zeros_like(y_ref)

    lane = lax.broadcasted_iota(jnp.int32, (LANES, LANES), 1)
    for hh in range(2):
        for s in range(ns):
            for k in range(2):
                lag = ns - 2 + k - s
                if lag >= 0:
                    m_ref[hh, s * LANES:(s + 1) * LANES, k * LANES:(k + 1) * LANES] = kb_ref[hh, lag]
        for s in range(ns):
            for part in range(2):
                for rho in range(2):
                    for a in range(pairs):
                        r0 = rho * (LANES // 2) + a * sub
                        c0 = part * hstate + a * LANES
                        wz_ref[hh, s * LANES + r0:s * LANES + r0 + sub, c0:c0 + LANES] = (
                            u_ref[hh, s, part, r0:r0 + sub, :])
        for t in range(ns):
            for part in range(2):
                blk = v_ref[hh, t, part]
                for a in range(pairs):
                    r0 = part * hstate + a * LANES
                    wy_ref[hh, r0:r0 + LANES, t * LANES:(t + 1) * LANES] = (
                        jnp.where(((lane % (LANES // 2)) // sub) == a, blk, jnp.zeros_like(blk)))
        zh = jnp.dot(x_ref[hh], wz_ref[hh], preferred_element_type=F32)
        z_ref[:, hh * hstate:(hh + 1) * hstate] = zh[:, :hstate]
        z_ref[:, half + hh * hstate:half + (hh + 1) * hstate] = zh[:, hstate:]

    a_re = a_ref[:, :half]
    a_im = a_ref[:, half:]

    state = [(jnp.zeros((1, half), F32), jnp.zeros((1, half), F32)) for _ in range(n_seq)]
    for c in range(rows_per_seq):
        for b in range(n_seq):
            s_re, s_im = state[b]
            r = b * rows_per_seq + c
            sp_ref[r:r + 1, :half] = s_re
            sp_ref[r:r + 1, half:] = s_im
            z_re = z_ref[r:r + 1, :half]
            z_im = z_ref[r:r + 1, half:]
            state[b] = (a_re * s_re - a_im * s_im + z_re, a_re * s_im + a_im * s_re + z_im)

    spb = [jnp.concatenate([sp_ref[:, hh * hstate:(hh + 1) * hstate],
                            sp_ref[:, half + hh * hstate:half + (hh + 1) * hstate]],
                           axis=1).astype(BF16) for hh in range(2)]
    pitch = y_ref.shape[0] // rows
    low = lax.broadcasted_iota(jnp.int32, (rows, LANES), 1) < LANES // 2
    for n in range(ns // 2):
        cols = slice(n * wide, (n + 1) * wide)
        kk = (n + 1) * wide
        ys = []
        for hh in range(2):
            acc = jnp.dot(x_ref[hh, :, :kk], m_ref[hh, ns * LANES - kk:, :],
                          preferred_element_type=F32)
            acc = acc + jnp.dot(spb[hh], wy_ref[hh, :, cols], preferred_element_type=F32)
            acc = acc + d_ref[hh, :, cols] * x_ref[hh, :, cols].astype(F32)
            ys.append(jax.nn.gelu(acc))
        for k in range(2):
            y0 = ys[0][:, k * LANES:(k + 1) * LANES]
            y1 = ys[1][:, k * LANES:(k + 1) * LANES]
            t = 2 * (2 * n + k)
            y_ref[pl.ds(t, rows, stride=pitch), :] = (
                jnp.where(low, y0, pltpu.roll(y1, LANES // 2, axis=1)))
            y_ref[pl.ds(t + 1, rows, stride=pitch), :] = (
                jnp.where(low, pltpu.roll(y0, LANES // 2, axis=1), y1))
    tokens = y_ref[...].reshape(rows, pitch, LANES)[:, :S5_T, :].reshape(rows * S5_T, LANES)
    o_ref[...] = tokens.astype(BF16)


def _s5(u2, kb, u_op, v_op, abar, dtile, *, n_seq):
    n_j, _, ns, _, _ = kb.shape
    rows = u2.shape[1]
    width = ns * LANES
    n_state = abar.shape[-1]
    return pl.pallas_call(
        functools.partial(_s5_kernel, rows_per_seq=rows // n_seq, n_seq=n_seq),
        grid=(n_j,),
        in_specs=[
            pl.BlockSpec((2, rows, width), lambda j: (j, 0, 0)),
            pl.BlockSpec((None, 2, ns, LANES, LANES), lambda j: (j, 0, 0, 0, 0)),
            pl.BlockSpec((None, 2, ns, 2, LANES, LANES), lambda j: (j, 0, 0, 0, 0, 0)),
            pl.BlockSpec((None, 2, ns, 2, LANES, LANES), lambda j: (j, 0, 0, 0, 0, 0)),
            pl.BlockSpec((None, 1, n_state), lambda j: (j, 0, 0)),
            pl.BlockSpec((None, 2, 1, width), lambda j: (j, 0, 0, 0)),
        ],
        out_specs=pl.BlockSpec((None, rows * S5_T, LANES), lambda j: (j, 0, 0)),
        out_shape=jax.ShapeDtypeStruct((n_j, rows * S5_T, LANES), BF16),
        scratch_shapes=[
            pltpu.VMEM((2, width, 2 * LANES), BF16),
            pltpu.VMEM((2, width, n_state // 2), BF16),
            pltpu.VMEM((2, n_state // 2, width), BF16),
            pltpu.VMEM((rows, n_state), F32),
            pltpu.VMEM((rows, n_state), F32),
            pltpu.VMEM((rows * S5_UNFOLD_PITCH, LANES), F32),
        ],
        compiler_params=pltpu.CompilerParams(
            dimension_semantics=("arbitrary",),
            vmem_limit_bytes=VMEM_LIMIT),
        name="s5",
    )(u2, kb, u_op, v_op, abar, dtile)


def _log_sigmoid(x):
    return jnp.minimum(x, 0.0) - jnp.log(1.0 + jnp.exp(-jnp.abs(x)))


def _gla_kernel(q_ref, k_ref, v_ref, gz_ref, gl_ref, gu_ref, gb_ref, nw_ref, o_ref, st_ref,
                *, chunk):
    n_heads, rows, dk = q_ref.shape
    n_chunks = rows // chunk
    d_gk = n_heads * dk

    @pl.when(pl.program_id(1) == 0)
    def _():
        st_ref[...] = jnp.zeros_like(st_ref)

    zg = jnp.dot(gl_ref[...], gu_ref[...], preferred_element_type=F32) + gb_ref[...]
    lg = _log_sigmoid(zg) * (1.0 / GLA_TAU)

    hi = lg.astype(BF16)
    lo = (lg - hi.astype(F32)).astype(BF16)
    r_id = lax.broadcasted_iota(jnp.int32, (chunk, chunk), 0)
    c_id = lax.broadcasted_iota(jnp.int32, (chunk, chunk), 1)
    causal = c_id <= r_id
    tri = jnp.where(causal, 1.0, 0.0).astype(BF16)
    tri2 = jnp.concatenate([tri, tri], axis=1)
    b_parts, last_parts = [], []
    for n in range(n_chunks):
        rs = slice(n * chunk, (n + 1) * chunk)
        b_n = jnp.dot(tri2, jnp.concatenate([hi[rs], lo[rs]], axis=0), preferred_element_type=F32)
        b_parts.append(b_n)
        last_parts.append(jnp.broadcast_to(b_n[chunk - 1:chunk], (chunk, d_gk)))
    b = jnp.concatenate(b_parts, axis=0)
    e_pos = jnp.exp(b)
    e_neg = jnp.exp(-b)
    decay = jnp.exp(jnp.concatenate(last_parts, axis=0))

    for h in range(n_heads):
        hs = slice(h * dk, (h + 1) * dk)
        q_e = (q_ref[h].astype(F32) * ((dk ** -0.5) * e_pos[:, hs])).astype(BF16)
        k_f = k_ref[h].astype(F32) * e_neg[:, hs]
        k_e = k_f.astype(BF16)
        k_t = (k_f * decay[:, hs]).astype(BF16)
        v = jnp.concatenate([v_ref[2 * h], v_ref[2 * h + 1]], axis=1)

        chunks = [slice(n * chunk, (n + 1) * chunk) for n in range(n_chunks)]
        attn = [lax.dot_general(q_e[rs], k_e[rs], NT_DIMS, preferred_element_type=F32)
                for rs in chunks]
        kv_t = [lax.dot_general(v[rs], k_t[rs], TN_DIMS, preferred_element_type=F32)
                for rs in chunks]
        attn = [jnp.where(causal, a, 0.0).astype(BF16) for a in attn]
        st = st_ref[h]
        st_in = []
        for n in range(n_chunks):
            st_in.append(st.astype(BF16))
            st = decay[n * chunk:n * chunk + 1, hs] * st + kv_t[n]
        st_ref[h] = st
        outs = [jnp.dot(attn[n], v[rs], preferred_element_type=F32)
                + lax.dot_general(q_e[rs], st_in[n], NT_DIMS, preferred_element_type=F32)
                for n, rs in enumerate(chunks)]

        o = jnp.concatenate(outs, axis=0)
        ms = jnp.mean(o * o, axis=-1, keepdims=True)
        o = o * lax.rsqrt(ms + EPS) * nw_ref[...]
        gz = jnp.concatenate([gz_ref[2 * h], gz_ref[2 * h + 1]], axis=1).astype(F32)
        y = (o * (gz * jax.nn.sigmoid(gz))).astype(BF16)
        o_ref[2 * h] = y[:, :LANES]
        o_ref[2 * h + 1] = y[:, LANES:]


def _gla(proj, g_low, gate_up_pad, gate_bias, norm_w, *, n_seq, seq_len, rows, slab_q, slab_k,
         slab_v, slab_gz):
    m = proj.shape[1]
    nb = seq_len // rows
    nh = GLA_HEADS
    dk = LANES
    dv = 2 * LANES
    tok = lambda b, n: b * nb + n
    return pl.pallas_call(
        functools.partial(_gla_kernel, chunk=GLA_CHUNK),
        grid=(n_seq, nb),
        in_specs=[
            pl.BlockSpec((nh, rows, LANES), lambda b, n: (slab_q // nh, tok(b, n), 0)),
            pl.BlockSpec((nh, rows, LANES), lambda b, n: (slab_k // nh, tok(b, n), 0)),
            pl.BlockSpec((2 * nh, rows, LANES), lambda b, n: (slab_v // (2 * nh), tok(b, n), 0)),
            pl.BlockSpec((2 * nh, rows, LANES), lambda b, n: (slab_gz // (2 * nh), tok(b, n), 0)),
            pl.BlockSpec((rows, LANES), lambda b, n: (tok(b, n), 0)),
            pl.BlockSpec((LANES, nh * dk), lambda b, n: (0, 0)),
            pl.BlockSpec((1, nh * dk), lambda b, n: (0, 0)),
            pl.BlockSpec((1, dv), lambda b, n: (0, 0)),
        ],
        out_specs=pl.BlockSpec((2 * nh, rows, LANES), lambda b, n: (0, tok(b, n), 0)),
        out_shape=jax.ShapeDtypeStruct((2 * nh, m, LANES), BF16),
        scratch_shapes=[pltpu.VMEM((nh, dv, dk), F32)],
        compiler_params=pltpu.CompilerParams(
            dimension_semantics=("parallel", "arbitrary"),
            vmem_limit_bytes=VMEM_LIMIT),
        name="gla",
    )(proj, proj, proj, proj, g_low, gate_up_pad, gate_bias, norm_w)


def _out_proj_kernel(g_ref, z_ref, yg_ref, x_ref, gw_hbm, gb_ref, wo_hbm, pw_ref,
                     o_ref, gw_ref, wo_ref, stage_ref, sem, *, layer):
    n_s5 = g_ref.shape[0]
    d_s5 = n_s5 * LANES
    tm = x_ref.shape[0]
    rc = OUT_PROJ_ROW_CHUNK

    @pl.when(pl.program_id(0) == 0)
    def _():
        _stream_cast_rows(wo_hbm.at[layer], wo_ref, stage_ref, sem)
        _stream_cast_rows(gw_hbm.at[layer], gw_ref, stage_ref, sem)

    for c in range(tm // rc):
        rows = slice(c * rc, (c + 1) * rc)
        g = jnp.concatenate([g_ref[s, rows, :] for s in range(n_s5)], axis=1)
        gate = jax.nn.sigmoid(jnp.dot(g, gw_ref[...], preferred_element_type=F32) + gb_ref[...])
        z = jnp.concatenate([z_ref[s, rows, :] for s in range(n_s5)], axis=1).astype(F32)
        y_s5 = (g.astype(F32) * gate * (z * jax.nn.sigmoid(z))).astype(BF16)
        y_gla = jnp.concatenate([yg_ref[s, rows, :] for s in range(yg_ref.shape[0])], axis=1)
        mixed = (jnp.dot(y_s5, wo_ref[:d_s5, :], preferred_element_type=F32)
                 + jnp.dot(y_gla, wo_ref[d_s5:, :], preferred_element_type=F32))
        ms = jnp.mean(mixed * mixed, axis=-1, keepdims=True)
        o_ref[rows, :] = x_ref[rows, :] + mixed * lax.rsqrt(ms + EPS) * pw_ref[...]


def _out_proj(g_slabs, proj, y_gla, x2d, glu_w, glu_b, w_out, post_w, layer, *, tm, slab_z):
    m, d = x2d.shape
    n_s5 = g_slabs.shape[0]
    n_gla = y_gla.shape[0]
    d_s5 = n_s5 * LANES
    d_mix = w_out.shape[1]
    assert glu_w.dtype == w_out.dtype and glu_w.shape[2] <= d
    return pl.pallas_call(
        functools.partial(_out_proj_kernel, layer=layer),
        grid=(m // tm,),
        in_specs=[
            pl.BlockSpec((n_s5, tm, LANES), lambda i: (0, i, 0)),
            pl.BlockSpec((n_s5, tm, LANES), lambda i: (slab_z // n_s5, i, 0)),
            pl.BlockSpec((n_gla, tm, LANES), lambda i: (0, i, 0)),
            pl.BlockSpec((tm, d), lambda i: (i, 0)),
            pl.BlockSpec(memory_space=pl.ANY),
            pl.BlockSpec((1, d_s5), lambda i: (0, 0)),
            pl.BlockSpec(memory_space=pl.ANY),
            pl.BlockSpec((1, d), lambda i: (0, 0)),
        ],
        out_specs=pl.BlockSpec((tm, d), lambda i: (i, 0)),
        out_shape=jax.ShapeDtypeStruct((m, d), F32),
        scratch_shapes=[
            pltpu.VMEM((d_s5, d_s5), BF16),
            pltpu.VMEM((d_mix, d), BF16),
            pltpu.VMEM((2, OUT_PROJ_W_CHUNK, d), w_out.dtype),
            pltpu.SemaphoreType.DMA((2,)),
        ],
        compiler_params=pltpu.CompilerParams(
            dimension_semantics=("arbitrary",),
            vmem_limit_bytes=VMEM_LIMIT),
        name="out_proj",
    )(g_slabs, proj, y_gla, x2d, glu_w, glu_b, w_out, post_w)


def kernel(x, pre_norm_w, w_in, s5_A_re, s5_A_im, s5_B_re, s5_B_im, s5_C_re, s5_C_im, s5_D,
           s5_log_dt, s5_glu_w, s5_glu_b, gla_gate_up, gla_gate_bias, gla_norm_w, w_out,
           post_norm_w):
    bsz, seq_len, d_model = x.shape
    depth = w_in.shape[0]
    d_in = w_in.shape[2]
    d_s5 = s5_glu_w.shape[1]
    rank, d_gk = gla_gate_up.shape[1:]
    d_gv = GLA_HEADS * gla_norm_w.shape[1]
    m = bsz * seq_len
    d_main = 2 * d_s5 + 2 * d_gk + 2 * d_gv
    tn = d_s5
    assert d_s5 % LANES == 0 and d_gk == GLA_HEADS * LANES and d_gv == GLA_HEADS * 2 * LANES
    assert d_in == d_main + rank and rank <= LANES and d_main % tn == 0 and 2 * d_gk == tn
    assert seq_len % (S5_T * 8) == 0 and w_out.shape[1] == 2 * d_s5

    n_s5 = d_s5 // LANES
    slab_z = 0
    slab_q = n_s5
    slab_k = slab_q + d_gk // LANES
    slab_v = slab_k + d_gk // LANES
    slab_gz = slab_v + d_gv // LANES

    resid = x.astype(F32).reshape(m, d_model)
    for l in range(depth):
        kb, u_op, v_op, abar = _s5_ops(s5_A_re[l], s5_A_im[l], s5_B_re[l], s5_B_im[l],
                                       s5_C_re[l], s5_C_im[l], s5_log_dt[l])
        dtile = jnp.tile(s5_D[l].astype(F32).reshape(n_s5, 2, 1, LANES // 2), (1, 1, 1, S5_T))

        u2, proj, g_low = _in_proj(resid, pre_norm_w[l].astype(F32)[None],
                                   jnp.swapaxes(w_in, 1, 2), l, d_main=d_main, tm=512, tn=tn)

        g_slabs = _s5(u2, kb, u_op, v_op, abar, dtile, n_seq=bsz)

        gate_up_pad = jnp.pad(gla_gate_up[l].astype(BF16), ((0, LANES - rank), (0, 0)))
        y_gla = _gla(proj, g_low, gate_up_pad, gla_gate_bias[l].astype(F32)[None],
                     gla_norm_w[l].astype(F32)[None], n_seq=bsz, seq_len=seq_len, rows=1024,
                     slab_q=slab_q, slab_k=slab_k, slab_v=slab_v, slab_gz=slab_gz)

        resid = _out_proj(g_slabs, proj, y_gla, resid, s5_glu_w, s5_glu_b[l].astype(F32)[None],
                          w_out, post_norm_w[l].astype(F32)[None], l, tm=512, slab_z=slab_z)
    return resid.reshape(bsz, seq_len, d_model).astype(x.dtype)
```

```python
import functools

import jax
import jax.numpy as jnp
from jax import lax
from jax.experimental import pallas as pl
from jax.experimental.pallas import tpu as pltpu

F32 = jnp.float32
BF16 = jnp.bfloat16

S5_GROUP = 16
GLA_HEADS = 4
GLA_TAU = 16.0
GLA_CHUNK = 64
EPS = 1e-6

LANES = 128
S5_T = 16
S5_UNFOLD_PITCH = 24
OUT_PROJ_ROW_CHUNK = 256
IN_PROJ_W_CHUNK = 256
IN_PROJ_W_SLOTS = 4
OUT_PROJ_W_CHUNK = 256
VMEM_LIMIT = 56 * 1024 * 1024


NT_DIMS = (((1,), (1,)), ((), ()))
TN_DIMS = (((0,), (0,)), ((), ()))


def _cmul(ar, ai, br, bi):
    return ar * br - ai * bi, ar * bi + ai * br


def _stream_cast_rows(src, dst_ref, stage_ref, sem):
    n_rows, width = src.shape
    chunk = stage_ref.shape[1]
    sizes = [chunk] * (n_rows // chunk) + ([n_rows % chunk] if n_rows % chunk else [])

    def copy(c):
        return pltpu.make_async_copy(
            src.at[pl.ds(c * chunk, sizes[c]), :],
            stage_ref.at[c % 2, pl.ds(0, sizes[c]), pl.ds(0, width)], sem.at[c % 2])

    for c in range(min(2, len(sizes))):
        copy(c).start()
    for c, rows in enumerate(sizes):
        copy(c).wait()
        dst_ref[c * chunk:c * chunk + rows, :width] = (
            stage_ref[c % 2, :rows, :width].astype(dst_ref.dtype))
        if c + 2 < len(sizes):
            copy(c + 2).start()


def _s5_ops_kernel(*refs, n_state):
    (a_re_ref, a_im_ref, ldt_row_ref, d_ref, b_re_ref, b_im_ref, c_re_ref, c_im_ref,
     kb_ref, u_ref, v_ref, ab_ref) = refs
    nt = S5_T
    ns = nt // 2
    half = LANES // 2
    gps, p = a_re_ref.shape

    def terms(t):
        hi = t.astype(BF16).astype(F32)
        mid = (t - hi).astype(BF16).astype(F32)
        return hi, mid, t - hi - mid

    def select(t, onehot):
        return sum(lax.dot_general(x, onehot, TN_DIMS, preferred_element_type=F32)
                   for x in terms(t))

    def spread_rows(t):
        return sum(lax.dot_general(spread, x, TN_DIMS, preferred_element_type=F32)
                   for x in terms(t))

    group_of_lane = lax.broadcasted_iota(jnp.int32, (gps, LANES), 1) // S5_GROUP
    spread = jnp.where(group_of_lane == lax.broadcasted_iota(jnp.int32, (gps, LANES), 0), 1.0, 0.0)
    eye = jnp.where(lax.broadcasted_iota(jnp.int32, (LANES, LANES), 0)
                    == lax.broadcasted_iota(jnp.int32, (LANES, LANES), 1), 1.0, 0.0)
    dup = lambda t: jnp.concatenate([t, t], axis=1)
    are_c = jnp.minimum(a_re_ref[...], -1e-4)
    aim_c = a_im_ref[...]
    n_groups = ldt_row_ref.shape[1]
    mine = (lax.broadcasted_iota(jnp.int32, (gps, n_groups), 1)
            == pl.program_id(0) * gps + lax.broadcasted_iota(jnp.int32, (gps, n_groups), 0))
    ldt = jnp.sum(jnp.where(mine, ldt_row_ref[...], 0.0), axis=1, keepdims=True)
    dt = jnp.exp(ldt)
    mag = jnp.exp(are_c * dt)
    abr_c = mag * jnp.cos(aim_c * dt)
    abi_c = mag * jnp.sin(aim_c * dt)
    are, aim, abr, abi = [spread_rows(dup(t)) for t in (are_c, aim_c, abr_c, abi_c)]
    b_re, b_im = [dup(r[...].reshape(gps * S5_GROUP, p)) for r in (b_re_ref, b_im_ref)]
    nbr, nbi = [select(t, spread) for t in (abr_c, abi_c)]
    cr, ci = [select(r[...].reshape(gps * S5_GROUP, p), eye) for r in (c_re_ref, c_im_ref)]

    den = are * are + aim * aim
    nr = abr - 1.0
    fr = (nr * are + abi * aim) / den
    fi = (abi * are - nr * aim) / den
    xr, xi = _cmul(fr, fi, b_re, b_im)

    row = lax.broadcasted_iota(jnp.int32, (LANES, LANES), 0)
    lane = lax.broadcasted_iota(jnp.int32, (LANES, LANES), 1)
    own_half = ((row // S5_GROUP) % 2) == (lane // half)
    same_group = ((row % half) // S5_GROUP) == ((lane % half) // S5_GROUP)
    first_half = lane < half

    def split(t):
        hi = t.astype(BF16)
        return hi, (t - hi.astype(F32)).astype(BF16)

    c_stack = jnp.concatenate([cr, ci], axis=0)
    c_swap = pltpu.roll(c_stack, half, axis=1)
    c_hi, c_lo = split(jnp.concatenate([c_stack, c_swap], axis=1))
    c_terms = jnp.concatenate([c_hi, c_lo, c_hi], axis=0)
    kers, kers_swap, xs = [], [], []
    for tau in range(nt):
        l_hi, l_lo = split(jnp.where(first_half, xr, -xi))
        ker = jnp.dot(jnp.concatenate([l_hi, l_hi, l_lo], axis=1), c_terms,
                      preferred_element_type=F32)
        kers.append(ker[:, :LANES])
        kers_swap.append(ker[:, LANES:])
        xs.append((jnp.where(own_half, xr, 0.0).astype(BF16),
                   jnp.where(own_half, xi, 0.0).astype(BF16)))
        if tau + 1 < nt:
            xr, xi = _cmul(xr, xi, abr, abi)

    d_diag = jnp.where(row == lane, d_ref[...], 0.0)
    kers[0] = kers[0] + d_diag
    kers_swap[0] = kers_swap[0] + pltpu.roll(d_diag, half, axis=1)

    zero = jnp.zeros((half, LANES), F32)
    first_half_rows = lax.broadcasted_iota(jnp.int32, (half, LANES), 1) < half
    for hh in range(2):
        rows = slice(hh * half, (hh + 1) * half)
        low, high = (kers, kers_swap) if hh == 0 else (kers_swap, kers)
        pick = lambda src, tau: src[tau][rows, :] if tau >= 0 else zero
        for d in range(ns):
            top = jnp.where(first_half_rows, pick(low, 2 * d), pick(high, 2 * d + 1))
            bot = jnp.where(first_half_rows, pick(low, 2 * d - 1), pick(high, 2 * d))
            blk = jnp.concatenate([top, bot], axis=0)
            kb_ref[hh, d] = jnp.where(same_group, blk, 0.0).astype(BF16)
        for s2 in range(ns):
            for part in range(2):
                u_ref[hh, s2, part] = jnp.concatenate(
                    [xs[nt - 1 - 2 * s2][part][rows, :], xs[nt - 2 - 2 * s2][part][rows, :]], axis=0)

    pr, pi = abr, abi
    for _ in range(4):
        pr, pi = _cmul(pr, pi, pr, pi)
    assert nt == 16
    pairs = LANES // (2 * S5_GROUP)
    for part, val in enumerate((pr, pi)):
        for a in range(pairs):
            r0 = 2 * a * S5_GROUP
            piece = jnp.where(first_half[:1], val[r0:r0 + 1], val[r0 + S5_GROUP:r0 + S5_GROUP + 1])
            c0 = part * (n_state // 2) + a * LANES
            ab_ref[:, c0:c0 + LANES] = piece

    lane_n = lax.broadcasted_iota(jnp.int32, (half, LANES), 1)
    parity = (lane_n // S5_GROUP) % 2
    first_n = lane_n < half
    pr, pi = nbr, nbi
    outs = []
    for t in range(nt):
        wr, wi = _cmul(cr, ci, pr, pi)
        outs.append((wr, -wi))
        if t + 1 < nt:
            pr, pi = _cmul(pr, pi, nbr, nbi)
    for t2 in range(ns):
        for part in range(2):
            even, odd = outs[2 * t2][part], outs[2 * t2 + 1][part]
            packed = (jnp.where(first_n, even, pltpu.roll(odd, half, axis=1)),
                      jnp.where(first_n, pltpu.roll(even, half, axis=1), odd))
            for hh in range(2):
                for q in range(2):
                    v_ref[hh, t2, part, q * half:(q + 1) * half, :] = (
                        jnp.where(parity == q, packed[hh], 0.0).astype(BF16))


def _s5_ops(a_re, a_im, b_re, b_im, c_re, c_im, log_dt, d_skip, layer):
    g, p = a_re.shape
    h = S5_GROUP
    gps = LANES // h
    n_j = g // gps
    assert 2 * p == LANES and g % gps == 0
    n_state = 2 * gps * p
    f = lambda t: t.astype(F32)
    per_group = [f(t).reshape(n_j, gps, p) for t in (a_re, a_im)]
    per_chan = [f(t).reshape(n_j, gps, h, p)
                for t in (jnp.swapaxes(b_re, 1, 2), jnp.swapaxes(b_im, 1, 2), c_re, c_im)]
    ns = S5_T // 2
    return pl.pallas_call(
        functools.partial(_s5_ops_kernel, n_state=n_state),
        grid=(n_j,),
        in_specs=([pl.BlockSpec((None, gps, p), lambda j: (j, 0, 0))] * 2
                  + [pl.BlockSpec((1, g), lambda j: (layer, 0)),
                     pl.BlockSpec((1, LANES), lambda j: (layer, j))]
                  + [pl.BlockSpec((None, gps, h, p), lambda j: (j, 0, 0, 0))] * 4),
        out_specs=[
            pl.BlockSpec((None, 2, ns, LANES, LANES), lambda j: (j, 0, 0, 0, 0)),
            pl.BlockSpec((None, 2, ns, 2, LANES, LANES), lambda j: (j, 0, 0, 0, 0, 0)),
            pl.BlockSpec((None, 2, ns, 2, LANES, LANES), lambda j: (j, 0, 0, 0, 0, 0)),
            pl.BlockSpec((None, 1, n_state), lambda j: (j, 0, 0)),
        ],
        out_shape=[
            jax.ShapeDtypeStruct((n_j, 2, ns, LANES, LANES), BF16),
            jax.ShapeDtypeStruct((n_j, 2, ns, 2, LANES, LANES), BF16),
            jax.ShapeDtypeStruct((n_j, 2, ns, 2, LANES, LANES), BF16),
            jax.ShapeDtypeStruct((n_j, 1, n_state), F32),
        ],
        compiler_params=pltpu.CompilerParams(dimension_semantics=("parallel",)),
        name="s5_ops",
    )(*per_group, f(log_dt), f(d_skip), *per_chan)


def _in_proj_kernel(x_ref, nw_ref, w_hbm, u_ref, p_ref, l_ref, h_ref, s_ref, w_ref, stage_ref, sem,
                    *, layer, d_main, tn):
    tm = x_ref.shape[0]
    n_slabs = tn // LANES
    fold = tm // S5_T

    n_rows = w_hbm.shape[1]
    n_slots, chunk = stage_ref.shape[:2]
    sizes = [chunk] * (n_rows // chunk) + ([n_rows % chunk] if n_rows % chunk else [])
    per_dot = tn // chunk
    assert per_dot <= n_slots and tn % chunk == 0

    def copy(c):
        return pltpu.make_async_copy(
            w_hbm.at[layer, pl.ds(c * chunk, sizes[c]), :],
            stage_ref.at[c % n_slots, pl.ds(0, sizes[c]), :], sem.at[c % n_slots])

    def land(c):
        copy(c).wait()
        w_ref[c * chunk:c * chunk + sizes[c], :] = (
            stage_ref[c % n_slots, :sizes[c], :].astype(BF16))
        if c + n_slots < len(sizes):
            copy(c + n_slots).start()

    def project(after_dot):
        xf = x_ref[...]
        h_ref[...] = (xf * nw_ref[...]).astype(BF16)
        rs = lax.rsqrt(jnp.mean(xf * xf, axis=-1, keepdims=True) + EPS)
        for n in range(d_main // tn):
            res = lax.dot_general(h_ref[...], w_ref[n * tn:(n + 1) * tn, :], NT_DIMS,
                                  preferred_element_type=F32) * rs
            after_dot(n)
            if n == 0:
                for s in range(n_slabs):
                    s_ref[s] = res[:, s * LANES:(s + 1) * LANES]
                low = lax.broadcasted_iota(jnp.int32, (fold, LANES), 1) < LANES // 2
                for s in range(n_slabs):
                    for t2 in range(S5_T // 2):
                        even = s_ref[s, pl.ds(2 * t2, fold, stride=S5_T), :]
                        odd = s_ref[s, pl.ds(2 * t2 + 1, fold, stride=S5_T), :]
                        cols = slice(t2 * LANES, (t2 + 1) * LANES)
                        u_ref[2 * s, :, cols] = (
                            jnp.where(low, even, pltpu.roll(odd, LANES // 2, axis=1)).astype(BF16))
                        u_ref[2 * s + 1, :, cols] = (
                            jnp.where(low, pltpu.roll(even, LANES // 2, axis=1), odd).astype(BF16))
            else:
                for s in range(n_slabs):
                    p_ref[(n - 1) * n_slabs + s] = res[:, s * LANES:(s + 1) * LANES].astype(BF16)
        low_rank = lax.dot_general(h_ref[...], w_ref[d_main:d_main + LANES, :], NT_DIMS,
                                   preferred_element_type=F32) * rs
        l_ref[...] = low_rank.astype(BF16)

    @pl.when(pl.program_id(0) == 0)
    def _():
        w_ref[n_rows:, :] = jnp.zeros((w_ref.shape[0] - n_rows, w_ref.shape[1]), BF16)
        for c in range(min(n_slots, len(sizes))):
            copy(c).start()
        for c in range(per_dot):
            land(c)

        def stage_next(n):
            for c in range((n + 1) * per_dot, min((n + 2) * per_dot, len(sizes))):
                land(c)

        project(stage_next)

    @pl.when(pl.program_id(0) > 0)
    def _():
        project(lambda n: None)


def _in_proj(x2d, norm_w, w_t, layer, *, d_main, tm, tn):
    m, d = x2d.shape
    n_slabs = tn // LANES
    n_proj = (d_main // tn - 1) * n_slabs
    assert d_main < w_t.shape[1] <= d_main + LANES
    return pl.pallas_call(
        functools.partial(_in_proj_kernel, layer=layer, d_main=d_main, tn=tn),
        grid=(m // tm,),
        in_specs=[
            pl.BlockSpec((tm, d), lambda i: (i, 0)),
            pl.BlockSpec((1, d), lambda i: (0, 0)),
            pl.BlockSpec(memory_space=pl.ANY),
        ],
        out_specs=[
            pl.BlockSpec((2 * n_slabs, tm // S5_T, S5_T * LANES // 2), lambda i: (0, i, 0)),
            pl.BlockSpec((n_proj, tm, LANES), lambda i: (0, i, 0)),
            pl.BlockSpec((tm, LANES), lambda i: (i, 0)),
        ],
        out_shape=[
            jax.ShapeDtypeStruct((2 * n_slabs, m // S5_T, S5_T * LANES // 2), BF16),
            jax.ShapeDtypeStruct((n_proj, m, LANES), BF16),
            jax.ShapeDtypeStruct((m, LANES), BF16),
        ],
        scratch_shapes=[
            pltpu.VMEM((tm, d), BF16),
            pltpu.VMEM((n_slabs, tm, LANES), F32),
            pltpu.VMEM((d_main + LANES, d), BF16),
            pltpu.VMEM((IN_PROJ_W_SLOTS, IN_PROJ_W_CHUNK, d), w_t.dtype),
            pltpu.SemaphoreType.DMA((IN_PROJ_W_SLOTS,)),
        ],
        compiler_params=pltpu.CompilerParams(
            dimension_semantics=("arbitrary",),
            vmem_limit_bytes=VMEM_LIMIT),
        name="in_proj",
    )(x2d, norm_w, w_t)


def _s5_kernel(x_ref, kb_ref, u_ref, v_ref, a_ref, o_ref,
               m_ref, wz_ref, wy_ref, z_ref, sp_ref, y_ref, *, rows_per_seq, n_seq):
    ns = S5_T // 2
    rows = x_ref.shape[1]
    half = a_ref.shape[-1] // 2
    hstate = half // 2
    pairs = hstate // LANES
    sub = LANES // (2 * pairs)
    wide = 2 * LANES

    @pl.when(pl.program_id(0) == 0)
    def _():
        m_ref[:, (ns - 1) * LANES:, :LANES] = jnp.zeros((2, LANES, LANES), BF16)
        wz_ref[...] = jnp.zeros_like(wz_ref)
        y_ref[...] = jnp.zeros_like(y_ref)

    lane = lax.broadcasted_iota(jnp.int32, (LANES, LANES), 1)
    for hh in range(2):
        for s in range(ns):
            for k in range(2):
                lag = ns - 2 + k - s
                if lag >= 0:
                    m_ref[hh, s * LANES:(s + 1) * LANES, k * LANES:(k + 1) * LANES] = kb_ref[hh, lag]
        for s in range(ns):
            for part in range(2):
                for rho in range(2):
                    for a in range(pairs):
                        r0 = rho * (LANES // 2) + a * sub
                        c0 = part * hstate + a * LANES
                        wz_ref[hh, s * LANES + r0:s * LANES + r0 + sub, c0:c0 + LANES] = (
                            u_ref[hh, s, part, r0:r0 + sub, :])
        for t in range(ns):
            for part in range(2):
                blk = v_ref[hh, t, part]
                for a in range(pairs):
                    r0 = part * hstate + a * LANES
                    wy_ref[hh, r0:r0 + LANES, t * LANES:(t + 1) * LANES] = (
                        jnp.where(((lane % (LANES // 2)) // sub) == a, blk, jnp.zeros_like(blk)))
        zh = jnp.dot(x_ref[hh], wz_ref[hh], preferred_element_type=F32)
        z_ref[:, hh * hstate:(hh + 1) * hstate] = zh[:, :hstate]
        z_ref[:, half + hh * hstate:half + (hh + 1) * hstate] = zh[:, hstate:]

    a_re = a_ref[:, :half]
    a_im = a_ref[:, half:]

    state = [(jnp.zeros((1, half), F32), jnp.zeros((1, half), F32)) for _ in range(n_seq)]
    for c in range(rows_per_seq):
        for b in range(n_seq):
            s_re, s_im = state[b]
            r = b * rows_per_seq + c
            sp_ref[r:r + 1, :half] = s_re
            sp_ref[r:r + 1, half:] = s_im
            z_re = z_ref[r:r + 1, :half]
            z_im = z_ref[r:r + 1, half:]
            state[b] = (a_re * s_re - a_im * s_im + z_re, a_re * s_im + a_im * s_re + z_im)

    spb = [jnp.concatenate([sp_ref[:, hh * hstate:(hh + 1) * hstate],
                            sp_ref[:, half + hh * hstate:half + (hh + 1) * hstate]],
                           axis=1).astype(BF16) for hh in range(2)]
    pitch = y_ref.shape[0] // rows
    low = lax.broadcasted_iota(jnp.int32, (rows, LANES), 1) < LANES // 2
    for n in range(ns // 2):
        cols = slice(n * wide, (n + 1) * wide)
        kk = (n + 1) * wide
        ys = []
        for hh in range(2):
            acc = jnp.dot(x_ref[hh, :, :kk], m_ref[hh, ns * LANES - kk:, :],
                          preferred_element_type=F32)
            acc = acc + jnp.dot(spb[hh], wy_ref[hh, :, cols], preferred_element_type=F32)
            ys.append(jax.nn.gelu(acc))
        for k in range(2):
            y0 = ys[0][:, k * LANES:(k + 1) * LANES]
            y1 = ys[1][:, k * LANES:(k + 1) * LANES]
            t = 2 * (2 * n + k)
            y_ref[pl.ds(t, rows, stride=pitch), :] = (
                jnp.where(low, y0, pltpu.roll(y1, LANES // 2, axis=1)))
            y_ref[pl.ds(t + 1, rows, stride=pitch), :] = (
                jnp.where(low, pltpu.roll(y0, LANES // 2, axis=1), y1))
    tokens = y_ref[...].reshape(rows, pitch, LANES)[:, :S5_T, :].reshape(rows * S5_T, LANES)
    o_ref[...] = tokens.astype(BF16)


def _s5(u2, kb, u_op, v_op, abar, *, n_seq):
    n_j, _, ns, _, _ = kb.shape
    rows = u2.shape[1]
    width = ns * LANES
    n_state = abar.shape[-1]
    return pl.pallas_call(
        functools.partial(_s5_kernel, rows_per_seq=rows // n_seq, n_seq=n_seq),
        grid=(n_j,),
        in_specs=[
            pl.BlockSpec((2, rows, width), lambda j: (j, 0, 0)),
            pl.BlockSpec((None, 2, ns, LANES, LANES), lambda j: (j, 0, 0, 0, 0)),
            pl.BlockSpec((None, 2, ns, 2, LANES, LANES), lambda j: (j, 0, 0, 0, 0, 0)),
            pl.BlockSpec((None, 2, ns, 2, LANES, LANES), lambda j: (j, 0, 0, 0, 0, 0)),
            pl.BlockSpec((None, 1, n_state), lambda j: (j, 0, 0)),
        ],
        out_specs=pl.BlockSpec((None, rows * S5_T, LANES), lambda j: (j, 0, 0)),
        out_shape=jax.ShapeDtypeStruct((n_j, rows * S5_T, LANES), BF16),
        scratch_shapes=[
            pltpu.VMEM((2, width, 2 * LANES), BF16),
            pltpu.VMEM((2, width, n_state // 2), BF16),
            pltpu.VMEM((2, n_state // 2, width), BF16),
            pltpu.VMEM((rows, n_state), F32),
            pltpu.VMEM((rows, n_state), F32),
            pltpu.VMEM((rows * S5_UNFOLD_PITCH, LANES), F32),
        ],
        compiler_params=pltpu.CompilerParams(
            dimension_semantics=("arbitrary",),
            vmem_limit_bytes=VMEM_LIMIT),
        name="s5",
    )(u2, kb, u_op, v_op, abar)


def _log_sigmoid(x):
    return jnp.minimum(x, 0.0) - jnp.log(1.0 + jnp.exp(-jnp.abs(x)))


def _gla_kernel(q_ref, k_ref, v_ref, gz_ref, gl_ref, gu_ref, gb_ref, nw_ref, o_ref, st_ref,
                *, chunk):
    n_heads, rows, dk = q_ref.shape
    n_chunks = rows // chunk
    d_gk = n_heads * dk

    @pl.when(pl.program_id(1) == 0)
    def _():
        st_ref[...] = jnp.zeros_like(st_ref)

    rank = gu_ref.shape[0]
    zg = jnp.dot(gl_ref[:, :rank], gu_ref[...].astype(BF16), preferred_element_type=F32) + gb_ref[...]
    lg = _log_sigmoid(zg) * (1.0 / GLA_TAU)

    hi = lg.astype(BF16)
    lo = (lg - hi.astype(F32)).astype(BF16)
    r_id = lax.broadcasted_iota(jnp.int32, (chunk, chunk), 0)
    c_id = lax.broadcasted_iota(jnp.int32, (chunk, chunk), 1)
    causal = c_id <= r_id
    tri = jnp.where(causal, 1.0, 0.0).astype(BF16)
    tri2 = jnp.concatenate([tri, tri], axis=1)
    b_parts, last_parts = [], []
    for n in range(n_chunks):
        rs = slice(n * chunk, (n + 1) * chunk)
        b_n = jnp.dot(tri2, jnp.concatenate([hi[rs], lo[rs]], axis=0), preferred_element_type=F32)
        b_parts.append(b_n)
        last_parts.append(jnp.broadcast_to(b_n[chunk - 1:chunk], (chunk, d_gk)))
    b = jnp.concatenate(b_parts, axis=0)
    e_pos = jnp.exp(b)
    e_neg = jnp.exp(-b)
    decay = jnp.exp(jnp.concatenate(last_parts, axis=0))

    for h in range(n_heads):
        hs = slice(h * dk, (h + 1) * dk)
        q_e = (q_ref[h].astype(F32) * ((dk ** -0.5) * e_pos[:, hs])).astype(BF16)
        k_f = k_ref[h].astype(F32) * e_neg[:, hs]
        k_e = k_f.astype(BF16)
        k_t = (k_f * decay[:, hs]).astype(BF16)
        v = jnp.concatenate([v_ref[2 * h], v_ref[2 * h + 1]], axis=1)

        chunks = [slice(n * chunk, (n + 1) * chunk) for n in range(n_chunks)]
        attn = [lax.dot_general(q_e[rs], k_e[rs], NT_DIMS, preferred_element_type=F32)
                for rs in chunks]
        kv_t = [lax.dot_general(v[rs], k_t[rs], TN_DIMS, preferred_element_type=F32)
                for rs in chunks]
        attn = [jnp.where(causal, a, 0.0).astype(BF16) for a in attn]
        st = st_ref[h]
        st_in = []
        for n in range(n_chunks):
            st_in.append(st.astype(BF16))
            st = decay[n * chunk:n * chunk + 1, hs] * st + kv_t[n]
        st_ref[h] = st
        outs = [jnp.dot(attn[n], v[rs], preferred_element_type=F32)
                + lax.dot_general(q_e[rs], st_in[n], NT_DIMS, preferred_element_type=F32)
                for n, rs in enumerate(chunks)]

        o = jnp.concatenate(outs, axis=0)
        ms = jnp.mean(o * o, axis=-1, keepdims=True)
        o = o * lax.rsqrt(ms + EPS) * nw_ref[...]
        gz = jnp.concatenate([gz_ref[2 * h], gz_ref[2 * h + 1]], axis=1).astype(F32)
        y = (o * (gz * jax.nn.sigmoid(gz))).astype(BF16)
        o_ref[2 * h] = y[:, :LANES]
        o_ref[2 * h + 1] = y[:, LANES:]


def _gla(proj, g_low, gate_up, gate_bias, norm_w, layer, *, n_seq, seq_len, rows, slab_q, slab_k,
         slab_v, slab_gz):
    m = proj.shape[1]
    rank = gate_up.shape[1]
    nb = seq_len // rows
    nh = GLA_HEADS
    dk = LANES
    dv = 2 * LANES
    tok = lambda b, n: b * nb + n
    return pl.pallas_call(
        functools.partial(_gla_kernel, chunk=GLA_CHUNK),
        grid=(n_seq, nb),
        in_specs=[
            pl.BlockSpec((nh, rows, LANES), lambda b, n: (slab_q // nh, tok(b, n), 0)),
            pl.BlockSpec((nh, rows, LANES), lambda b, n: (slab_k // nh, tok(b, n), 0)),
            pl.BlockSpec((2 * nh, rows, LANES), lambda b, n: (slab_v // (2 * nh), tok(b, n), 0)),
            pl.BlockSpec((2 * nh, rows, LANES), lambda b, n: (slab_gz // (2 * nh), tok(b, n), 0)),
            pl.BlockSpec((rows, LANES), lambda b, n: (tok(b, n), 0)),
            pl.BlockSpec((None, rank, nh * dk), lambda b, n: (layer, 0, 0)),
            pl.BlockSpec((1, nh * dk), lambda b, n: (layer, 0)),
            pl.BlockSpec((1, dv), lambda b, n: (layer, 0)),
        ],
        out_specs=pl.BlockSpec((2 * nh, rows, LANES), lambda b, n: (0, tok(b, n), 0)),
        out_shape=jax.ShapeDtypeStruct((2 * nh, m, LANES), BF16),
        scratch_shapes=[pltpu.VMEM((nh, dv, dk), F32)],
        compiler_params=pltpu.CompilerParams(
            dimension_semantics=("parallel", "arbitrary"),
            vmem_limit_bytes=VMEM_LIMIT),
        name="gla",
    )(proj, proj, proj, proj, g_low, gate_up.astype(F32), gate_bias.astype(F32), norm_w.astype(F32))


def _out_proj_kernel(g_ref, z_ref, yg_ref, x_ref, gw_hbm, gb_ref, wo_hbm, pw_ref,
                     o_ref, gw_ref, wo_ref, stage_ref, sem, *, layer):
    n_s5 = g_ref.shape[0]
    d_s5 = n_s5 * LANES
    tm = x_ref.shape[0]
    rc = OUT_PROJ_ROW_CHUNK

    @pl.when(pl.program_id(0) == 0)
    def _():
        _stream_cast_rows(wo_hbm.at[layer], wo_ref, stage_ref, sem)
        _stream_cast_rows(gw_hbm.at[layer], gw_ref, stage_ref, sem)

    for c in range(tm // rc):
        rows = slice(c * rc, (c + 1) * rc)
        g = jnp.concatenate([g_ref[s, rows, :] for s in range(n_s5)], axis=1)
        gate = jax.nn.sigmoid(jnp.dot(g, gw_ref[...], preferred_element_type=F32) + gb_ref[...])
        z = jnp.concatenate([z_ref[s, rows, :] for s in range(n_s5)], axis=1).astype(F32)
        y_s5 = (g.astype(F32) * gate * (z * jax.nn.sigmoid(z))).astype(BF16)
        y_gla = jnp.concatenate([yg_ref[s, rows, :] for s in range(yg_ref.shape[0])], axis=1)
        mixed = (jnp.dot(y_s5, wo_ref[:d_s5, :], preferred_element_type=F32)
                 + jnp.dot(y_gla, wo_ref[d_s5:, :], preferred_element_type=F32))
        ms = jnp.mean(mixed * mixed, axis=-1, keepdims=True)
        o_ref[rows, :] = x_ref[rows, :] + mixed * lax.rsqrt(ms + EPS) * pw_ref[...]


def _out_proj(g_slabs, proj, y_gla, x2d, glu_w, glu_b, w_out, post_w, layer, *, tm, slab_z):
    m, d = x2d.shape
    n_s5 = g_slabs.shape[0]
    n_gla = y_gla.shape[0]
    d_s5 = n_s5 * LANES
    d_mix = w_out.shape[1]
    assert glu_w.dtype == w_out.dtype and glu_w.shape[2] <= d
    return pl.pallas_call(
        functools.partial(_out_proj_kernel, layer=layer),
        grid=(m // tm,),
        in_specs=[
            pl.BlockSpec((n_s5, tm, LANES), lambda i: (0, i, 0)),
            pl.BlockSpec((n_s5, tm, LANES), lambda i: (slab_z // n_s5, i, 0)),
            pl.BlockSpec((n_gla, tm, LANES), lambda i: (0, i, 0)),
            pl.BlockSpec((tm, d), lambda i: (i, 0)),
            pl.BlockSpec(memory_space=pl.ANY),
            pl.BlockSpec((1, d_s5), lambda i: (0, 0)),
            pl.BlockSpec(memory_space=pl.ANY),
            pl.BlockSpec((1, d), lambda i: (0, 0)),
        ],
        out_specs=pl.BlockSpec((tm, d), lambda i: (i, 0)),
        out_shape=jax.ShapeDtypeStruct((m, d), F32),
        scratch_shapes=[
            pltpu.VMEM((d_s5, d_s5), BF16),
            pltpu.VMEM((d_mix, d), BF16),
            pltpu.VMEM((2, OUT_PROJ_W_CHUNK, d), w_out.dtype),
            pltpu.SemaphoreType.DMA((2,)),
        ],
        compiler_params=pltpu.CompilerParams(
            dimension_semantics=("arbitrary",),
            vmem_limit_bytes=VMEM_LIMIT),
        name="out_proj",
    )(g_slabs, proj, y_gla, x2d, glu_w, glu_b, w_out, post_w)


def kernel(x, pre_norm_w, w_in, s5_A_re, s5_A_im, s5_B_re, s5_B_im, s5_C_re, s5_C_im, s5_D,
           s5_log_dt, s5_glu_w, s5_glu_b, gla_gate_up, gla_gate_bias, gla_norm_w, w_out,
           post_norm_w):
    bsz, seq_len, d_model = x.shape
    depth = w_in.shape[0]
    d_in = w_in.shape[2]
    d_s5 = s5_glu_w.shape[1]
    rank, d_gk = gla_gate_up.shape[1:]
    d_gv = GLA_HEADS * gla_norm_w.shape[1]
    m = bsz * seq_len
    d_main = 2 * d_s5 + 2 * d_gk + 2 * d_gv
    tn = d_s5
    assert d_s5 % LANES == 0 and d_gk == GLA_HEADS * LANES and d_gv == GLA_HEADS * 2 * LANES
    assert d_in == d_main + rank and rank <= LANES and d_main % tn == 0 and 2 * d_gk == tn
    assert seq_len % (S5_T * 8) == 0 and w_out.shape[1] == 2 * d_s5

    n_s5 = d_s5 // LANES
    slab_z = 0
    slab_q = n_s5
    slab_k = slab_q + d_gk // LANES
    slab_v = slab_k + d_gk // LANES
    slab_gz = slab_v + d_gv // LANES

    resid = x.astype(F32).reshape(m, d_model)
    for l in range(depth):
        kb, u_op, v_op, abar = _s5_ops(s5_A_re[l], s5_A_im[l], s5_B_re[l], s5_B_im[l],
                                       s5_C_re[l], s5_C_im[l], s5_log_dt, s5_D, l)

        u2, proj, g_low = _in_proj(resid, pre_norm_w[l].astype(F32)[None],
                                   jnp.swapaxes(w_in, 1, 2), l, d_main=d_main, tm=512, tn=tn)

        g_slabs = _s5(u2, kb, u_op, v_op, abar, n_seq=bsz)

        y_gla = _gla(proj, g_low, gla_gate_up, gla_gate_bias, gla_norm_w, l,
                     n_seq=bsz, seq_len=seq_len, rows=512,
                     slab_q=slab_q, slab_k=slab_k, slab_v=slab_v, slab_gz=slab_gz)

        resid = _out_proj(g_slabs, proj, y_gla, resid, s5_glu_w, s5_glu_b[l].astype(F32)[None],
                          w_out, post_norm_w[l].astype(F32)[None], l, tm=512, slab_z=slab_z)
    return resid.reshape(bsz, seq_len, d_model).astype(x.dtype)
```

```python
import functools

import jax
import jax.numpy as jnp
from jax import lax
from jax.experimental import pallas as pl
from jax.experimental.pallas import tpu as pltpu

F32 = jnp.float32
BF16 = jnp.bfloat16

S5_GROUP = 16
GLA_HEADS = 4
GLA_TAU = 16.0
GLA_CHUNK = 64
EPS = 1e-6

LANES = 128
S5_T = 16
S5_UNFOLD_PITCH = 24
OUT_PROJ_ROW_CHUNK = 256
IN_PROJ_W_CHUNK = 256
IN_PROJ_W_SLOTS = 4
OUT_PROJ_W_CHUNK = 256
OUT_PROJ_W_SLOTS = 4
VMEM_LIMIT = 56 * 1024 * 1024


NT_DIMS = (((1,), (1,)), ((), ()))
TN_DIMS = (((0,), (0,)), ((), ()))


def _cmul(ar, ai, br, bi):
    return ar * br - ai * bi, ar * bi + ai * br


class _WeightRing:
    def __init__(self, stage_ref, sem):
        self.stage, self.sem, self.jobs = stage_ref, sem, []
        self.n_slots, self.chunk = stage_ref.shape[:2]

    def add(self, src, dst_ref):
        n, width = src.shape
        first = len(self.jobs)
        for r0 in range(0, n, self.chunk):
            self.jobs.append((src, dst_ref, r0, min(self.chunk, n - r0), width))
        return list(range(first, len(self.jobs)))

    def _copy(self, k):
        src, _, r0, rows, width = self.jobs[k]
        slot = k % self.n_slots
        return pltpu.make_async_copy(
            src.at[pl.ds(r0, rows), :],
            self.stage.at[slot, pl.ds(0, rows), pl.ds(0, width)], self.sem.at[slot])

    def start(self):
        for k in range(min(self.n_slots, len(self.jobs))):
            self._copy(k).start()

    def land(self, ks):
        for k in ks:
            _, dst_ref, r0, rows, width = self.jobs[k]
            self._copy(k).wait()
            dst_ref[r0:r0 + rows, :width] = (
                self.stage[k % self.n_slots, :rows, :width].astype(dst_ref.dtype))
            if k + self.n_slots < len(self.jobs):
                self._copy(k + self.n_slots).start()


def _s5_ops_kernel(*refs, n_state):
    (a_re_ref, a_im_ref, ldt_row_ref, d_ref, b_re_ref, b_im_ref, c_re_ref, c_im_ref,
     kb_ref, u_ref, v_ref, ab_ref) = refs
    nt = S5_T
    ns = nt // 2
    half = LANES // 2
    gps, p = a_re_ref.shape

    def terms(t):
        hi = t.astype(BF16).astype(F32)
        mid = (t - hi).astype(BF16).astype(F32)
        return hi, mid, t - hi - mid

    def select(t, onehot):
        return sum(lax.dot_general(x, onehot, TN_DIMS, preferred_element_type=F32)
                   for x in terms(t))

    def spread_rows(t):
        return sum(lax.dot_general(spread, x, TN_DIMS, preferred_element_type=F32)
                   for x in terms(t))

    group_of_lane = lax.broadcasted_iota(jnp.int32, (gps, LANES), 1) // S5_GROUP
    spread = jnp.where(group_of_lane == lax.broadcasted_iota(jnp.int32, (gps, LANES), 0), 1.0, 0.0)
    eye = jnp.where(lax.broadcasted_iota(jnp.int32, (LANES, LANES), 0)
                    == lax.broadcasted_iota(jnp.int32, (LANES, LANES), 1), 1.0, 0.0)
    dup = lambda t: jnp.concatenate([t, t], axis=1)
    are_c = jnp.minimum(a_re_ref[...], -1e-4)
    aim_c = a_im_ref[...]
    n_groups = ldt_row_ref.shape[1]
    mine = (lax.broadcasted_iota(jnp.int32, (gps, n_groups), 1)
            == pl.program_id(0) * gps + lax.broadcasted_iota(jnp.int32, (gps, n_groups), 0))
    ldt = jnp.sum(jnp.where(mine, ldt_row_ref[...], 0.0), axis=1, keepdims=True)
    dt = jnp.exp(ldt)
    mag = jnp.exp(are_c * dt)
    abr_c = mag * jnp.cos(aim_c * dt)
    abi_c = mag * jnp.sin(aim_c * dt)
    are, aim, abr, abi = [spread_rows(dup(t)) for t in (are_c, aim_c, abr_c, abi_c)]
    b_re, b_im = [dup(r[...].reshape(gps * S5_GROUP, p)) for r in (b_re_ref, b_im_ref)]
    nbr, nbi = [select(t, spread) for t in (abr_c, abi_c)]
    cr, ci = [select(r[...].reshape(gps * S5_GROUP, p), eye) for r in (c_re_ref, c_im_ref)]

    den = are * are + aim * aim
    nr = abr - 1.0
    fr = (nr * are + abi * aim) / den
    fi = (abi * are - nr * aim) / den
    xr, xi = _cmul(fr, fi, b_re, b_im)

    row = lax.broadcasted_iota(jnp.int32, (LANES, LANES), 0)
    lane = lax.broadcasted_iota(jnp.int32, (LANES, LANES), 1)
    own_half = ((row // S5_GROUP) % 2) == (lane // half)
    same_group = ((row % half) // S5_GROUP) == ((lane % half) // S5_GROUP)
    first_half = lane < half

    def split(t):
        hi = t.astype(BF16)
        return hi, (t - hi.astype(F32)).astype(BF16)

    c_stack = jnp.concatenate([cr, ci], axis=0)
    c_swap = pltpu.roll(c_stack, half, axis=1)
    c_hi, c_lo = split(jnp.concatenate([c_stack, c_swap], axis=1))
    c_terms = jnp.concatenate([c_hi, c_lo, c_hi], axis=0)
    kers, kers_swap, xs = [], [], []
    for tau in range(nt):
        l_hi, l_lo = split(jnp.where(first_half, xr, -xi))
        ker = jnp.dot(jnp.concatenate([l_hi, l_hi, l_lo], axis=1), c_terms,
                      preferred_element_type=F32)
        kers.append(ker[:, :LANES])
        kers_swap.append(ker[:, LANES:])
        xs.append((jnp.where(own_half, xr, 0.0).astype(BF16),
                   jnp.where(own_half, xi, 0.0).astype(BF16)))
        if tau + 1 < nt:
            xr, xi = _cmul(xr, xi, abr, abi)

    d_diag = jnp.where(row == lane, d_ref[...], 0.0)
    kers[0] = kers[0] + d_diag
    kers_swap[0] = kers_swap[0] + pltpu.roll(d_diag, half, axis=1)

    zero = jnp.zeros((half, LANES), F32)
    first_half_rows = lax.broadcasted_iota(jnp.int32, (half, LANES), 1) < half
    for hh in range(2):
        rows = slice(hh * half, (hh + 1) * half)
        low, high = (kers, kers_swap) if hh == 0 else (kers_swap, kers)
        pick = lambda src, tau: src[tau][rows, :] if tau >= 0 else zero
        for d in range(ns):
            top = jnp.where(first_half_rows, pick(low, 2 * d), pick(high, 2 * d + 1))
            bot = jnp.where(first_half_rows, pick(low, 2 * d - 1), pick(high, 2 * d))
            blk = jnp.concatenate([top, bot], axis=0)
            kb_ref[hh, d] = jnp.where(same_group, blk, 0.0).astype(BF16)
        for s2 in range(ns):
            for part in range(2):
                u_ref[hh, s2, part] = jnp.concatenate(
                    [xs[nt - 1 - 2 * s2][part][rows, :], xs[nt - 2 - 2 * s2][part][rows, :]], axis=0)

    pr, pi = abr, abi
    for _ in range(4):
        pr, pi = _cmul(pr, pi, pr, pi)
    assert nt == 16
    pairs = LANES // (2 * S5_GROUP)
    for part, val in enumerate((pr, pi)):
        for a in range(pairs):
            r0 = 2 * a * S5_GROUP
            piece = jnp.where(first_half[:1], val[r0:r0 + 1], val[r0 + S5_GROUP:r0 + S5_GROUP + 1])
            c0 = part * (n_state // 2) + a * LANES
            ab_ref[:, c0:c0 + LANES] = piece

    lane_n = lax.broadcasted_iota(jnp.int32, (half, LANES), 1)
    parity = (lane_n // S5_GROUP) % 2
    first_n = lane_n < half
    pr, pi = nbr, nbi
    outs = []
    for t in range(nt):
        wr, wi = _cmul(cr, ci, pr, pi)
        outs.append((wr, -wi))
        if t + 1 < nt:
            pr, pi = _cmul(pr, pi, nbr, nbi)
    for t2 in range(ns):
        for part in range(2):
            even, odd = outs[2 * t2][part], outs[2 * t2 + 1][part]
            packed = (jnp.where(first_n, even, pltpu.roll(odd, half, axis=1)),
                      jnp.where(first_n, pltpu.roll(even, half, axis=1), odd))
            for hh in range(2):
                for q in range(2):
                    v_ref[hh, t2, part, q * half:(q + 1) * half, :] = (
                        jnp.where(parity == q, packed[hh], 0.0).astype(BF16))


def _s5_ops(a_re, a_im, b_re, b_im, c_re, c_im, log_dt, d_skip, layer):
    g, p = a_re.shape
    h = S5_GROUP
    gps = LANES // h
    n_j = g // gps
    assert 2 * p == LANES and g % gps == 0
    n_state = 2 * gps * p
    f = lambda t: t.astype(F32)
    per_group = [f(t).reshape(n_j, gps, p) for t in (a_re, a_im)]
    per_chan = [f(t).reshape(n_j, gps, h, p)
                for t in (jnp.swapaxes(b_re, 1, 2), jnp.swapaxes(b_im, 1, 2), c_re, c_im)]
    ns = S5_T // 2
    return pl.pallas_call(
        functools.partial(_s5_ops_kernel, n_state=n_state),
        grid=(n_j,),
        in_specs=([pl.BlockSpec((None, gps, p), lambda j: (j, 0, 0))] * 2
                  + [pl.BlockSpec((1, g), lambda j: (layer, 0)),
                     pl.BlockSpec((1, LANES), lambda j: (layer, j))]
                  + [pl.BlockSpec((None, gps, h, p), lambda j: (j, 0, 0, 0))] * 4),
        out_specs=[
            pl.BlockSpec((None, 2, ns, LANES, LANES), lambda j: (j, 0, 0, 0, 0)),
            pl.BlockSpec((None, 2, ns, 2, LANES, LANES), lambda j: (j, 0, 0, 0, 0, 0)),
            pl.BlockSpec((None, 2, ns, 2, LANES, LANES), lambda j: (j, 0, 0, 0, 0, 0)),
            pl.BlockSpec((None, 1, n_state), lambda j: (j, 0, 0)),
        ],
        out_shape=[
            jax.ShapeDtypeStruct((n_j, 2, ns, LANES, LANES), BF16),
            jax.ShapeDtypeStruct((n_j, 2, ns, 2, LANES, LANES), BF16),
            jax.ShapeDtypeStruct((n_j, 2, ns, 2, LANES, LANES), BF16),
            jax.ShapeDtypeStruct((n_j, 1, n_state), F32),
        ],
        compiler_params=pltpu.CompilerParams(dimension_semantics=("parallel",)),
        name="s5_ops",
    )(*per_group, f(log_dt), f(d_skip), *per_chan)


def _in_proj_kernel(x_ref, nw_ref, w_hbm, u_ref, p_ref, l_ref, h_ref, s_ref, w_ref, stage_ref, sem,
                    *, layer, d_main, tn):
    tm = x_ref.shape[0]
    n_slabs = tn // LANES
    fold = tm // S5_T

    n_rows = w_hbm.shape[1]
    ring = _WeightRing(stage_ref, sem)
    jobs = ring.add(w_hbm.at[layer], w_ref)
    per_dot = tn // ring.chunk
    assert per_dot <= ring.n_slots and tn % ring.chunk == 0

    def project(after_dot):
        xf = x_ref[...]
        h_ref[...] = (xf * nw_ref[...]).astype(BF16)
        rs = lax.rsqrt(jnp.mean(xf * xf, axis=-1, keepdims=True) + EPS)
        for n in range(d_main // tn):
            res = lax.dot_general(h_ref[...], w_ref[n * tn:(n + 1) * tn, :], NT_DIMS,
                                  preferred_element_type=F32) * rs
            after_dot(n)
            if n == 0:
                for s in range(n_slabs):
                    s_ref[s] = res[:, s * LANES:(s + 1) * LANES]
                low = lax.broadcasted_iota(jnp.int32, (fold, LANES), 1) < LANES // 2
                for s in range(n_slabs):
                    for t2 in range(S5_T // 2):
                        even = s_ref[s, pl.ds(2 * t2, fold, stride=S5_T), :]
                        odd = s_ref[s, pl.ds(2 * t2 + 1, fold, stride=S5_T), :]
                        cols = slice(t2 * LANES, (t2 + 1) * LANES)
                        u_ref[2 * s, :, cols] = (
                            jnp.where(low, even, pltpu.roll(odd, LANES // 2, axis=1)).astype(BF16))
                        u_ref[2 * s + 1, :, cols] = (
                            jnp.where(low, pltpu.roll(even, LANES // 2, axis=1), odd).astype(BF16))
            else:
                for s in range(n_slabs):
                    p_ref[(n - 1) * n_slabs + s] = res[:, s * LANES:(s + 1) * LANES].astype(BF16)
        low_rank = lax.dot_general(h_ref[...], w_ref[d_main:d_main + LANES, :], NT_DIMS,
                                   preferred_element_type=F32) * rs
        l_ref[...] = low_rank.astype(BF16)

    @pl.when(pl.program_id(0) == 0)
    def _():
        w_ref[n_rows:, :] = jnp.zeros((w_ref.shape[0] - n_rows, w_ref.shape[1]), BF16)
        ring.start()
        ring.land(jobs[:per_dot])

        def stage_next(n):
            ring.land(jobs[(n + 1) * per_dot:(n + 2) * per_dot])

        project(stage_next)

    @pl.when(pl.program_id(0) > 0)
    def _():
        project(lambda n: None)


def _in_proj(x2d, norm_w, w_t, layer, *, d_main, tm, tn):
    m, d = x2d.shape
    n_slabs = tn // LANES
    n_proj = (d_main // tn - 1) * n_slabs
    assert d_main < w_t.shape[1] <= d_main + LANES
    return pl.pallas_call(
        functools.partial(_in_proj_kernel, layer=layer, d_main=d_main, tn=tn),
        grid=(m // tm,),
        in_specs=[
            pl.BlockSpec((tm, d), lambda i: (i, 0)),
            pl.BlockSpec((1, d), lambda i: (0, 0)),
            pl.BlockSpec(memory_space=pl.ANY),
        ],
        out_specs=[
            pl.BlockSpec((2 * n_slabs, tm // S5_T, S5_T * LANES // 2), lambda i: (0, i, 0)),
            pl.BlockSpec((n_proj, tm, LANES), lambda i: (0, i, 0)),
            pl.BlockSpec((tm, LANES), lambda i: (i, 0)),
        ],
        out_shape=[
            jax.ShapeDtypeStruct((2 * n_slabs, m // S5_T, S5_T * LANES // 2), BF16),
            jax.ShapeDtypeStruct((n_proj, m, LANES), BF16),
            jax.ShapeDtypeStruct((m, LANES), BF16),
        ],
        scratch_shapes=[
            pltpu.VMEM((tm, d), BF16),
            pltpu.VMEM((n_slabs, tm, LANES), F32),
            pltpu.VMEM((d_main + LANES, d), BF16),
            pltpu.VMEM((IN_PROJ_W_SLOTS, IN_PROJ_W_CHUNK, d), w_t.dtype),
            pltpu.SemaphoreType.DMA((IN_PROJ_W_SLOTS,)),
        ],
        compiler_params=pltpu.CompilerParams(
            dimension_semantics=("arbitrary",),
            vmem_limit_bytes=VMEM_LIMIT),
        name="in_proj",
    )(x2d, norm_w, w_t)


def _s5_kernel(x_ref, kb_ref, u_ref, v_ref, a_ref, o_ref,
               m_ref, wz_ref, wy_ref, z_ref, sp_ref, y_ref, *, rows_per_seq, n_seq):
    ns = S5_T // 2
    rows = x_ref.shape[1]
    half = a_ref.shape[-1] // 2
    hstate = half // 2
    pairs = hstate // LANES
    sub = LANES // (2 * pairs)
    wide = 2 * LANES

    @pl.when(pl.program_id(0) == 0)
    def _():
        m_ref[:, (ns - 1) * LANES:, :LANES] = jnp.zeros((2, LANES, LANES), BF16)
        wz_ref[...] = jnp.zeros_like(wz_ref)
        y_ref[...] = jnp.zeros_like(y_ref)

    lane = lax.broadcasted_iota(jnp.int32, (LANES, LANES), 1)
    for hh in range(2):
        for s in range(ns):
            for k in range(2):
                lag = ns - 2 + k - s
                if lag >= 0:
                    m_ref[hh, s * LANES:(s + 1) * LANES, k * LANES:(k + 1) * LANES] = kb_ref[hh, lag]
        for s in range(ns):
            for part in range(2):
                for rho in range(2):
                    for a in range(pairs):
                        r0 = rho * (LANES // 2) + a * sub
                        c0 = part * hstate + a * LANES
                        wz_ref[hh, s * LANES + r0:s * LANES + r0 + sub, c0:c0 + LANES] = (
                            u_ref[hh, s, part, r0:r0 + sub, :])
        for t in range(ns):
            for part in range(2):
                blk = v_ref[hh, t, part]
                for a in range(pairs):
                    r0 = part * hstate + a * LANES
                    wy_ref[hh, r0:r0 + LANES, t * LANES:(t + 1) * LANES] = (
                        jnp.where(((lane % (LANES // 2)) // sub) == a, blk, jnp.zeros_like(blk)))
        zh = jnp.dot(x_ref[hh], wz_ref[hh], preferred_element_type=F32)
        z_ref[:, hh * hstate:(hh + 1) * hstate] = zh[:, :hstate]
        z_ref[:, half + hh * hstate:half + (hh + 1) * hstate] = zh[:, hstate:]

    a_re = a_ref[:, :half]
    a_im = a_ref[:, half:]

    state = [(jnp.zeros((1, half), F32), jnp.zeros((1, half), F32)) for _ in range(n_seq)]
    for c in range(rows_per_seq):
        for b in range(n_seq):
            s_re, s_im = state[b]
            r = b * rows_per_seq + c
            sp_ref[r:r + 1, :half] = s_re
            sp_ref[r:r + 1, half:] = s_im
            z_re = z_ref[r:r + 1, :half]
            z_im = z_ref[r:r + 1, half:]
            state[b] = (a_re * s_re - a_im * s_im + z_re, a_re * s_im + a_im * s_re + z_im)

    spb = [jnp.concatenate([sp_ref[:, hh * hstate:(hh + 1) * hstate],
                            sp_ref[:, half + hh * hstate:half + (hh + 1) * hstate]],
                           axis=1).astype(BF16) for hh in range(2)]
    pitch = y_ref.shape[0] // rows
    low = lax.broadcasted_iota(jnp.int32, (rows, LANES), 1) < LANES // 2
    for n in range(ns // 2):
        cols = slice(n * wide, (n + 1) * wide)
        kk = (n + 1) * wide
        ys = []
        for hh in range(2):
            acc = jnp.dot(x_ref[hh, :, :kk], m_ref[hh, ns * LANES - kk:, :],
                          preferred_element_type=F32)
            acc = acc + jnp.dot(spb[hh], wy_ref[hh, :, cols], preferred_element_type=F32)
            ys.append(jax.nn.gelu(acc))
        for k in range(2):
            y0 = ys[0][:, k * LANES:(k + 1) * LANES]
            y1 = ys[1][:, k * LANES:(k + 1) * LANES]
            t = 2 * (2 * n + k)
            y_ref[pl.ds(t, rows, stride=pitch), :] = (
                jnp.where(low, y0, pltpu.roll(y1, LANES // 2, axis=1)))
            y_ref[pl.ds(t + 1, rows, stride=pitch), :] = (
                jnp.where(low, pltpu.roll(y0, LANES // 2, axis=1), y1))
    tokens = y_ref[...].reshape(rows, pitch, LANES)[:, :S5_T, :].reshape(rows * S5_T, LANES)
    o_ref[...] = tokens.astype(BF16)


def _s5(u2, kb, u_op, v_op, abar, *, n_seq):
    n_j, _, ns, _, _ = kb.shape
    rows = u2.shape[1]
    width = ns * LANES
    n_state = abar.shape[-1]
    return pl.pallas_call(
        functools.partial(_s5_kernel, rows_per_seq=rows // n_seq, n_seq=n_seq),
        grid=(n_j,),
        in_specs=[
            pl.BlockSpec((2, rows, width), lambda j: (j, 0, 0)),
            pl.BlockSpec((None, 2, ns, LANES, LANES), lambda j: (j, 0, 0, 0, 0)),
            pl.BlockSpec((None, 2, ns, 2, LANES, LANES), lambda j: (j, 0, 0, 0, 0, 0)),
            pl.BlockSpec((None, 2, ns, 2, LANES, LANES), lambda j: (j, 0, 0, 0, 0, 0)),
            pl.BlockSpec((None, 1, n_state), lambda j: (j, 0, 0)),
        ],
        out_specs=pl.BlockSpec((None, rows * S5_T, LANES), lambda j: (j, 0, 0)),
        out_shape=jax.ShapeDtypeStruct((n_j, rows * S5_T, LANES), BF16),
        scratch_shapes=[
            pltpu.VMEM((2, width, 2 * LANES), BF16),
            pltpu.VMEM((2, width, n_state // 2), BF16),
            pltpu.VMEM((2, n_state // 2, width), BF16),
            pltpu.VMEM((rows, n_state), F32),
            pltpu.VMEM((rows, n_state), F32),
            pltpu.VMEM((rows * S5_UNFOLD_PITCH, LANES), F32),
        ],
        compiler_params=pltpu.CompilerParams(
            dimension_semantics=("arbitrary",),
            vmem_limit_bytes=VMEM_LIMIT),
        name="s5",
    )(u2, kb, u_op, v_op, abar)


def _log_sigmoid(x):
    return jnp.minimum(x, 0.0) - jnp.log(1.0 + jnp.exp(-jnp.abs(x)))


def _gla_kernel(q_ref, k_ref, v_ref, gz_ref, gl_ref, gu_ref, gb_ref, nw_ref, o_ref, st_ref,
                *, chunk):
    n_heads, rows, dk = q_ref.shape
    n_chunks = rows // chunk
    d_gk = n_heads * dk

    @pl.when(pl.program_id(1) == 0)
    def _():
        st_ref[...] = jnp.zeros_like(st_ref)

    rank = gu_ref.shape[0]
    zg = jnp.dot(gl_ref[:, :rank], gu_ref[...].astype(BF16), preferred_element_type=F32) + gb_ref[...]
    lg = _log_sigmoid(zg) * (1.0 / GLA_TAU)

    hi = lg.astype(BF16)
    lo = (lg - hi.astype(F32)).astype(BF16)
    r_id = lax.broadcasted_iota(jnp.int32, (chunk, chunk), 0)
    c_id = lax.broadcasted_iota(jnp.int32, (chunk, chunk), 1)
    causal = c_id <= r_id
    tri = jnp.where(causal, 1.0, 0.0).astype(BF16)
    tri2 = jnp.concatenate([tri, tri], axis=1)
    b_parts, last_parts = [], []
    for n in range(n_chunks):
        rs = slice(n * chunk, (n + 1) * chunk)
        b_n = jnp.dot(tri2, jnp.concatenate([hi[rs], lo[rs]], axis=0), preferred_element_type=F32)
        b_parts.append(b_n)
        last_parts.append(jnp.broadcast_to(b_n[chunk - 1:chunk], (chunk, d_gk)))
    b = jnp.concatenate(b_parts, axis=0)
    e_pos = jnp.exp(b)
    e_neg = jnp.exp(-b)
    decay = jnp.exp(jnp.concatenate(last_parts, axis=0))

    for h in range(n_heads):
        hs = slice(h * dk, (h + 1) * dk)
        q_e = (q_ref[h].astype(F32) * ((dk ** -0.5) * e_pos[:, hs])).astype(BF16)
        k_f = k_ref[h].astype(F32) * e_neg[:, hs]
        k_e = k_f.astype(BF16)
        k_t = (k_f * decay[:, hs]).astype(BF16)
        v = jnp.concatenate([v_ref[2 * h], v_ref[2 * h + 1]], axis=1)

        chunks = [slice(n * chunk, (n + 1) * chunk) for n in range(n_chunks)]
        attn = [lax.dot_general(q_e[rs], k_e[rs], NT_DIMS, preferred_element_type=F32)
                for rs in chunks]
        kv_t = [lax.dot_general(v[rs], k_t[rs], TN_DIMS, preferred_element_type=F32)
                for rs in chunks]
        attn = [jnp.where(causal, a, 0.0).astype(BF16) for a in attn]
        st = st_ref[h]
        st_in = []
        for n in range(n_chunks):
            st_in.append(st.astype(BF16))
            st = decay[n * chunk:n * chunk + 1, hs] * st + kv_t[n]
        st_ref[h] = st
        outs = [jnp.dot(attn[n], v[rs], preferred_element_type=F32)
                + lax.dot_general(q_e[rs], st_in[n], NT_DIMS, preferred_element_type=F32)
                for n, rs in enumerate(chunks)]

        o = jnp.concatenate(outs, axis=0)
        ms = jnp.mean(o * o, axis=-1, keepdims=True)
        o = o * lax.rsqrt(ms + EPS) * nw_ref[...]
        gz = jnp.concatenate([gz_ref[2 * h], gz_ref[2 * h + 1]], axis=1).astype(F32)
        y = (o * (gz * jax.nn.sigmoid(gz))).astype(BF16)
        o_ref[2 * h] = y[:, :LANES]
        o_ref[2 * h + 1] = y[:, LANES:]


def _gla(proj, g_low, gate_up, gate_bias, norm_w, layer, *, n_seq, seq_len, rows, slab_q, slab_k,
         slab_v, slab_gz):
    m = proj.shape[1]
    rank = gate_up.shape[1]
    nb = seq_len // rows
    nh = GLA_HEADS
    dk = LANES
    dv = 2 * LANES
    tok = lambda b, n: b * nb + n
    return pl.pallas_call(
        functools.partial(_gla_kernel, chunk=GLA_CHUNK),
        grid=(n_seq, nb),
        in_specs=[
            pl.BlockSpec((nh, rows, LANES), lambda b, n: (slab_q // nh, tok(b, n), 0)),
            pl.BlockSpec((nh, rows, LANES), lambda b, n: (slab_k // nh, tok(b, n), 0)),
            pl.BlockSpec((2 * nh, rows, LANES), lambda b, n: (slab_v // (2 * nh), tok(b, n), 0)),
            pl.BlockSpec((2 * nh, rows, LANES), lambda b, n: (slab_gz // (2 * nh), tok(b, n), 0)),
            pl.BlockSpec((rows, LANES), lambda b, n: (tok(b, n), 0)),
            pl.BlockSpec((None, rank, nh * dk), lambda b, n: (layer, 0, 0)),
            pl.BlockSpec((1, nh * dk), lambda b, n: (layer, 0)),
            pl.BlockSpec((1, dv), lambda b, n: (layer, 0)),
        ],
        out_specs=pl.BlockSpec((2 * nh, rows, LANES), lambda b, n: (0, tok(b, n), 0)),
        out_shape=jax.ShapeDtypeStruct((2 * nh, m, LANES), BF16),
        scratch_shapes=[pltpu.VMEM((nh, dv, dk), F32)],
        compiler_params=pltpu.CompilerParams(
            dimension_semantics=("parallel", "arbitrary"),
            vmem_limit_bytes=VMEM_LIMIT),
        name="gla",
    )(proj, proj, proj, proj, g_low, gate_up.astype(F32), gate_bias.astype(F32), norm_w.astype(F32))


def _out_proj_kernel(g_ref, z_ref, yg_ref, x_ref, gw_hbm, gb_ref, wo_hbm, pw_ref,
                     o_ref, gw_ref, wo_ref, stage_ref, sem, *, layer):
    n_s5 = g_ref.shape[0]
    d_s5 = n_s5 * LANES
    tm = x_ref.shape[0]
    rc = OUT_PROJ_ROW_CHUNK

    ring = _WeightRing(stage_ref, sem)
    glu_jobs = ring.add(gw_hbm.at[layer], gw_ref)
    out_jobs = ring.add(wo_hbm.at[layer], wo_ref)
    split = len(out_jobs) * d_s5 // wo_ref.shape[0]

    def mix(after_glu, after_s5):
        for c in range(tm // rc):
            rows = slice(c * rc, (c + 1) * rc)
            g = jnp.concatenate([g_ref[s, rows, :] for s in range(n_s5)], axis=1)
            pre = jnp.dot(g, gw_ref[...], preferred_element_type=F32)
            if c == 0:
                after_glu()
            gate = jax.nn.sigmoid(pre + gb_ref[...])
            z = jnp.concatenate([z_ref[s, rows, :] for s in range(n_s5)], axis=1).astype(F32)
            y_s5 = (g.astype(F32) * gate * (z * jax.nn.sigmoid(z))).astype(BF16)
            mixed = jnp.dot(y_s5, wo_ref[:d_s5, :], preferred_element_type=F32)
            if c == 0:
                after_s5()
            y_gla = jnp.concatenate([yg_ref[s, rows, :] for s in range(yg_ref.shape[0])], axis=1)
            mixed = mixed + jnp.dot(y_gla, wo_ref[d_s5:, :], preferred_element_type=F32)
            ms = jnp.mean(mixed * mixed, axis=-1, keepdims=True)
            o_ref[rows, :] = x_ref[rows, :] + mixed * lax.rsqrt(ms + EPS) * pw_ref[...]

    @pl.when(pl.program_id(0) == 0)
    def _():
        ring.start()
        ring.land(glu_jobs)
        mix(lambda: ring.land(out_jobs[:split]), lambda: ring.land(out_jobs[split:]))

    @pl.when(pl.program_id(0) > 0)
    def _():
        mix(lambda: None, lambda: None)


def _out_proj(g_slabs, proj, y_gla, x2d, glu_w, glu_b, w_out, post_w, layer, *, tm, slab_z):
    m, d = x2d.shape
    n_s5 = g_slabs.shape[0]
    n_gla = y_gla.shape[0]
    d_s5 = n_s5 * LANES
    d_mix = w_out.shape[1]
    assert glu_w.dtype == w_out.dtype and glu_w.shape[2] <= d
    return pl.pallas_call(
        functools.partial(_out_proj_kernel, layer=layer),
        grid=(m // tm,),
        in_specs=[
            pl.BlockSpec((n_s5, tm, LANES), lambda i: (0, i, 0)),
            pl.BlockSpec((n_s5, tm, LANES), lambda i: (slab_z // n_s5, i, 0)),
            pl.BlockSpec((n_gla, tm, LANES), lambda i: (0, i, 0)),
            pl.BlockSpec((tm, d), lambda i: (i, 0)),
            pl.BlockSpec(memory_space=pl.ANY),
            pl.BlockSpec((1, d_s5), lambda i: (0, 0)),
            pl.BlockSpec(memory_space=pl.ANY),
            pl.BlockSpec((1, d), lambda i: (0, 0)),
        ],
        out_specs=pl.BlockSpec((tm, d), lambda i: (i, 0)),
        out_shape=jax.ShapeDtypeStruct((m, d), F32),
        scratch_shapes=[
            pltpu.VMEM((d_s5, d_s5), BF16),
            pltpu.VMEM((d_mix, d), BF16),
            pltpu.VMEM((OUT_PROJ_W_SLOTS, OUT_PROJ_W_CHUNK, d), w_out.dtype),
            pltpu.SemaphoreType.DMA((OUT_PROJ_W_SLOTS,)),
        ],
        compiler_params=pltpu.CompilerParams(
            dimension_semantics=("arbitrary",),
            vmem_limit_bytes=VMEM_LIMIT),
        name="out_proj",
    )(g_slabs, proj, y_gla, x2d, glu_w, glu_b, w_out, post_w)


def kernel(x, pre_norm_w, w_in, s5_A_re, s5_A_im, s5_B_re, s5_B_im, s5_C_re, s5_C_im, s5_D,
           s5_log_dt, s5_glu_w, s5_glu_b, gla_gate_up, gla_gate_bias, gla_norm_w, w_out,
           post_norm_w):
    bsz, seq_len, d_model = x.shape
    depth = w_in.shape[0]
    d_in = w_in.shape[2]
    d_s5 = s5_glu_w.shape[1]
    rank, d_gk = gla_gate_up.shape[1:]
    d_gv = GLA_HEADS * gla_norm_w.shape[1]
    m = bsz * seq_len
    d_main = 2 * d_s5 + 2 * d_gk + 2 * d_gv
    tn = d_s5
    assert d_s5 % LANES == 0 and d_gk == GLA_HEADS * LANES and d_gv == GLA_HEADS * 2 * LANES
    assert d_in == d_main + rank and rank <= LANES and d_main % tn == 0 and 2 * d_gk == tn
    assert seq_len % (S5_T * 8) == 0 and w_out.shape[1] == 2 * d_s5

    n_s5 = d_s5 // LANES
    slab_z = 0
    slab_q = n_s5
    slab_k = slab_q + d_gk // LANES
    slab_v = slab_k + d_gk // LANES
    slab_gz = slab_v + d_gv // LANES

    resid = x.astype(F32).reshape(m, d_model)
    for l in range(depth):
        kb, u_op, v_op, abar = _s5_ops(s5_A_re[l], s5_A_im[l], s5_B_re[l], s5_B_im[l],
                                       s5_C_re[l], s5_C_im[l], s5_log_dt, s5_D, l)

        u2, proj, g_low = _in_proj(resid, pre_norm_w[l].astype(F32)[None],
                                   jnp.swapaxes(w_in, 1, 2), l, d_main=d_main, tm=512, tn=tn)

        g_slabs = _s5(u2, kb, u_op, v_op, abar, n_seq=bsz)

        y_gla = _gla(proj, g_low, gla_gate_up, gla_gate_bias, gla_norm_w, l,
                     n_seq=bsz, seq_len=seq_len, rows=512,
                     slab_q=slab_q, slab_k=slab_k, slab_v=slab_v, slab_gz=slab_gz)

        resid = _out_proj(g_slabs, proj, y_gla, resid, s5_glu_w, s5_glu_b[l].astype(F32)[None],
                          w_out, post_norm_w[l].astype(F32)[None], l, tm=512, slab_z=slab_z)
    return resid.reshape(bsz, seq_len, d_model).astype(x.dtype)
```

```python
import functools

import jax
import jax.numpy as jnp
from jax import lax
from jax.experimental import pallas as pl
from jax.experimental.pallas import tpu as pltpu

F32 = jnp.float32
BF16 = jnp.bfloat16

S5_GROUP = 16
GLA_HEADS = 4
GLA_TAU = 16.0
GLA_CHUNK = 64
EPS = 1e-6

LANES = 128
S5_T = 16
S5_FOLD_PITCH = 24
OUT_PROJ_ROW_CHUNK = 256
IN_PROJ_W_CHUNK = 256
IN_PROJ_W_SLOTS = 4
OUT_PROJ_W_CHUNK = 256
OUT_PROJ_W_SLOTS = 4
VMEM_LIMIT = 56 * 1024 * 1024


NT_DIMS = (((1,), (1,)), ((), ()))
TN_DIMS = (((0,), (0,)), ((), ()))


def _cmul(ar, ai, br, bi):
    return ar * br - ai * bi, ar * bi + ai * br


class _WeightRing:
    def __init__(self, stage_ref, sem):
        self.stage, self.sem, self.jobs = stage_ref, sem, []
        self.n_slots, self.chunk = stage_ref.shape[:2]

    def add(self, src, dst_ref):
        n, width = src.shape
        first = len(self.jobs)
        for r0 in range(0, n, self.chunk):
            self.jobs.append((src, dst_ref, r0, min(self.chunk, n - r0), width))
        return list(range(first, len(self.jobs)))

    def _copy(self, k):
        src, _, r0, rows, width = self.jobs[k]
        slot = k % self.n_slots
        return pltpu.make_async_copy(
            src.at[pl.ds(r0, rows), :],
            self.stage.at[slot, pl.ds(0, rows), pl.ds(0, width)], self.sem.at[slot])

    def start(self):
        for k in range(min(self.n_slots, len(self.jobs))):
            self._copy(k).start()

    def land(self, ks):
        for k in ks:
            _, dst_ref, r0, rows, width = self.jobs[k]
            self._copy(k).wait()
            dst_ref[r0:r0 + rows, :width] = (
                self.stage[k % self.n_slots, :rows, :width].astype(dst_ref.dtype))
            if k + self.n_slots < len(self.jobs):
                self._copy(k + self.n_slots).start()


def _s5_ops_kernel(*refs, n_state):
    (a_re_ref, a_im_ref, ldt_row_ref, d_ref, b_re_ref, b_im_ref, c_re_ref, c_im_ref,
     kb_ref, u_ref, v_ref, ab_ref) = refs
    nt = S5_T
    ns = nt // 2
    half = LANES // 2
    gps, p = a_re_ref.shape

    def terms(t):
        hi = t.astype(BF16).astype(F32)
        mid = (t - hi).astype(BF16).astype(F32)
        return hi, mid, t - hi - mid

    def select(t, onehot):
        return sum(lax.dot_general(x, onehot, TN_DIMS, preferred_element_type=F32)
                   for x in terms(t))

    def spread_rows(t):
        return sum(lax.dot_general(spread, x, TN_DIMS, preferred_element_type=F32)
                   for x in terms(t))

    group_of_lane = lax.broadcasted_iota(jnp.int32, (gps, LANES), 1) // S5_GROUP
    spread = jnp.where(group_of_lane == lax.broadcasted_iota(jnp.int32, (gps, LANES), 0), 1.0, 0.0)
    eye = jnp.where(lax.broadcasted_iota(jnp.int32, (LANES, LANES), 0)
                    == lax.broadcasted_iota(jnp.int32, (LANES, LANES), 1), 1.0, 0.0)
    dup = lambda t: jnp.concatenate([t, t], axis=1)
    are_c = jnp.minimum(a_re_ref[...], -1e-4)
    aim_c = a_im_ref[...]
    n_groups = ldt_row_ref.shape[1]
    mine = (lax.broadcasted_iota(jnp.int32, (gps, n_groups), 1)
            == pl.program_id(0) * gps + lax.broadcasted_iota(jnp.int32, (gps, n_groups), 0))
    ldt = jnp.sum(jnp.where(mine, ldt_row_ref[...], 0.0), axis=1, keepdims=True)
    dt = jnp.exp(ldt)
    mag = jnp.exp(are_c * dt)
    abr_c = mag * jnp.cos(aim_c * dt)
    abi_c = mag * jnp.sin(aim_c * dt)
    are, aim, abr, abi = [spread_rows(dup(t)) for t in (are_c, aim_c, abr_c, abi_c)]
    b_re, b_im = [dup(r[...].reshape(gps * S5_GROUP, p)) for r in (b_re_ref, b_im_ref)]
    nbr, nbi = [select(t, spread) for t in (abr_c, abi_c)]
    cr, ci = [select(r[...].reshape(gps * S5_GROUP, p), eye) for r in (c_re_ref, c_im_ref)]

    den = are * are + aim * aim
    nr = abr - 1.0
    fr = (nr * are + abi * aim) / den
    fi = (abi * are - nr * aim) / den
    xr, xi = _cmul(fr, fi, b_re, b_im)

    row = lax.broadcasted_iota(jnp.int32, (LANES, LANES), 0)
    lane = lax.broadcasted_iota(jnp.int32, (LANES, LANES), 1)
    own_half = ((row // S5_GROUP) % 2) == (lane // half)
    same_group = ((row % half) // S5_GROUP) == ((lane % half) // S5_GROUP)
    first_half = lane < half

    def split(t):
        hi = t.astype(BF16)
        return hi, (t - hi.astype(F32)).astype(BF16)

    c_stack = jnp.concatenate([cr, ci], axis=0)
    c_swap = pltpu.roll(c_stack, half, axis=1)
    c_hi, c_lo = split(jnp.concatenate([c_stack, c_swap], axis=1))
    c_terms = jnp.concatenate([c_hi, c_lo, c_hi], axis=0)
    kers, kers_swap, xs = [], [], []
    for tau in range(nt):
        l_hi, l_lo = split(jnp.where(first_half, xr, -xi))
        ker = jnp.dot(jnp.concatenate([l_hi, l_hi, l_lo], axis=1), c_terms,
                      preferred_element_type=F32)
        kers.append(ker[:, :LANES])
        kers_swap.append(ker[:, LANES:])
        xs.append((jnp.where(own_half, xr, 0.0).astype(BF16),
                   jnp.where(own_half, xi, 0.0).astype(BF16)))
        if tau + 1 < nt:
            xr, xi = _cmul(xr, xi, abr, abi)

    d_diag = jnp.where(row == lane, d_ref[...], 0.0)
    kers[0] = kers[0] + d_diag
    kers_swap[0] = kers_swap[0] + pltpu.roll(d_diag, half, axis=1)

    zero = jnp.zeros((half, LANES), F32)
    first_half_rows = lax.broadcasted_iota(jnp.int32, (half, LANES), 1) < half
    for hh in range(2):
        rows = slice(hh * half, (hh + 1) * half)
        low, high = (kers, kers_swap) if hh == 0 else (kers_swap, kers)
        pick = lambda src, tau: src[tau][rows, :] if tau >= 0 else zero
        for d in range(ns):
            top = jnp.where(first_half_rows, pick(low, 2 * d), pick(high, 2 * d + 1))
            bot = jnp.where(first_half_rows, pick(low, 2 * d - 1), pick(high, 2 * d))
            blk = jnp.concatenate([top, bot], axis=0)
            kb_ref[hh, d] = jnp.where(same_group, blk, 0.0).astype(BF16)
        for s2 in range(ns):
            for part in range(2):
                u_ref[hh, s2, part] = jnp.concatenate(
                    [xs[nt - 1 - 2 * s2][part][rows, :], xs[nt - 2 - 2 * s2][part][rows, :]], axis=0)

    pr, pi = abr, abi
    for _ in range(4):
        pr, pi = _cmul(pr, pi, pr, pi)
    assert nt == 16
    pairs = LANES // (2 * S5_GROUP)
    for part, val in enumerate((pr, pi)):
        for a in range(pairs):
            r0 = 2 * a * S5_GROUP
            piece = jnp.where(first_half[:1], val[r0:r0 + 1], val[r0 + S5_GROUP:r0 + S5_GROUP + 1])
            c0 = part * (n_state // 2) + a * LANES
            ab_ref[:, c0:c0 + LANES] = piece

    lane_n = lax.broadcasted_iota(jnp.int32, (half, LANES), 1)
    parity = (lane_n // S5_GROUP) % 2
    first_n = lane_n < half
    pr, pi = nbr, nbi
    outs = []
    for t in range(nt):
        wr, wi = _cmul(cr, ci, pr, pi)
        outs.append((wr, -wi))
        if t + 1 < nt:
            pr, pi = _cmul(pr, pi, nbr, nbi)
    for t2 in range(ns):
        for part in range(2):
            even, odd = outs[2 * t2][part], outs[2 * t2 + 1][part]
            packed = (jnp.where(first_n, even, pltpu.roll(odd, half, axis=1)),
                      jnp.where(first_n, pltpu.roll(even, half, axis=1), odd))
            for hh in range(2):
                for q in range(2):
                    v_ref[hh, t2, part, q * half:(q + 1) * half, :] = (
                        jnp.where(parity == q, packed[hh], 0.0).astype(BF16))


def _s5_ops(a_re, a_im, b_re, b_im, c_re, c_im, log_dt, d_skip, layer):
    g, p = a_re.shape
    h = S5_GROUP
    gps = LANES // h
    n_j = g // gps
    assert 2 * p == LANES and g % gps == 0
    n_state = 2 * gps * p
    f = lambda t: t.astype(F32)
    per_group = [f(t).reshape(n_j, gps, p) for t in (a_re, a_im)]
    per_chan = [f(t).reshape(n_j, gps, h, p)
                for t in (jnp.swapaxes(b_re, 1, 2), jnp.swapaxes(b_im, 1, 2), c_re, c_im)]
    ns = S5_T // 2
    return pl.pallas_call(
        functools.partial(_s5_ops_kernel, n_state=n_state),
        grid=(n_j,),
        in_specs=([pl.BlockSpec((None, gps, p), lambda j: (j, 0, 0))] * 2
                  + [pl.BlockSpec((1, g), lambda j: (layer, 0)),
                     pl.BlockSpec((1, LANES), lambda j: (layer, j))]
                  + [pl.BlockSpec((None, gps, h, p), lambda j: (j, 0, 0, 0))] * 4),
        out_specs=[
            pl.BlockSpec((None, 2, ns, LANES, LANES), lambda j: (j, 0, 0, 0, 0)),
            pl.BlockSpec((None, 2, ns, 2, LANES, LANES), lambda j: (j, 0, 0, 0, 0, 0)),
            pl.BlockSpec((None, 2, ns, 2, LANES, LANES), lambda j: (j, 0, 0, 0, 0, 0)),
            pl.BlockSpec((None, 1, n_state), lambda j: (j, 0, 0)),
        ],
        out_shape=[
            jax.ShapeDtypeStruct((n_j, 2, ns, LANES, LANES), BF16),
            jax.ShapeDtypeStruct((n_j, 2, ns, 2, LANES, LANES), BF16),
            jax.ShapeDtypeStruct((n_j, 2, ns, 2, LANES, LANES), BF16),
            jax.ShapeDtypeStruct((n_j, 1, n_state), F32),
        ],
        compiler_params=pltpu.CompilerParams(dimension_semantics=("parallel",)),
        name="s5_ops",
    )(*per_group, f(log_dt), f(d_skip), *per_chan)


def _in_proj_kernel(x_ref, nw_ref, w_hbm, u_ref, p_ref, l_ref, h_ref, s_ref, w_ref, stage_ref, sem,
                    *, layer, d_main, tn):
    tm = x_ref.shape[0]
    n_slabs = tn // LANES
    fold = tm // S5_T

    n_rows = w_hbm.shape[1]
    ring = _WeightRing(stage_ref, sem)
    jobs = ring.add(w_hbm.at[layer], w_ref)
    per_dot = tn // ring.chunk
    assert per_dot <= ring.n_slots and tn % ring.chunk == 0

    def project(after_dot):
        xf = x_ref[...]
        h_ref[...] = (xf * nw_ref[...]).astype(BF16)
        rs = lax.rsqrt(jnp.mean(xf * xf, axis=-1, keepdims=True) + EPS)
        for n in range(d_main // tn):
            res = lax.dot_general(h_ref[...], w_ref[n * tn:(n + 1) * tn, :], NT_DIMS,
                                  preferred_element_type=F32) * rs
            after_dot(n)
            if n == 0:
                pitch = s_ref.shape[1] // fold
                for s in range(n_slabs):
                    for c in range(fold):
                        s_ref[s, c * pitch:c * pitch + S5_T, :] = (
                            res[c * S5_T:(c + 1) * S5_T, s * LANES:(s + 1) * LANES])
                low = lax.broadcasted_iota(jnp.int32, (fold, LANES), 1) < LANES // 2
                for s in range(n_slabs):
                    for t2 in range(S5_T // 2):
                        even = s_ref[s, pl.ds(2 * t2, fold, stride=pitch), :]
                        odd = s_ref[s, pl.ds(2 * t2 + 1, fold, stride=pitch), :]
                        cols = slice(t2 * LANES, (t2 + 1) * LANES)
                        u_ref[2 * s, :, cols] = (
                            jnp.where(low, even, pltpu.roll(odd, LANES // 2, axis=1)).astype(BF16))
                        u_ref[2 * s + 1, :, cols] = (
                            jnp.where(low, pltpu.roll(even, LANES // 2, axis=1), odd).astype(BF16))
            else:
                for s in range(n_slabs):
                    p_ref[(n - 1) * n_slabs + s] = res[:, s * LANES:(s + 1) * LANES].astype(BF16)
        low_rank = lax.dot_general(h_ref[...], w_ref[d_main:d_main + LANES, :], NT_DIMS,
                                   preferred_element_type=F32) * rs
        l_ref[...] = low_rank.astype(BF16)

    @pl.when(pl.program_id(0) == 0)
    def _():
        w_ref[n_rows:, :] = jnp.zeros((w_ref.shape[0] - n_rows, w_ref.shape[1]), BF16)
        ring.start()
        ring.land(jobs[:per_dot])

        def stage_next(n):
            ring.land(jobs[(n + 1) * per_dot:(n + 2) * per_dot])

        project(stage_next)

    @pl.when(pl.program_id(0) > 0)
    def _():
        project(lambda n: None)


def _in_proj(x2d, norm_w, w_t, layer, *, d_main, tm, tn):
    m, d = x2d.shape
    n_slabs = tn // LANES
    n_proj = (d_main // tn - 1) * n_slabs
    assert d_main < w_t.shape[1] <= d_main + LANES
    return pl.pallas_call(
        functools.partial(_in_proj_kernel, layer=layer, d_main=d_main, tn=tn),
        grid=(m // tm,),
        in_specs=[
            pl.BlockSpec((tm, d), lambda i: (i, 0)),
            pl.BlockSpec((1, d), lambda i: (0, 0)),
            pl.BlockSpec(memory_space=pl.ANY),
        ],
        out_specs=[
            pl.BlockSpec((2 * n_slabs, tm // S5_T, S5_T * LANES // 2), lambda i: (0, i, 0)),
            pl.BlockSpec((n_proj, tm, LANES), lambda i: (0, i, 0)),
            pl.BlockSpec((tm, LANES), lambda i: (i, 0)),
        ],
        out_shape=[
            jax.ShapeDtypeStruct((2 * n_slabs, m // S5_T, S5_T * LANES // 2), BF16),
            jax.ShapeDtypeStruct((n_proj, m, LANES), BF16),
            jax.ShapeDtypeStruct((m, LANES), BF16),
        ],
        scratch_shapes=[
            pltpu.VMEM((tm, d), BF16),
            pltpu.VMEM((n_slabs, tm // S5_T * S5_FOLD_PITCH, LANES), F32),
            pltpu.VMEM((d_main + LANES, d), BF16),
            pltpu.VMEM((IN_PROJ_W_SLOTS, IN_PROJ_W_CHUNK, d), w_t.dtype),
            pltpu.SemaphoreType.DMA((IN_PROJ_W_SLOTS,)),
        ],
        compiler_params=pltpu.CompilerParams(
            dimension_semantics=("arbitrary",),
            vmem_limit_bytes=VMEM_LIMIT),
        name="in_proj",
    )(x2d, norm_w, w_t)


def _s5_kernel(x_ref, kb_ref, u_ref, v_ref, a_ref, o_ref,
               m_ref, wz_ref, wy_ref, z_ref, sp_ref, y_ref, *, rows_per_seq, n_seq):
    ns = S5_T // 2
    rows = x_ref.shape[1]
    half = a_ref.shape[-1] // 2
    hstate = half // 2
    pairs = hstate // LANES
    sub = LANES // (2 * pairs)
    wide = 2 * LANES

    @pl.when(pl.program_id(0) == 0)
    def _():
        m_ref[:, (ns - 1) * LANES:, :LANES] = jnp.zeros((2, LANES, LANES), BF16)
        wz_ref[...] = jnp.zeros_like(wz_ref)
        y_ref[...] = jnp.zeros_like(y_ref)

    lane = lax.broadcasted_iota(jnp.int32, (LANES, LANES), 1)
    for hh in range(2):
        for s in range(ns):
            for k in range(2):
                lag = ns - 2 + k - s
                if lag >= 0:
                    m_ref[hh, s * LANES:(s + 1) * LANES, k * LANES:(k + 1) * LANES] = kb_ref[hh, lag]
        for s in range(ns):
            for part in range(2):
                for rho in range(2):
                    for a in range(pairs):
                        r0 = rho * (LANES // 2) + a * sub
                        c0 = part * hstate + a * LANES
                        wz_ref[hh, s * LANES + r0:s * LANES + r0 + sub, c0:c0 + LANES] = (
                            u_ref[hh, s, part, r0:r0 + sub, :])
        for t in range(ns):
            for part in range(2):
                blk = v_ref[hh, t, part]
                for a in range(pairs):
                    r0 = part * hstate + a * LANES
                    wy_ref[hh, r0:r0 + LANES, t * LANES:(t + 1) * LANES] = (
                        jnp.where(((lane % (LANES // 2)) // sub) == a, blk, jnp.zeros_like(blk)))
        zh = jnp.dot(x_ref[hh], wz_ref[hh], preferred_element_type=F32)
        z_ref[:, hh * hstate:(hh + 1) * hstate] = zh[:, :hstate]
        z_ref[:, half + hh * hstate:half + (hh + 1) * hstate] = zh[:, hstate:]

    a_re = a_ref[:, :half]
    a_im = a_ref[:, half:]

    state = [(jnp.zeros((1, half), F32), jnp.zeros((1, half), F32)) for _ in range(n_seq)]
    for c in range(rows_per_seq):
        for b in range(n_seq):
            s_re, s_im = state[b]
            r = b * rows_per_seq + c
            sp_ref[r:r + 1, :half] = s_re
            sp_ref[r:r + 1, half:] = s_im
            z_re = z_ref[r:r + 1, :half]
            z_im = z_ref[r:r + 1, half:]
            state[b] = (a_re * s_re - a_im * s_im + z_re, a_re * s_im + a_im * s_re + z_im)

    spb = [jnp.concatenate([sp_ref[:, hh * hstate:(hh + 1) * hstate],
                            sp_ref[:, half + hh * hstate:half + (hh + 1) * hstate]],
                           axis=1).astype(BF16) for hh in range(2)]
    pitch = y_ref.shape[0] // rows
    low = lax.broadcasted_iota(jnp.int32, (rows, LANES), 1) < LANES // 2
    for n in range(ns // 2):
        cols = slice(n * wide, (n + 1) * wide)
        kk = (n + 1) * wide
        ys = []
        for hh in range(2):
            acc = jnp.dot(x_ref[hh, :, :kk], m_ref[hh, ns * LANES - kk:, :],
                          preferred_element_type=F32)
            acc = acc + jnp.dot(spb[hh], wy_ref[hh, :, cols], preferred_element_type=F32)
            ys.append(jax.nn.gelu(acc))
        for k in range(2):
            y0 = ys[0][:, k * LANES:(k + 1) * LANES]
            y1 = ys[1][:, k * LANES:(k + 1) * LANES]
            t = 2 * (2 * n + k)
            y_ref[pl.ds(t, rows, stride=pitch), :] = (
                jnp.where(low, y0, pltpu.roll(y1, LANES // 2, axis=1)))
            y_ref[pl.ds(t + 1, rows, stride=pitch), :] = (
                jnp.where(low, pltpu.roll(y0, LANES // 2, axis=1), y1))
    tokens = y_ref[...].reshape(rows, pitch, LANES)[:, :S5_T, :].reshape(rows * S5_T, LANES)
    o_ref[...] = tokens.astype(BF16)


def _s5(u2, kb, u_op, v_op, abar, *, n_seq):
    n_j, _, ns, _, _ = kb.shape
    rows = u2.shape[1]
    width = ns * LANES
    n_state = abar.shape[-1]
    return pl.pallas_call(
        functools.partial(_s5_kernel, rows_per_seq=rows // n_seq, n_seq=n_seq),
        grid=(n_j,),
        in_specs=[
            pl.BlockSpec((2, rows, width), lambda j: (j, 0, 0)),
            pl.BlockSpec((None, 2, ns, LANES, LANES), lambda j: (j, 0, 0, 0, 0)),
            pl.BlockSpec((None, 2, ns, 2, LANES, LANES), lambda j: (j, 0, 0, 0, 0, 0)),
            pl.BlockSpec((None, 2, ns, 2, LANES, LANES), lambda j: (j, 0, 0, 0, 0, 0)),
            pl.BlockSpec((None, 1, n_state), lambda j: (j, 0, 0)),
        ],
        out_specs=pl.BlockSpec((None, rows * S5_T, LANES), lambda j: (j, 0, 0)),
        out_shape=jax.ShapeDtypeStruct((n_j, rows * S5_T, LANES), BF16),
        scratch_shapes=[
            pltpu.VMEM((2, width, 2 * LANES), BF16),
            pltpu.VMEM((2, width, n_state // 2), BF16),
            pltpu.VMEM((2, n_state // 2, width), BF16),
            pltpu.VMEM((rows, n_state), F32),
            pltpu.VMEM((rows, n_state), F32),
            pltpu.VMEM((rows * S5_FOLD_PITCH, LANES), F32),
        ],
        compiler_params=pltpu.CompilerParams(
            dimension_semantics=("arbitrary",),
            vmem_limit_bytes=VMEM_LIMIT),
        name="s5",
    )(u2, kb, u_op, v_op, abar)


def _log_sigmoid(x):
    return jnp.minimum(x, 0.0) - jnp.log(1.0 + jnp.exp(-jnp.abs(x)))


def _gla_kernel(q_ref, k_ref, v_ref, gz_ref, gl_ref, gu_ref, gb_ref, nw_ref, o_ref, st_ref,
                *, chunk):
    n_heads, rows, dk = q_ref.shape
    n_chunks = rows // chunk
    d_gk = n_heads * dk

    @pl.when(pl.program_id(1) == 0)
    def _():
        st_ref[...] = jnp.zeros_like(st_ref)

    rank = gu_ref.shape[0]
    zg = jnp.dot(gl_ref[:, :rank], gu_ref[...].astype(BF16), preferred_element_type=F32) + gb_ref[...]
    lg = _log_sigmoid(zg) * (1.0 / GLA_TAU)

    hi = lg.astype(BF16)
    lo = (lg - hi.astype(F32)).astype(BF16)
    r_id = lax.broadcasted_iota(jnp.int32, (chunk, chunk), 0)
    c_id = lax.broadcasted_iota(jnp.int32, (chunk, chunk), 1)
    causal = c_id <= r_id
    tri = jnp.where(causal, 1.0, 0.0).astype(BF16)
    tri2 = jnp.concatenate([tri, tri], axis=1)
    b_parts, last_parts = [], []
    for n in range(n_chunks):
        rs = slice(n * chunk, (n + 1) * chunk)
        b_n = jnp.dot(tri2, jnp.concatenate([hi[rs], lo[rs]], axis=0), preferred_element_type=F32)
        b_parts.append(b_n)
        last_parts.append(jnp.broadcast_to(b_n[chunk - 1:chunk], (chunk, d_gk)))
    b = jnp.concatenate(b_parts, axis=0)
    e_pos = jnp.exp(b)
    e_neg = jnp.exp(-b)
    decay = jnp.exp(jnp.concatenate(last_parts, axis=0))

    for h in range(n_heads):
        hs = slice(h * dk, (h + 1) * dk)
        q_e = (q_ref[h].astype(F32) * ((dk ** -0.5) * e_pos[:, hs])).astype(BF16)
        k_f = k_ref[h].astype(F32) * e_neg[:, hs]
        k_e = k_f.astype(BF16)
        k_t = (k_f * decay[:, hs]).astype(BF16)
        v = jnp.concatenate([v_ref[2 * h], v_ref[2 * h + 1]], axis=1)

        chunks = [slice(n * chunk, (n + 1) * chunk) for n in range(n_chunks)]
        attn = [lax.dot_general(q_e[rs], k_e[rs], NT_DIMS, preferred_element_type=F32)
                for rs in chunks]
        kv_t = [lax.dot_general(v[rs], k_t[rs], TN_DIMS, preferred_element_type=F32)
                for rs in chunks]
        attn = [jnp.where(causal, a, 0.0).astype(BF16) for a in attn]
        st = st_ref[h]
        st_in = []
        for n in range(n_chunks):
            st_in.append(st.astype(BF16))
            st = decay[n * chunk:n * chunk + 1, hs] * st + kv_t[n]
        st_ref[h] = st
        outs = [jnp.dot(attn[n], v[rs], preferred_element_type=F32)
                + lax.dot_general(q_e[rs], st_in[n], NT_DIMS, preferred_element_type=F32)
                for n, rs in enumerate(chunks)]

        o = jnp.concatenate(outs, axis=0)
        ms = jnp.mean(o * o, axis=-1, keepdims=True)
        o = o * lax.rsqrt(ms + EPS) * nw_ref[...]
        gz = jnp.concatenate([gz_ref[2 * h], gz_ref[2 * h + 1]], axis=1).astype(F32)
        y = (o * (gz * jax.nn.sigmoid(gz))).astype(BF16)
        o_ref[2 * h] = y[:, :LANES]
        o_ref[2 * h + 1] = y[:, LANES:]


def _gla(proj, g_low, gate_up, gate_bias, norm_w, layer, *, n_seq, seq_len, rows, slab_q, slab_k,
         slab_v, slab_gz):
    m = proj.shape[1]
    rank = gate_up.shape[1]
    nb = seq_len // rows
    nh = GLA_HEADS
    dk = LANES
    dv = 2 * LANES
    tok = lambda b, n: b * nb + n
    return pl.pallas_call(
        functools.partial(_gla_kernel, chunk=GLA_CHUNK),
        grid=(n_seq, nb),
        in_specs=[
            pl.BlockSpec((nh, rows, LANES), lambda b, n: (slab_q // nh, tok(b, n), 0)),
            pl.BlockSpec((nh, rows, LANES), lambda b, n: (slab_k // nh, tok(b, n), 0)),
            pl.BlockSpec((2 * nh, rows, LANES), lambda b, n: (slab_v // (2 * nh), tok(b, n), 0)),
            pl.BlockSpec((2 * nh, rows, LANES), lambda b, n: (slab_gz // (2 * nh), tok(b, n), 0)),
            pl.BlockSpec((rows, LANES), lambda b, n: (tok(b, n), 0)),
            pl.BlockSpec((None, rank, nh * dk), lambda b, n: (layer, 0, 0)),
            pl.BlockSpec((1, nh * dk), lambda b, n: (layer, 0)),
            pl.BlockSpec((1, dv), lambda b, n: (layer, 0)),
        ],
        out_specs=pl.BlockSpec((2 * nh, rows, LANES), lambda b, n: (0, tok(b, n), 0)),
        out_shape=jax.ShapeDtypeStruct((2 * nh, m, LANES), BF16),
        scratch_shapes=[pltpu.VMEM((nh, dv, dk), F32)],
        compiler_params=pltpu.CompilerParams(
            dimension_semantics=("parallel", "arbitrary"),
            vmem_limit_bytes=VMEM_LIMIT),
        name="gla",
    )(proj, proj, proj, proj, g_low, gate_up.astype(F32), gate_bias.astype(F32), norm_w.astype(F32))


def _out_proj_kernel(g_ref, z_ref, yg_ref, x_ref, gw_hbm, gb_ref, wo_hbm, pw_ref,
                     o_ref, gw_ref, wo_ref, stage_ref, sem, *, layer):
    n_s5 = g_ref.shape[0]
    d_s5 = n_s5 * LANES
    tm = x_ref.shape[0]
    rc = OUT_PROJ_ROW_CHUNK

    ring = _WeightRing(stage_ref, sem)
    glu_jobs = ring.add(gw_hbm.at[layer], gw_ref)
    out_jobs = ring.add(wo_hbm.at[layer], wo_ref)
    split = len(out_jobs) * d_s5 // wo_ref.shape[0]

    def mix(after_glu, after_s5):
        for c in range(tm // rc):
            rows = slice(c * rc, (c + 1) * rc)
            g = jnp.concatenate([g_ref[s, rows, :] for s in range(n_s5)], axis=1)
            pre = jnp.dot(g, gw_ref[...], preferred_element_type=F32)
            if c == 0:
                after_glu()
            gate = jax.nn.sigmoid(pre + gb_ref[...])
            z = jnp.concatenate([z_ref[s, rows, :] for s in range(n_s5)], axis=1).astype(F32)
            y_s5 = (g.astype(F32) * gate * (z * jax.nn.sigmoid(z))).astype(BF16)
            mixed = jnp.dot(y_s5, wo_ref[:d_s5, :], preferred_element_type=F32)
            if c == 0:
                after_s5()
            y_gla = jnp.concatenate([yg_ref[s, rows, :] for s in range(yg_ref.shape[0])], axis=1)
            mixed = mixed + jnp.dot(y_gla, wo_ref[d_s5:, :], preferred_element_type=F32)
            ms = jnp.mean(mixed * mixed, axis=-1, keepdims=True)
            o_ref[rows, :] = x_ref[rows, :] + mixed * lax.rsqrt(ms + EPS) * pw_ref[...]

    @pl.when(pl.program_id(0) == 0)
    def _():
        ring.start()
        ring.land(glu_jobs)
        mix(lambda: ring.land(out_jobs[:split]), lambda: ring.land(out_jobs[split:]))

    @pl.when(pl.program_id(0) > 0)
    def _():
        mix(lambda: None, lambda: None)


def _out_proj(g_slabs, proj, y_gla, x2d, glu_w, glu_b, w_out, post_w, layer, *, tm, slab_z):
    m, d = x2d.shape
    n_s5 = g_slabs.shape[0]
    n_gla = y_gla.shape[0]
    d_s5 = n_s5 * LANES
    d_mix = w_out.shape[1]
    assert glu_w.dtype == w_out.dtype and glu_w.shape[2] <= d
    return pl.pallas_call(
        functools.partial(_out_proj_kernel, layer=layer),
        grid=(m // tm,),
        in_specs=[
            pl.BlockSpec((n_s5, tm, LANES), lambda i: (0, i, 0)),
            pl.BlockSpec((n_s5, tm, LANES), lambda i: (slab_z // n_s5, i, 0)),
            pl.BlockSpec((n_gla, tm, LANES), lambda i: (0, i, 0)),
            pl.BlockSpec((tm, d), lambda i: (i, 0)),
            pl.BlockSpec(memory_space=pl.ANY),
            pl.BlockSpec((1, d_s5), lambda i: (0, 0)),
            pl.BlockSpec(memory_space=pl.ANY),
            pl.BlockSpec((1, d), lambda i: (0, 0)),
        ],
        out_specs=pl.BlockSpec((tm, d), lambda i: (i, 0)),
        out_shape=jax.ShapeDtypeStruct((m, d), F32),
        scratch_shapes=[
            pltpu.VMEM((d_s5, d_s5), BF16),
            pltpu.VMEM((d_mix, d), BF16),
            pltpu.VMEM((OUT_PROJ_W_SLOTS, OUT_PROJ_W_CHUNK, d), w_out.dtype),
            pltpu.SemaphoreType.DMA((OUT_PROJ_W_SLOTS,)),
        ],
        compiler_params=pltpu.CompilerParams(
            dimension_semantics=("arbitrary",),
            vmem_limit_bytes=VMEM_LIMIT),
        name="out_proj",
    )(g_slabs, proj, y_gla, x2d, glu_w, glu_b, w_out, post_w)


def kernel(x, pre_norm_w, w_in, s5_A_re, s5_A_im, s5_B_re, s5_B_im, s5_C_re, s5_C_im, s5_D,
           s5_log_dt, s5_glu_w, s5_glu_b, gla_gate_up, gla_gate_bias, gla_norm_w, w_out,
           post_norm_w):
    bsz, seq_len, d_model = x.shape
    depth = w_in.shape[0]
    d_in = w_in.shape[2]
    d_s5 = s5_glu_w.shape[1]
    rank, d_gk = gla_gate_up.shape[1:]
    d_gv = GLA_HEADS * gla_norm_w.shape[1]
    m = bsz * seq_len
    d_main = 2 * d_s5 + 2 * d_gk + 2 * d_gv
    tn = d_s5
    assert d_s5 % LANES == 0 and d_gk == GLA_HEADS * LANES and d_gv == GLA_HEADS * 2 * LANES
    assert d_in == d_main + rank and rank <= LANES and d_main % tn == 0 and 2 * d_gk == tn
    assert seq_len % (S5_T * 8) == 0 and w_out.shape[1] == 2 * d_s5

    n_s5 = d_s5 // LANES
    slab_z = 0
    slab_q = n_s5
    slab_k = slab_q + d_gk // LANES
    slab_v = slab_k + d_gk // LANES
    slab_gz = slab_v + d_gv // LANES

    resid = x.astype(F32).reshape(m, d_model)
    for l in range(depth):
        kb, u_op, v_op, abar = _s5_ops(s5_A_re[l], s5_A_im[l], s5_B_re[l], s5_B_im[l],
                                       s5_C_re[l], s5_C_im[l], s5_log_dt, s5_D, l)

        u2, proj, g_low = _in_proj(resid, pre_norm_w[l].astype(F32)[None],
                                   jnp.swapaxes(w_in, 1, 2), l, d_main=d_main, tm=512, tn=tn)

        g_slabs = _s5(u2, kb, u_op, v_op, abar, n_seq=bsz)

        y_gla = _gla(proj, g_low, gla_gate_up, gla_gate_bias, gla_norm_w, l,
                     n_seq=bsz, seq_len=seq_len, rows=512,
                     slab_q=slab_q, slab_k=slab_k, slab_v=slab_v, slab_gz=slab_gz)

        resid = _out_proj(g_slabs, proj, y_gla, resid, s5_glu_w, s5_glu_b[l].astype(F32)[None],
                          w_out, post_norm_w[l].astype(F32)[None], l, tm=512, slab_z=slab_z)
    return resid.reshape(bsz, seq_len, d_model).astype(x.dtype)
```
